```python
import jax
import jax.numpy as jnp
from jax import lax
import numpy as np

D_MODEL = 1024
BATCH = 8
SEQ = 4096
DEPTH = 2

ROPE_THETA = 10000.0

MLA_HEADS = 8
MLA_NOPE_DIM = 64
MLA_ROPE_DIM = 32
MLA_V_DIM = 64
MLA_Q_RANK = 768
MLA_KV_RANK = 256
Q_BLOCK = 128

NSA_HEADS = 8
NSA_GROUPS = 2
NSA_HEADS_PER_GROUP = NSA_HEADS // NSA_GROUPS
NSA_HEAD_DIM = 64
CMP_BLOCK = 32
CMP_STRIDE = 16
CMP_HIDDEN = 2 * NSA_HEAD_DIM
SEL_BLOCK = 64
SEL_TOPN = 16
WINDOW = 512
NSA_Q_CHUNK = 64
FORCE_SCORE = 1e9

PEER_HEADS = 8
PEER_KEY_DIM = 128
PEER_N_KEYS = 128
PEER_N_EXPERTS = PEER_N_KEYS * PEER_N_KEYS
PEER_TOPK = 16
PEER_TOKEN_CHUNK = 128

DEEPNORM_ALPHA = (2 * DEPTH) ** 0.25
DEEPNORM_BETA = (8 * DEPTH) ** -0.25
LN_EPS = 1e-5
RMS_EPS = 1e-6
NEG_BIG = -1e30

MLA_MIX = MLA_HEADS * MLA_V_DIM
NSA_MIX = NSA_HEADS * NSA_HEAD_DIM
IN_WIDTHS = (MLA_Q_RANK, MLA_KV_RANK, MLA_ROPE_DIM, NSA_MIX,
             6 * NSA_GROUPS * NSA_HEAD_DIM, 3 * NSA_HEADS, 2 * D_MODEL)
IN_WIDTH = sum(IN_WIDTHS)

kernel_name = "hybrid_mla_nsa_peer_deepnorm"


def layer_norm(x, g, b):
    xf = x.astype(jnp.float32)
    mu = jnp.mean(xf, axis=-1, keepdims=True)
    var = jnp.mean(jnp.square(xf - mu), axis=-1, keepdims=True)
    y = (xf - mu) * lax.rsqrt(var + LN_EPS)
    return (y * g.astype(jnp.float32) + b.astype(jnp.float32)).astype(x.dtype)


def rms_norm(x, g):
    xf = x.astype(jnp.float32)
    y = xf * lax.rsqrt(jnp.mean(jnp.square(xf), axis=-1, keepdims=True) + RMS_EPS)
    return (y * g.astype(jnp.float32)).astype(x.dtype)


def rope_angles(positions, dim):
    inv_freq = ROPE_THETA ** (-jnp.arange(0, dim, 2, dtype=jnp.float32) / dim)
    ang = positions.astype(jnp.float32)[..., None] * inv_freq
    return jnp.cos(ang), jnp.sin(ang)


def apply_rope(x, cos, sin):
    half = x.shape[-1] // 2
    c = cos[:, :, None, :].astype(x.dtype)
    s = sin[:, :, None, :].astype(x.dtype)
    x1, x2 = x[..., :half], x[..., half:]
    return jnp.concatenate([x1 * c - x2 * s, x1 * s + x2 * c], axis=-1)


def masked_softmax(s, mask):
    p = jax.nn.softmax(jnp.where(mask, s, NEG_BIG), axis=-1)
    return jnp.where(mask, p, 0.0)


def mla_attention(q_nope, q_rope, k_nope, k_rope, v):
    B, S, H, _ = q_nope.shape
    dv = v.shape[-1]
    scale = (MLA_NOPE_DIM + MLA_ROPE_DIM) ** -0.5
    kpos = jnp.arange(S)

    def block(i):
        q0 = i * Q_BLOCK
        qn = lax.dynamic_slice_in_dim(q_nope, q0, Q_BLOCK, axis=1)
        qr = lax.dynamic_slice_in_dim(q_rope, q0, Q_BLOCK, axis=1)
        s = (jnp.einsum('bqhd,bkhd->bhqk', qn, k_nope).astype(jnp.float32)
             + jnp.einsum('bqhd,bkd->bhqk', qr, k_rope).astype(jnp.float32)) * scale
        qpos = q0 + jnp.arange(Q_BLOCK)
        p = masked_softmax(s, kpos[None, :] <= qpos[:, None])
        return jnp.einsum('bhqk,bkhd->bqhd', p.astype(v.dtype), v)

    out = lax.map(block, jnp.arange(S // Q_BLOCK))
    return jnp.moveaxis(out, 0, 1).reshape(B, S, H * dv)


def compress_blocks(k, pe, w1, w2):
    B, S, G, dk = k.shape
    n_cmp = (S - CMP_BLOCK) // CMP_STRIDE + 1
    idx = jnp.arange(n_cmp)[:, None] * CMP_STRIDE + jnp.arange(CMP_BLOCK)[None, :]
    blocks = k[:, idx] + pe[None, None, :, None, :]
    blocks = blocks.transpose(0, 1, 3, 2, 4).reshape(B, n_cmp, G, CMP_BLOCK * dk)
    hid = jax.nn.gelu(blocks @ w1, approximate=False)
    return hid @ w2


def nsa_attention(q, kc, vc, k_sel, v_sel, k_win, v_win, gates):
    B, S, H, dk = q.shape
    G, R, C = NSA_GROUPS, NSA_HEADS_PER_GROUP, NSA_Q_CHUNK
    scale = dk ** -0.5
    qg = q.reshape(B, S, G, R, dk)
    gg = gates.reshape(B, S, G, R, 3)
    n_cmp = kc.shape[1]
    n_sel = S // SEL_BLOCK
    top_n = min(SEL_TOPN, n_sel)

    cmp_lo = jnp.arange(n_cmp) * CMP_STRIDE
    cmp_end = cmp_lo + CMP_BLOCK - 1
    sel_lo = jnp.arange(n_sel) * SEL_BLOCK
    overlap = ((cmp_lo[:, None] <= sel_lo[None, :] + SEL_BLOCK - 1)
               & (cmp_end[:, None] >= sel_lo[None, :])).astype(jnp.float32)

    kb = k_sel.reshape(B, n_sel, SEL_BLOCK, G, dk).transpose(0, 3, 1, 2, 4)
    vb = v_sel.reshape(B, n_sel, SEL_BLOCK, G, dk).transpose(0, 3, 1, 2, 4)
    kw_pad = jnp.pad(k_win, ((0, 0), (WINDOW, 0), (0, 0), (0, 0)))
    vw_pad = jnp.pad(v_win, ((0, 0), (WINDOW, 0), (0, 0), (0, 0)))
    b_ix = jnp.arange(B)[:, None, None, None]
    g_ix = jnp.arange(G)[None, :, None, None]
    blk = jnp.arange(n_sel)

    def chunk(i):
        q0 = i * C
        qc = lax.dynamic_slice_in_dim(qg, q0, C, axis=1)
        gc = lax.dynamic_slice_in_dim(gg, q0, C, axis=1)
        t = q0 + jnp.arange(C)

        s_c = jnp.einsum('bcgrd,bngd->bgrcn', qc, kc).astype(jnp.float32) * scale
        p_c = masked_softmax(s_c, cmp_end[None, :] <= t[:, None])
        o_c = jnp.einsum('bgrcn,bngd->bcgrd', p_c.astype(qc.dtype), vc)

        imp = jnp.einsum('bgrcn,nj->bgcj', p_c, overlap)
        cur = t // SEL_BLOCK
        forced = (blk[None, :] == 0) | (blk[None, :] == cur[:, None]) | (blk[None, :] == cur[:, None] - 1)
        future = blk[None, :] > cur[:, None]
        imp = jnp.where(forced, FORCE_SCORE, jnp.where(future, -FORCE_SCORE, imp))
        _, idx = lax.top_k(imp, top_n)

        kg = kb[b_ix, g_ix, idx]
        vg = vb[b_ix, g_ix, idx]
        tok = idx[..., None] * SEL_BLOCK + jnp.arange(SEL_BLOCK)
        mask_s = (tok <= t[None, None, :, None, None]).reshape(B, G, 1, C, top_n * SEL_BLOCK)
        s_s = jnp.einsum('bcgrd,bgcnld->bgrcnl', qc, kg).astype(jnp.float32) * scale
        p_s = masked_softmax(s_s.reshape(B, G, R, C, top_n * SEL_BLOCK), mask_s)
        p_s = p_s.reshape(B, G, R, C, top_n, SEL_BLOCK)
        o_s = jnp.einsum('bgrcnl,bgcnld->bcgrd', p_s.astype(qc.dtype), vg)

        kw = lax.dynamic_slice_in_dim(kw_pad, q0, C + WINDOW, axis=1)
        vw = lax.dynamic_slice_in_dim(vw_pad, q0, C + WINDOW, axis=1)
        kpos = q0 - WINDOW + jnp.arange(C + WINDOW)
        mask_w = ((kpos[None, :] <= t[:, None]) & (kpos[None, :] > t[:, None] - WINDOW)
                  & (kpos[None, :] >= 0))
        s_w = jnp.einsum('bcgrd,bkgd->bgrck', qc, kw).astype(jnp.float32) * scale
        p_w = masked_softmax(s_w, mask_w)
        o_w = jnp.einsum('bgrck,bkgd->bcgrd', p_w.astype(qc.dtype), vw)

        return gc[..., 0:1] * o_c + gc[..., 1:2] * o_s + gc[..., 2:3] * o_w

    out = lax.map(chunk, jnp.arange(S // C))
    return jnp.moveaxis(out, 0, 1).reshape(B, S, H * dk)


def peer(x, w_query, sub_keys, expert_u, expert_v):
    B, S, D = x.shape
    T = B * S
    H, K = PEER_HEADS, PEER_TOPK
    xt = x.reshape(T, D)
    q = (xt @ w_query).reshape(T, H, 2, PEER_KEY_DIM // 2)
    s = jnp.einsum('thpd,hpnd->thpn', q, sub_keys).astype(jnp.float32)
    sv, si = lax.top_k(s, K)
    cand = (sv[:, :, 0, :, None] + sv[:, :, 1, None, :]).reshape(T, H, K * K)
    cidx = (si[:, :, 0, :, None] * PEER_N_KEYS + si[:, :, 1, None, :]).reshape(T, H, K * K)
    best, pos = lax.top_k(cand, K)
    eidx = jnp.take_along_axis(cidx, pos, axis=-1)
    gate = jax.nn.softmax(best, axis=-1).astype(x.dtype)
    n_chunks = T // PEER_TOKEN_CHUNK

    def chunk(args):
        xc, ec, gc = args
        a = jnp.einsum('chkd,cd->chk', expert_u[ec], xc)
        w = gc * jax.nn.gelu(a, approximate=False)
        return jnp.einsum('chk,chkd->cd', w, expert_v[ec])

    y = lax.map(chunk, (xt.reshape(n_chunks, PEER_TOKEN_CHUNK, D),
                        eidx.reshape(n_chunks, PEER_TOKEN_CHUNK, H, K),
                        gate.reshape(n_chunks, PEER_TOKEN_CHUNK, H, K)))
    return y.reshape(B, S, D)


def hybrid_layer(x, rope_mla, rope_nsa, w_in, mla_q_norm, mla_w_uq, mla_kv_norm, mla_w_ukv,
                 ck_pe, ck_w1, ck_w2, cv_pe, cv_w1, cv_w2, w_branch_mla, w_branch_nsa, w_out,
                 ln1_g, ln1_b, peer_w_query, peer_sub_keys, peer_u, peer_v, ln2_g, ln2_b):
    B, S, _ = x.shape
    offsets = [int(o) for o in np.cumsum(IN_WIDTHS)[:-1]]
    c_q, c_kv, k_r, n_q, n_kv, n_gate, m_gate = jnp.split(x @ w_in, offsets, axis=-1)

    q = (rms_norm(c_q, mla_q_norm) @ mla_w_uq).reshape(B, S, MLA_HEADS, MLA_NOPE_DIM + MLA_ROPE_DIM)
    q_nope = q[..., :MLA_NOPE_DIM]
    q_rope = apply_rope(q[..., MLA_NOPE_DIM:], *rope_mla)
    kv = (rms_norm(c_kv, mla_kv_norm) @ mla_w_ukv).reshape(B, S, MLA_HEADS, MLA_NOPE_DIM + MLA_V_DIM)
    k_nope, v = kv[..., :MLA_NOPE_DIM], kv[..., MLA_NOPE_DIM:]
    k_rope = apply_rope(k_r[:, :, None, :], *rope_mla)[:, :, 0, :]
    y_mla = mla_attention(q_nope, q_rope, k_nope, k_rope, v) @ w_branch_mla

    qn = apply_rope(n_q.reshape(B, S, NSA_HEADS, NSA_HEAD_DIM), *rope_nsa)
    kvs = n_kv.reshape(B, S, 6, NSA_GROUPS, NSA_HEAD_DIM)
    k_cmp = apply_rope(kvs[:, :, 0], *rope_nsa)
    k_sel = apply_rope(kvs[:, :, 2], *rope_nsa)
    k_win = apply_rope(kvs[:, :, 4], *rope_nsa)
    kc = compress_blocks(k_cmp, ck_pe, ck_w1, ck_w2)
    vc = compress_blocks(kvs[:, :, 1], cv_pe, cv_w1, cv_w2)
    nsa_gates = jax.nn.sigmoid(n_gate.reshape(B, S, NSA_HEADS, 3))
    y_nsa = nsa_attention(qn, kc, vc, k_sel, kvs[:, :, 3], k_win, kvs[:, :, 5], nsa_gates) @ w_branch_nsa

    g_mla, g_nsa = jnp.split(jax.nn.sigmoid(m_gate), 2, axis=-1)
    mixed = (g_mla * y_mla + g_nsa * y_nsa) @ w_out
    x = layer_norm(DEEPNORM_ALPHA * x + mixed, ln1_g, ln1_b)

    x = layer_norm(DEEPNORM_ALPHA * x + peer(x, peer_w_query, peer_sub_keys, peer_u, peer_v), ln2_g, ln2_b)
    return x


def setup_inputs(seed: int = 0) -> dict:
    key = jax.random.key(seed)
    ks = jax.random.split(key, 28)
    L, D = DEPTH, D_MODEL
    f32 = jnp.float32

    def nrm(k, shape, scale):
        return jax.random.normal(k, shape, f32) * scale

    def gain(k, shape):
        return 1.0 + nrm(k, shape, 0.02)

    offsets = jax.random.randint(ks[1], (BATCH, 1), 0, 1024, dtype=jnp.int32)
    cmp_in = CMP_BLOCK * NSA_HEAD_DIM
    return {
        'x': nrm(ks[0], (BATCH, SEQ, D), 1.0),
        'positions': offsets + jnp.arange(SEQ, dtype=jnp.int32)[None, :],
        'ln_in_g': gain(ks[2], (D,)),
        'ln_in_b': nrm(ks[3], (D,), 0.02),
        'w_in': nrm(ks[4], (L, D, IN_WIDTH), D ** -0.5),
        'mla_q_norm': gain(ks[5], (L, MLA_Q_RANK)),
        'mla_w_uq': nrm(ks[6], (L, MLA_Q_RANK, MLA_HEADS * (MLA_NOPE_DIM + MLA_ROPE_DIM)), MLA_Q_RANK ** -0.5),
        'mla_kv_norm': gain(ks[7], (L, MLA_KV_RANK)),
        'mla_w_ukv': nrm(ks[8], (L, MLA_KV_RANK, MLA_HEADS * (MLA_NOPE_DIM + MLA_V_DIM)), MLA_KV_RANK ** -0.5),
        'nsa_cmp_k_pe': nrm(ks[9], (L, CMP_BLOCK, NSA_HEAD_DIM), 0.02),
        'nsa_cmp_k_w1': nrm(ks[10], (L, cmp_in, CMP_HIDDEN), cmp_in ** -0.5),
        'nsa_cmp_k_w2': nrm(ks[11], (L, CMP_HIDDEN, NSA_HEAD_DIM), CMP_HIDDEN ** -0.5),
        'nsa_cmp_v_pe': nrm(ks[12], (L, CMP_BLOCK, NSA_HEAD_DIM), 0.02),
        'nsa_cmp_v_w1': nrm(ks[13], (L, cmp_in, CMP_HIDDEN), cmp_in ** -0.5),
        'nsa_cmp_v_w2': nrm(ks[14], (L, CMP_HIDDEN, NSA_HEAD_DIM), CMP_HIDDEN ** -0.5),
        'w_branch_mla': nrm(ks[15], (L, MLA_MIX, D), DEEPNORM_BETA * MLA_MIX ** -0.5),
        'w_branch_nsa': nrm(ks[16], (L, NSA_MIX, D), DEEPNORM_BETA * NSA_MIX ** -0.5),
        'w_out': nrm(ks[17], (L, D, D), DEEPNORM_BETA * D ** -0.5),
        'ln1_g': gain(ks[18], (L, D)),
        'ln1_b': nrm(ks[19], (L, D), 0.02),
        'peer_w_query': nrm(ks[20], (L, D, PEER_HEADS * PEER_KEY_DIM), D ** -0.5),
        'peer_sub_keys': nrm(ks[21], (L, PEER_HEADS, 2, PEER_N_KEYS, PEER_KEY_DIM // 2), (PEER_KEY_DIM // 2) ** -0.5),
        'peer_u': nrm(ks[22], (L, PEER_N_EXPERTS, D), D ** -0.5),
        'peer_v': nrm(ks[23], (L, PEER_N_EXPERTS, D), DEEPNORM_BETA),
        'ln2_g': gain(ks[24], (L, D)),
        'ln2_b': nrm(ks[25], (L, D), 0.02),
    }


def reference(x, positions, ln_in_g, ln_in_b, w_in, mla_q_norm, mla_w_uq, mla_kv_norm, mla_w_ukv,
              nsa_cmp_k_pe, nsa_cmp_k_w1, nsa_cmp_k_w2, nsa_cmp_v_pe, nsa_cmp_v_w1, nsa_cmp_v_w2,
              w_branch_mla, w_branch_nsa, w_out, ln1_g, ln1_b,
              peer_w_query, peer_sub_keys, peer_u, peer_v, ln2_g, ln2_b):
    rope_mla = rope_angles(positions, MLA_ROPE_DIM)
    rope_nsa = rope_angles(positions, NSA_HEAD_DIM)
    h = layer_norm(x, ln_in_g, ln_in_b)
    for l in range(DEPTH):
        h = hybrid_layer(h, rope_mla, rope_nsa, w_in[l], mla_q_norm[l], mla_w_uq[l], mla_kv_norm[l], mla_w_ukv[l],
                         nsa_cmp_k_pe[l], nsa_cmp_k_w1[l], nsa_cmp_k_w2[l],
                         nsa_cmp_v_pe[l], nsa_cmp_v_w1[l], nsa_cmp_v_w2[l],
                         w_branch_mla[l], w_branch_nsa[l], w_out[l], ln1_g[l], ln1_b[l],
                         peer_w_query[l], peer_sub_keys[l], peer_u[l], peer_v[l], ln2_g[l], ln2_b[l])
    return h
```

```python
import functools
import math

import numpy as np
import jax
import jax.numpy as jnp
from jax import lax
from jax.experimental import pallas as pl
from jax.experimental.pallas import tpu as pltpu

D_MODEL = 1024
DEPTH = 2
ROPE_THETA = 10000.0

MLA_HEADS = 8
MLA_NOPE_DIM = 64
MLA_ROPE_DIM = 32
MLA_V_DIM = 64
MLA_Q_RANK = 768
MLA_KV_RANK = 256

NSA_HEADS = 8
NSA_GROUPS = 2
NSA_HPG = NSA_HEADS // NSA_GROUPS
NSA_HEAD_DIM = 64
CMP_BLOCK = 32
CMP_STRIDE = 16
CMP_HIDDEN = 2 * NSA_HEAD_DIM
SEL_BLOCK = 64
SEL_TOPN = 16
WINDOW = 512
FORCE_SCORE = 1e9

PEER_HEADS = 8
PEER_KEY_DIM = 128
PEER_N_KEYS = 128
PEER_N_EXPERTS = PEER_N_KEYS * PEER_N_KEYS
PEER_TOPK = 16

DEEPNORM_ALPHA = (2 * DEPTH) ** 0.25
LN_EPS = 1e-5
RMS_EPS = 1e-6
NEG_BIG = -1e30

IN_WIDTHS = (MLA_Q_RANK, MLA_KV_RANK, MLA_ROPE_DIM, NSA_HEADS * NSA_HEAD_DIM,
             6 * NSA_GROUPS * NSA_HEAD_DIM, 3 * NSA_HEADS, 2 * D_MODEL)

LANES = 128
VMEM_LIMIT = 56 * 1024 * 1024

F32 = jnp.float32
CDT = jnp.bfloat16

_NT = (((1,), (1,)), ((), ()))


def _dot(a, b):
    return jnp.dot(a, b, preferred_element_type=F32)


def _dot_nt(a, b):
    return lax.dot_general(a, b, _NT, preferred_element_type=F32)


def _params(*sem):
    return pltpu.CompilerParams(dimension_semantics=sem, vmem_limit_bytes=VMEM_LIMIT)


def _gelu(x):
    return 0.5 * x * (1.0 + lax.erf(x * (2.0 ** -0.5)))


def _layer_norm(z, g, b):
    mu = jnp.mean(z, axis=-1, keepdims=True)
    d = z - mu
    var = jnp.mean(d * d, axis=-1, keepdims=True)
    return d * lax.rsqrt(var + LN_EPS) * g + b


def _ln_in_kernel(x_ref, g_ref, b_ref, h_ref, hb_ref):
    h = _layer_norm(x_ref[...], g_ref[...], b_ref[...])
    h_ref[...] = h
    hb_ref[...] = h.astype(CDT)


def _ln_in(x, g, b, tm=512):
    T, D = x.shape
    row = pl.BlockSpec((tm, D), lambda i: (i, 0))
    vec = pl.BlockSpec((1, D), lambda i: (0, 0))
    return pl.pallas_call(
        _ln_in_kernel, grid=(T // tm,), in_specs=[row, vec, vec], out_specs=[row, row],
        out_shape=[jax.ShapeDtypeStruct((T, D), F32), jax.ShapeDtypeStruct((T, D), CDT)],
        compiler_params=_params("parallel"), name="ln_in")(x, g.reshape(1, D), b.reshape(1, D))


def _proj_kernel(*refs, rope, act, n_rep):
    if rope:
        hb_ref, w_ref, wr_ref, cos_ref, sin_ref, o_ref = refs
    else:
        hb_ref, w_ref, o_ref = refs
    hb = hb_ref[...]
    y = _dot(hb, w_ref[...])
    if rope:
        yr = _dot(hb, wr_ref[...])
        c, s = cos_ref[...], sin_ref[...]
        if n_rep > 1:
            c = jnp.concatenate([c] * n_rep, axis=1)
            s = jnp.concatenate([s] * n_rep, axis=1)
        y = y * c + yr * s
    if act == "sigmoid":
        y = 1.0 / (1.0 + jnp.exp(-y))
    o_ref[...] = y.astype(o_ref.dtype)


def _proj(hb, w, out_dtype, w_rot=None, tables=None, act=None, tm=1024, name="proj"):
    T, K = hb.shape
    N = w.shape[1]
    tn = N if N <= 768 else 512
    assert N % tn == 0 and tn % LANES == 0 and T % tm == 0
    rope = w_rot is not None
    x_spec = pl.BlockSpec((tm, K), lambda j, i: (i, 0))
    w_spec = pl.BlockSpec((K, tn), lambda j, i: (0, j))
    t_spec = pl.BlockSpec((tm, LANES), lambda j, i: (i, 0))
    o_spec = pl.BlockSpec((tm, tn), lambda j, i: (i, j))
    if rope:
        args = (hb, w, w_rot, tables[0], tables[1])
        in_specs = [x_spec, w_spec, w_spec, t_spec, t_spec]
    else:
        args = (hb, w)
        in_specs = [x_spec, w_spec]
    kern = functools.partial(_proj_kernel, rope=rope, act=act, n_rep=tn // LANES)
    return pl.pallas_call(
        kern, grid=(N // tn, T // tm), in_specs=in_specs, out_specs=o_spec,
        out_shape=jax.ShapeDtypeStruct((T, N), out_dtype),
        compiler_params=_params("parallel", "parallel"), name=name)(*args)


def _rms(x, g):
    return x * lax.rsqrt(jnp.mean(x * x, axis=-1, keepdims=True) + RMS_EPS) * g


def _mla_up_kernel(c_ref, kr_ref, qg_ref, kvg_ref, wqn_ref, wqr_ref, wqrr_ref, pq_ref,
                   wkn_ref, wv_ref, pk_ref, cos_ref, sin_ref, q_ref, k_ref, v_ref):
    c = c_ref[...].astype(F32)
    cqn = _rms(c[:, :MLA_Q_RANK], qg_ref[...]).astype(CDT)
    ckvn = _rms(c[:, MLA_Q_RANK:], kvg_ref[...]).astype(CDT)
    cos = jnp.concatenate([cos_ref[...]] * 2, axis=1)
    sin = jnp.concatenate([sin_ref[...]] * 2, axis=1)
    roped = (_dot(cqn, wqr_ref[...]) * cos + _dot(cqn, wqrr_ref[...]) * sin).astype(CDT)
    q_ref[...] = (_dot(cqn, wqn_ref[...]) + _dot(roped, pq_ref[...])).astype(CDT)
    k_ref[...] = (_dot(ckvn, wkn_ref[...]) + _dot(kr_ref[...], pk_ref[...])).astype(CDT)
    v_ref[...] = _dot(ckvn, wv_ref[...]).astype(CDT)


def _mla_up(cqkv, kr, lw, tables, tm=512):
    T = cqkv.shape[0]
    row = lambda n: pl.BlockSpec((tm, n), lambda i: (i, 0))
    full = lambda a: pl.BlockSpec(a.shape, lambda i: (0, 0))
    ws = (lw["q_gain"], lw["kv_gain"], lw["wqn"], lw["wqr"], lw["wqr_rot"], lw["place_q"],
          lw["wkn"], lw["wv"], lw["place_k"])
    hp = MLA_HEADS * LANES
    return pl.pallas_call(
        _mla_up_kernel, grid=(T // tm,),
        in_specs=[row(cqkv.shape[1]), row(LANES)] + [full(a) for a in ws] + [row(LANES), row(LANES)],
        out_specs=[row(hp), row(hp), row(MLA_HEADS * MLA_V_DIM)],
        out_shape=[jax.ShapeDtypeStruct((T, hp), CDT), jax.ShapeDtypeStruct((T, hp), CDT),
                   jax.ShapeDtypeStruct((T, MLA_HEADS * MLA_V_DIM), CDT)],
        compiler_params=_params("parallel"), name="mla_up")(cqkv, kr, *ws, tables[0], tables[1])


def _mla_attn_kernel(q_ref, k_ref, v_ref, o_ref, *, tq, scale):
    i = pl.program_id(2)
    outs = []
    for hd in range(2):
        q = q_ref[0, :, hd * LANES:(hd + 1) * LANES]

        def step(kt, carry, diagonal, hd=hd, q=q):
            m, l, acc = carry
            k0 = pl.multiple_of(kt * tq, tq)
            k = k_ref[0, pl.ds(k0, tq), hd * LANES:(hd + 1) * LANES]
            v = v_ref[0, pl.ds(k0, tq), :]
            s = _dot_nt(q, k) * scale
            if diagonal:
                row = lax.broadcasted_iota(jnp.int32, (tq, tq), 0)
                col = lax.broadcasted_iota(jnp.int32, (tq, tq), 1)
                s = jnp.where(col <= row, s, NEG_BIG)
            m_new = jnp.maximum(m, jnp.max(s, axis=1, keepdims=True))
            alpha = jnp.exp(m - m_new)
            p = jnp.exp(s - m_new)
            l = l * alpha + jnp.sum(p, axis=1, keepdims=True)
            acc = acc * alpha + _dot(p.astype(CDT), v)
            return m_new, l, acc

        init = (jnp.full((tq, 1), NEG_BIG, F32), jnp.zeros((tq, 1), F32), jnp.zeros((tq, LANES), F32))
        carry = lax.fori_loop(0, i, lambda kt, c: step(kt, c, False), init)
        _, l, acc = step(i, carry, True)
        outs.append(acc / l)
    lane = lax.broadcasted_iota(jnp.int32, (tq, LANES), 1)
    o_ref[0] = jnp.where(lane < MLA_V_DIM, outs[0], outs[1]).astype(o_ref.dtype)


def _mla_attn(q, k, v, tq=512):
    B, S, _ = q.shape
    scale = (MLA_NOPE_DIM + MLA_ROPE_DIM) ** -0.5
    kern = functools.partial(_mla_attn_kernel, tq=tq, scale=scale)
    return pl.pallas_call(
        kern, grid=(B, MLA_HEADS // 2, S // tq),
        in_specs=[pl.BlockSpec((1, tq, 2 * LANES), lambda b, h, i: (b, i, h)),
                  pl.BlockSpec((1, S, 2 * LANES), lambda b, h, i: (b, 0, h)),
                  pl.BlockSpec((1, S, LANES), lambda b, h, i: (b, 0, h))],
        out_specs=pl.BlockSpec((1, tq, LANES), lambda b, h, i: (b, i, h)),
        out_shape=jax.ShapeDtypeStruct((B, S, MLA_HEADS * MLA_V_DIM), CDT),
        compiler_params=_params("parallel", "parallel", "arbitrary"), name="mla_attn")(q, k, v)


def _compress_kernel(c_ref, pe_ref, w1a_ref, w1b_ref, w2_ref, o_ref, *, n_valid):
    c = c_ref[0].astype(F32)
    n = c.shape[0]
    a = _dot((c + pe_ref[0:1, :]).astype(CDT), w1a_ref[...])
    b = _dot((c + pe_ref[1:2, :]).astype(CDT), w1b_ref[...])
    hid = _gelu(a + pltpu.roll(b, n - 1, 0))
    out = _dot(hid.astype(CDT), w2_ref[...])
    row = lax.broadcasted_iota(jnp.int32, out.shape, 0)
    o_ref[0] = jnp.where(row < n_valid, out, 0.0).astype(o_ref.dtype)


def _compress(chunks, pe, w1, w2):
    BG, n, width = chunks.shape
    pe2 = pe.reshape(2, width).astype(F32)
    w1a = w1[:width].astype(CDT)
    w1b = w1[width:].astype(CDT)
    w2p = jnp.pad(w2, ((0, 0), (0, LANES - w2.shape[1]))).astype(CDT)
    full = lambda a: pl.BlockSpec(a.shape, lambda i: (0, 0))
    return pl.pallas_call(
        functools.partial(_compress_kernel, n_valid=n - 1), grid=(BG,),
        in_specs=[pl.BlockSpec((1, n, width), lambda i: (i, 0, 0)), full(pe2), full(w1a), full(w1b), full(w2p)],
        out_specs=pl.BlockSpec((1, n, LANES), lambda i: (i, 0, 0)),
        out_shape=jax.ShapeDtypeStruct((BG, n, LANES), CDT),
        compiler_params=_params("parallel"), name="nsa_compress")(chunks, pe2, w1a, w1b, w2p)


def _nsa_kernel(q_ref, kc_ref, vc_ref, ks_ref, vs_ref, kw_ref, vw_ref, g_ref, ovl_ref, o_ref,
                *, tq, n_sel, n_cmp, scale):
    i = pl.program_id(2)
    q0 = i * tq
    R = NSA_HPG
    ncp = kc_ref.shape[1]
    Q = jnp.concatenate([q_ref[0, :, r * LANES:(r + 1) * LANES] for r in range(R)], axis=0)

    s = (_dot_nt(Q, kc_ref[0]) * scale).reshape(R, tq, ncp)
    t_c = q0 + lax.broadcasted_iota(jnp.int32, (tq, ncp), 0)
    n_c = lax.broadcasted_iota(jnp.int32, (tq, ncp), 1)
    mask_c = ((n_c * CMP_STRIDE + (CMP_BLOCK - 1) <= t_c) & (n_c < n_cmp))[None]
    s = jnp.where(mask_c, s, NEG_BIG)
    p = jnp.where(mask_c, jnp.exp(s - jnp.max(s, axis=2, keepdims=True)), 0.0)
    l = jnp.sum(p, axis=2, keepdims=True)
    p = p * jnp.where(l > 0.0, 1.0 / l, 0.0)
    o_c = _dot(p.reshape(R * tq, ncp).astype(CDT), vc_ref[0])

    psum = jnp.sum(p, axis=0)
    p_hi = psum.astype(CDT)
    p_lo = (psum - p_hi.astype(F32)).astype(CDT)
    imp = (_dot_nt(ovl_ref[...], p_hi) + _dot_nt(ovl_ref[...], p_lo))[0:n_sel]
    jj = lax.broadcasted_iota(jnp.int32, (n_sel, tq), 0)
    cur = (q0 + lax.broadcasted_iota(jnp.int32, (n_sel, tq), 1)) // SEL_BLOCK
    forced = (jj == 0) | (jj == cur) | (jj == cur - 1)
    val = jnp.where(forced, FORCE_SCORE, jnp.where(jj > cur, -FORCE_SCORE, imp))
    rank = jnp.zeros((n_sel, tq), F32)
    for jp in range(n_sel):
        rowv = val[jp:jp + 1, :]
        gt = jnp.where(rowv > val, 1.0, 0.0)
        ge = jnp.where(rowv >= val, 1.0, 0.0)
        rank = rank + jnp.where(jj > jp, ge, gt)
    top_n = min(SEL_TOPN, n_sel)
    sel_t = jnp.where((rank < top_n) & (jj <= cur), 1.0, 0.0)
    if n_sel < LANES:
        sel_t = jnp.concatenate([sel_t, jnp.zeros((LANES - n_sel, tq), F32)], axis=0)
    sel = sel_t.T.astype(CDT)

    def attend(k_ref, v_ref, lo, hi, mask_fn):
        def step(kt, carry):
            m, l, acc = carry
            k0 = pl.multiple_of(kt * tq, tq)
            kb = k_ref[0, pl.ds(k0, tq), :]
            vb = v_ref[0, pl.ds(k0, tq), :]
            mask = mask_fn(kt)[None]
            s = jnp.where(mask, (_dot_nt(Q, kb) * scale).reshape(R, tq, tq), NEG_BIG)
            m_new = jnp.maximum(m, jnp.max(s, axis=2, keepdims=True))
            alpha = jnp.exp(m - m_new)
            p = jnp.where(mask, jnp.exp(s - m_new), 0.0)
            l = l * alpha + jnp.sum(p, axis=2, keepdims=True)
            pv = _dot(p.reshape(R * tq, tq).astype(CDT), vb).reshape(R, tq, LANES)
            return m_new, l, acc * alpha + pv
        init = (jnp.full((R, tq, 1), NEG_BIG, F32), jnp.zeros((R, tq, 1), F32), jnp.zeros((R, tq, LANES), F32))
        _, l, acc = lax.fori_loop(lo, hi, step, init)
        return acc * jnp.where(l > 0.0, 1.0 / l, 0.0)

    row = lax.broadcasted_iota(jnp.int32, (tq, tq), 0)
    col = lax.broadcasted_iota(jnp.int32, (tq, tq), 1)
    blk_per_tile = tq // SEL_BLOCK

    def mask_sel(kt):
        ej = lax.broadcasted_iota(jnp.int32, (LANES, tq), 0)
        ec = lax.broadcasted_iota(jnp.int32, (LANES, tq), 1)
        expand = jnp.where(ej == kt * blk_per_tile + ec // SEL_BLOCK, 1.0, 0.0).astype(CDT)
        chosen = _dot(sel, expand) > 0.5
        return chosen & (kt * tq + col <= q0 + row)

    def mask_win(kt):
        d = (q0 + row) - (kt * tq + col)
        return (d >= 0) & (d < WINDOW)

    o_s = attend(ks_ref, vs_ref, 0, i + 1, mask_sel)
    o_w = attend(kw_ref, vw_ref, jnp.maximum(i - WINDOW // tq, 0), i + 1, mask_win)

    g = g_ref[0]
    o_c = o_c.reshape(R, tq, LANES)
    for r in range(R):
        o = (g[:, 3 * r:3 * r + 1] * o_c[r] + g[:, 3 * r + 1:3 * r + 2] * o_s[r]
             + g[:, 3 * r + 2:3 * r + 3] * o_w[r])
        o_ref[0, :, r * LANES:(r + 1) * LANES] = o.astype(o_ref.dtype)


def _nsa(q, kc, vc, ks, vs, kw, vw, gates, tq=256):
    B, S, _ = q.shape
    G = NSA_GROUPS
    n_sel = S // SEL_BLOCK
    n_cmp = (S - CMP_BLOCK) // CMP_STRIDE + 1
    ncp = kc.shape[1]
    assert n_sel <= LANES and n_sel % 8 == 0 and ncp % LANES == 0 and WINDOW % tq == 0
    jn = np.arange(LANES)[:, None] * SEL_BLOCK
    cn = np.arange(ncp)[None, :] * CMP_STRIDE
    ovl = ((cn <= jn + SEL_BLOCK - 1) & (cn + CMP_BLOCK - 1 >= jn)
           & (np.arange(LANES)[:, None] < n_sel) & (np.arange(ncp)[None, :] < n_cmp))
    ovl = jnp.asarray(ovl, CDT)
    kern = functools.partial(_nsa_kernel, tq=tq, n_sel=n_sel, n_cmp=n_cmp, scale=NSA_HEAD_DIM ** -0.5)
    cmp_spec = pl.BlockSpec((1, ncp, LANES), lambda b, g, i: (b * G + g, 0, 0))
    kv_spec = pl.BlockSpec((1, S, LANES), lambda b, g, i: (b, 0, g))
    qo_spec = pl.BlockSpec((1, tq, NSA_HPG * LANES), lambda b, g, i: (b, i, g))
    return pl.pallas_call(
        kern, grid=(B, G, S // tq),
        in_specs=[qo_spec, cmp_spec, cmp_spec, kv_spec, kv_spec, kv_spec, kv_spec,
                  pl.BlockSpec((1, tq, LANES), lambda b, g, i: (b, i, g)),
                  pl.BlockSpec(ovl.shape, lambda b, g, i: (0, 0))],
        out_specs=qo_spec,
        out_shape=jax.ShapeDtypeStruct((B, S, NSA_HEADS * LANES), CDT),
        compiler_params=_params("parallel", "parallel", "arbitrary"), name="nsa_attn")(
            q, kc, vc, ks, vs, kw, vw, gates, ovl)


def _merge_kernel(om_ref, on_ref, mg_ref, h_ref, wbm_ref, wbn_ref, wo_ref, g_ref, b_ref, x_ref, xb_ref):
    D = D_MODEL
    y_mla = _dot(om_ref[...], wbm_ref[...])
    y_nsa = _dot(on_ref[...], wbn_ref[...])
    mg = mg_ref[...].astype(F32)
    mixed = _dot((mg[:, :D] * y_mla + mg[:, D:] * y_nsa).astype(CDT), wo_ref[...])
    x = _layer_norm(DEEPNORM_ALPHA * h_ref[...] + mixed, g_ref[...], b_ref[...])
    x_ref[...] = x
    xb_ref[...] = x.astype(CDT)


def _merge(o_mla, o_nsa, mg, h, lw, tm=512):
    T, D = h.shape
    row = lambda n: pl.BlockSpec((tm, n), lambda i: (i, 0))
    full = lambda a: pl.BlockSpec(a.shape, lambda i: (0, 0))
    ws = (lw["w_bm"], lw["w_bn"], lw["w_out"], lw["ln1_g"], lw["ln1_b"])
    return pl.pallas_call(
        _merge_kernel, grid=(T // tm,),
        in_specs=[row(o_mla.shape[1]), row(o_nsa.shape[1]), row(2 * D), row(D)] + [full(a) for a in ws],
        out_specs=[row(D), row(D)],
        out_shape=[jax.ShapeDtypeStruct((T, D), F32), jax.ShapeDtypeStruct((T, D), CDT)],
        compiler_params=_params("parallel"), name="merge_ln1")(o_mla, o_nsa, mg, h, *ws)


_PEER_PAIRS = [(a, b) for a in range(PEER_TOPK) for b in range(PEER_TOPK) if (a + 1) * (b + 1) <= PEER_TOPK]
_N_CAND = -(-len(_PEER_PAIRS) // 8) * 8


def _top_values(v, want_rank):
    tops = []
    rank = jnp.full(v.shape, 127.0, F32) if want_rank else None
    for r in range(PEER_TOPK):
        m = jnp.max(v, axis=0, keepdims=True)
        tops.append(m)
        eq = v == m
        if want_rank:
            rank = jnp.where(eq, float(r), rank)
        v = jnp.where(eq, -jnp.inf, v)
    return tops, rank


def _peer_stats_kernel(xb_ref, wq_ref, keys_ref, c1_ref, e1_ref, r2_ref, e2_ref, s_ref, cand_ref):
    qp = _dot(xb_ref[...], wq_ref[...]).astype(CDT)
    s_ref[...] = _dot_nt(keys_ref[...], qp)
    n = PEER_N_KEYS
    cand_ref[...] = jnp.full(cand_ref.shape, -jnp.inf, F32)
    for h in range(PEER_HEADS):
        s1 = s_ref[(2 * h) * n:(2 * h + 1) * n, :]
        s2 = s_ref[(2 * h + 1) * n:(2 * h + 2) * n, :]
        t1, _ = _top_values(s1, False)
        t2, rank2 = _top_values(s2, True)
        for c, (a, b) in enumerate(_PEER_PAIRS):
            cand_ref[c:c + 1, :] = t1[a] + t2[b]
        cand = cand_ref[...]
        tc, _ = _top_values(cand, False)
        tau = tc[PEER_TOPK - 1]
        top = t1[0] + t2[0]
        z = jnp.sum(jnp.where(cand >= tau, jnp.exp(cand - top), 0.0), axis=0, keepdims=True)
        cnt = jnp.zeros(s1.shape, F32)
        for b in range(PEER_TOPK):
            cnt = cnt + jnp.where(s1 + t2[b] >= tau, 1.0, 0.0)
        c1_ref[h] = cnt
        e1_ref[h] = jnp.exp(s1 - t1[0])
        r2_ref[h] = rank2
        e2_ref[h] = jnp.exp(s2 - t2[0]) / z


def _peer_stats(xb, wq, keys_bd_t, tm=256):
    T, D = xb.shape
    H, n = PEER_HEADS, PEER_N_KEYS
    full = lambda a: pl.BlockSpec(a.shape, lambda i: (0, 0))
    o_spec = pl.BlockSpec((H, n, tm), lambda i: (0, 0, i))
    o_shape = jax.ShapeDtypeStruct((H, n, T), F32)
    return pl.pallas_call(
        _peer_stats_kernel, grid=(T // tm,),
        in_specs=[pl.BlockSpec((tm, D), lambda i: (i, 0)), full(wq), full(keys_bd_t)],
        out_specs=[o_spec] * 4, out_shape=[o_shape] * 4,
        scratch_shapes=[pltpu.VMEM((2 * H * n, tm), F32), pltpu.VMEM((_N_CAND, tm), F32)],
        compiler_params=_params("parallel"), name="peer_stats")(xb, wq, keys_bd_t)


def _peer_dense_kernel(xb_ref, x_ref, u_ref, vt_ref, c1_ref, e1_ref, r2_ref, e2_ref, g_ref, b_ref,
                       o_ref, ob_ref, acc_ref, a_ref, h_ref, *, tm, te):
    j = pl.program_id(1)
    n = PEER_N_KEYS
    rows_per_tile = te // n

    @pl.when(j == 0)
    def _():
        acc_ref[...] = jnp.zeros(acc_ref.shape, F32)

    a_ref[...] = _dot_nt(u_ref[...], xb_ref[...])
    for r in range(rows_per_tile):
        for lc in range(tm // LANES):
            sl = slice(lc * LANES, (lc + 1) * LANES)
            w = jnp.zeros((n, LANES), F32)
            for h in range(PEER_HEADS):
                c1 = c1_ref[h, r:r + 1, sl]
                e1 = e1_ref[h, r:r + 1, sl]
                w = w + jnp.where(r2_ref[h, :, sl] < c1, e1 * e2_ref[h, :, sl], 0.0)
            act = _gelu(a_ref[r * n:(r + 1) * n, sl])
            h_ref[r * n:(r + 1) * n, sl] = (w * act).astype(CDT)
    acc_ref[...] += _dot(vt_ref[...], h_ref[...])

    @pl.when(j == pl.num_programs(1) - 1)
    def _():
        z = DEEPNORM_ALPHA * x_ref[...] + acc_ref[...].T
        out = _layer_norm(z, g_ref[...], b_ref[...])
        o_ref[...] = out
        ob_ref[...] = out.astype(CDT)


def _peer_dense(xb, x, u, vt, stats, ln_g, ln_b, tm=512, te=1024):
    T, D = x.shape
    E = u.shape[0]
    H, n = PEER_HEADS, PEER_N_KEYS
    row = pl.BlockSpec((tm, D), lambda i, j: (i, 0))
    st = pl.BlockSpec((H, n, tm), lambda i, j: (0, 0, i))
    st1 = pl.BlockSpec((H, te // n, tm), lambda i, j: (0, j, i))
    vec = pl.BlockSpec((1, D), lambda i, j: (0, 0))
    assert (te // n) % 8 == 0
    kern = functools.partial(_peer_dense_kernel, tm=tm, te=te)
    return pl.pallas_call(
        kern, grid=(T // tm, E // te),
        in_specs=[row, row, pl.BlockSpec((te, D), lambda i, j: (j, 0)), pl.BlockSpec((D, te), lambda i, j: (0, j)),
                  st1, st1, st, st, vec, vec],
        out_specs=[row, row],
        out_shape=[jax.ShapeDtypeStruct((T, D), F32), jax.ShapeDtypeStruct((T, D), CDT)],
        scratch_shapes=[pltpu.VMEM((D, tm), F32), pltpu.VMEM((te, tm), F32), pltpu.VMEM((te, tm), CDT)],
        compiler_params=_params("parallel", "arbitrary"), name="peer_dense")(
            xb, x, u, vt, *stats, ln_g, ln_b)


def _rot_cols(w, dim):
    k, n = w.shape
    w4 = w.reshape(k, n // dim, 2, dim // 2)
    return jnp.stack([-w4[:, :, 1], w4[:, :, 0]], axis=2).reshape(k, n)


def _pad_heads(w, dim):
    k, n = w.shape
    w3 = w.reshape(k, n // dim, dim)
    return jnp.pad(w3, ((0, 0), (0, 0), (0, LANES - dim))).reshape(k, (n // dim) * LANES)


def _pad_cols(w, n):
    return jnp.pad(w, ((0, 0), (0, n - w.shape[1])))


def _layer_weights(l, w_in, mla_q_norm, mla_w_uq, mla_kv_norm, mla_w_ukv, w_branch_mla, w_branch_nsa, w_out,
                   ln1_g, ln1_b, peer_w_query, peer_sub_keys, peer_u, peer_v, ln2_g, ln2_b):
    off = np.cumsum((0,) + IN_WIDTHS)
    wi = w_in[l]
    seg = lambda k: wi[:, off[k]:off[k + 1]]
    G, dk = NSA_GROUPS, NSA_HEAD_DIM
    kv = seg(4).reshape(D_MODEL, 6, G * dk)
    lw = {}
    lw["w_cqkv"] = jnp.concatenate([seg(0), seg(1)], axis=1).astype(CDT)
    kr = seg(2)
    lw["w_kr"] = _pad_cols(kr, LANES).astype(CDT)
    lw["w_kr_rot"] = _pad_cols(_rot_cols(kr, MLA_ROPE_DIM), LANES).astype(CDT)
    lw["w_nq"] = _pad_heads(seg(3), dk).astype(CDT)
    lw["w_nq_rot"] = _pad_heads(_rot_cols(seg(3), dk), dk).astype(CDT)
    lw["w_kcmp"] = kv[:, 0].astype(CDT)
    lw["w_kcmp_rot"] = _rot_cols(kv[:, 0], dk).astype(CDT)
    for name, idx in (("ksel", 2), ("kwin", 4)):
        lw["w_" + name] = _pad_heads(kv[:, idx], dk).astype(CDT)
        lw["w_" + name + "_rot"] = _pad_heads(_rot_cols(kv[:, idx], dk), dk).astype(CDT)
    lw["w_vall"] = jnp.concatenate([kv[:, 1], _pad_heads(kv[:, 3], dk), _pad_heads(kv[:, 5], dk)], axis=1).astype(CDT)
    ng = seg(5).reshape(D_MODEL, G, NSA_HPG * 3)
    lw["w_ngate"] = jnp.pad(ng, ((0, 0), (0, 0), (0, LANES - NSA_HPG * 3))).reshape(D_MODEL, G * LANES).astype(CDT)
    lw["w_mgate"] = seg(6).astype(CDT)

    H = MLA_HEADS
    dq = MLA_NOPE_DIM + MLA_ROPE_DIM
    uq = mla_w_uq[l].reshape(MLA_Q_RANK, H, dq)
    lw["wqn"] = _pad_heads(uq[:, :, :MLA_NOPE_DIM].reshape(MLA_Q_RANK, -1), MLA_NOPE_DIM).astype(CDT)
    wqr = uq[:, :, MLA_NOPE_DIM:].reshape(MLA_Q_RANK, H * MLA_ROPE_DIM)
    lw["wqr"] = wqr.astype(CDT)
    lw["wqr_rot"] = _rot_cols(wqr, MLA_ROPE_DIM).astype(CDT)
    pq = np.zeros((H * MLA_ROPE_DIM, H * LANES), np.float32)
    pk = np.zeros((LANES, H * LANES), np.float32)
    for h in range(H):
        for j in range(MLA_ROPE_DIM):
            pq[h * MLA_ROPE_DIM + j, h * LANES + MLA_NOPE_DIM + j] = 1.0
            pk[j, h * LANES + MLA_NOPE_DIM + j] = 1.0
    lw["place_q"] = jnp.asarray(pq, CDT)
    lw["place_k"] = jnp.asarray(pk, CDT)
    ukv = mla_w_ukv[l].reshape(MLA_KV_RANK, H, MLA_NOPE_DIM + MLA_V_DIM)
    lw["wkn"] = _pad_heads(ukv[:, :, :MLA_NOPE_DIM].reshape(MLA_KV_RANK, -1), MLA_NOPE_DIM).astype(CDT)
    lw["wv"] = ukv[:, :, MLA_NOPE_DIM:].reshape(MLA_KV_RANK, H * MLA_V_DIM).astype(CDT)
    lw["q_gain"] = mla_q_norm[l].reshape(1, -1)
    lw["kv_gain"] = mla_kv_norm[l].reshape(1, -1)

    lw["w_bm"] = w_branch_mla[l].astype(CDT)
    wbn = w_branch_nsa[l].reshape(NSA_HEADS, dk, D_MODEL)
    lw["w_bn"] = jnp.pad(wbn, ((0, 0), (0, LANES - dk), (0, 0))).reshape(NSA_HEADS * LANES, D_MODEL).astype(CDT)
    lw["w_out"] = w_out[l].astype(CDT)
    lw["ln1_g"] = ln1_g[l].reshape(1, -1)
    lw["ln1_b"] = ln1_b[l].reshape(1, -1)

    lw["w_pq"] = peer_w_query[l].astype(CDT)
    sk = peer_sub_keys[l].reshape(PEER_HEADS * 2, PEER_N_KEYS, PEER_KEY_DIM // 2)
    eye = jnp.eye(PEER_HEADS * 2, dtype=sk.dtype)
    lw["keys_bd_t"] = jnp.einsum("gnd,gf->gnfd", sk, eye).reshape(
        PEER_HEADS * 2 * PEER_N_KEYS, PEER_HEADS * PEER_KEY_DIM).astype(CDT)
    lw["u"] = peer_u[l].astype(CDT)
    lw["vt"] = peer_v[l].T.astype(CDT)
    lw["ln2_g"] = ln2_g[l].reshape(1, -1)
    lw["ln2_b"] = ln2_b[l].reshape(1, -1)
    return lw


def _rope_tables(positions, dim):
    inv_freq = ROPE_THETA ** (-jnp.arange(0, dim, 2, dtype=F32) / dim)
    ang = positions.astype(F32).reshape(-1, 1) * inv_freq
    rep = lambda t: jnp.tile(jnp.concatenate([t, t], axis=-1), (1, LANES // dim))
    return rep(jnp.cos(ang)), rep(jnp.sin(ang))


def _hybrid_layer(h, hb, B, S, lw, cmp_w, tab_mla, tab_nsa):
    T = B * S
    G, dk = NSA_GROUPS, NSA_HEAD_DIM
    cqkv = _proj(hb, lw["w_cqkv"], CDT, name="proj_cqkv")
    kr = _proj(hb, lw["w_kr"], CDT, lw["w_kr_rot"], tab_mla, name="proj_kr")
    nq = _proj(hb, lw["w_nq"], CDT, lw["w_nq_rot"], tab_nsa, name="proj_nq")
    kcmp = _proj(hb, lw["w_kcmp"], CDT, lw["w_kcmp_rot"], tab_nsa, name="proj_kcmp")
    ksel = _proj(hb, lw["w_ksel"], CDT, lw["w_ksel_rot"], tab_nsa, name="proj_ksel")
    kwin = _proj(hb, lw["w_kwin"], CDT, lw["w_kwin_rot"], tab_nsa, name="proj_kwin")
    vall = _proj(hb, lw["w_vall"], CDT, name="proj_v")
    ngate = _proj(hb, lw["w_ngate"], F32, act="sigmoid", name="proj_ngate")
    mgate = _proj(hb, lw["w_mgate"], CDT, act="sigmoid", name="proj_mgate")

    q, k, v = _mla_up(cqkv, kr, lw, tab_mla)
    o_mla = _mla_attn(q.reshape(B, S, -1), k.reshape(B, S, -1), v.reshape(B, S, -1))

    def chunks(a):
        return a.reshape(B, S, G, dk).transpose(0, 2, 1, 3).reshape(B * G, S // CMP_STRIDE, CMP_STRIDE * dk)

    kc = _compress(chunks(kcmp), cmp_w["k_pe"], cmp_w["k_w1"], cmp_w["k_w2"])
    vc = _compress(chunks(vall[:, :G * dk]), cmp_w["v_pe"], cmp_w["v_w1"], cmp_w["v_w2"])
    vsel = vall[:, G * dk:G * dk + G * LANES]
    vwin = vall[:, G * dk + G * LANES:]
    r3 = lambda a: a.reshape(B, S, -1)
    o_nsa = _nsa(r3(nq), kc, vc, r3(ksel), r3(vsel), r3(kwin), r3(vwin), r3(ngate))

    x1, x1b = _merge(o_mla.reshape(T, -1), o_nsa.reshape(T, -1), mgate, h, lw)
    stats = _peer_stats(x1b, lw["w_pq"], lw["keys_bd_t"])
    return _peer_dense(x1b, x1, lw["u"], lw["vt"], stats, lw["ln2_g"], lw["ln2_b"])


def kernel(x, positions, ln_in_g, ln_in_b, w_in, mla_q_norm, mla_w_uq, mla_kv_norm, mla_w_ukv, nsa_cmp_k_pe, nsa_cmp_k_w1, nsa_cmp_k_w2, nsa_cmp_v_pe, nsa_cmp_v_w1, nsa_cmp_v_w2, w_branch_mla, w_branch_nsa, w_out, ln1_g, ln1_b, peer_w_query, peer_sub_keys, peer_u, peer_v, ln2_g, ln2_b):
    B, S, D = x.shape
    tab_mla = _rope_tables(positions, MLA_ROPE_DIM)
    tab_nsa = _rope_tables(positions, NSA_HEAD_DIM)
    h, hb = _ln_in(x.reshape(B * S, D), ln_in_g, ln_in_b)
    for l in range(DEPTH):
        lw = _layer_weights(l, w_in, mla_q_norm, mla_w_uq, mla_kv_norm, mla_w_ukv, w_branch_mla, w_branch_nsa,
                            w_out, ln1_g, ln1_b, peer_w_query, peer_sub_keys, peer_u, peer_v, ln2_g, ln2_b)
        cmp_w = {"k_pe": nsa_cmp_k_pe[l], "k_w1": nsa_cmp_k_w1[l], "k_w2": nsa_cmp_k_w2[l],
                 "v_pe": nsa_cmp_v_pe[l], "v_w1": nsa_cmp_v_w1[l], "v_w2": nsa_cmp_v_w2[l]}
        h, hb = _hybrid_layer(h, hb, B, S, lw, cmp_w, tab_mla, tab_nsa)
    return h.reshape(B, S, D)
```

```python
import functools
import math

import numpy as np
import jax
import jax.numpy as jnp
from jax import lax
from jax.experimental import pallas as pl
from jax.experimental.pallas import tpu as pltpu

D_MODEL = 1024
DEPTH = 2
ROPE_THETA = 10000.0

MLA_HEADS = 8
MLA_NOPE_DIM = 64
MLA_ROPE_DIM = 32
MLA_V_DIM = 64
MLA_Q_RANK = 768
MLA_KV_RANK = 256

NSA_HEADS = 8
NSA_GROUPS = 2
NSA_HPG = NSA_HEADS // NSA_GROUPS
NSA_HEAD_DIM = 64
CMP_BLOCK = 32
CMP_STRIDE = 16
CMP_HIDDEN = 2 * NSA_HEAD_DIM
SEL_BLOCK = 64
SEL_TOPN = 16
WINDOW = 512
FORCE_SCORE = 1e9

PEER_HEADS = 8
PEER_KEY_DIM = 128
PEER_N_KEYS = 128
PEER_N_EXPERTS = PEER_N_KEYS * PEER_N_KEYS
PEER_TOPK = 16

DEEPNORM_ALPHA = (2 * DEPTH) ** 0.25
LN_EPS = 1e-5
RMS_EPS = 1e-6
NEG_BIG = -1e30

IN_WIDTHS = (MLA_Q_RANK, MLA_KV_RANK, MLA_ROPE_DIM, NSA_HEADS * NSA_HEAD_DIM,
             6 * NSA_GROUPS * NSA_HEAD_DIM, 3 * NSA_HEADS, 2 * D_MODEL)

LANES = 128
VMEM_LIMIT = 56 * 1024 * 1024

F32 = jnp.float32
CDT = jnp.bfloat16

_NT = (((1,), (1,)), ((), ()))


def _dot(a, b):
    return jnp.dot(a, b, preferred_element_type=F32)


def _dot_nt(a, b):
    return lax.dot_general(a, b, _NT, preferred_element_type=F32)


def _params(*sem):
    return pltpu.CompilerParams(dimension_semantics=sem, vmem_limit_bytes=VMEM_LIMIT)


def _gelu(x):
    return 0.5 * x * (1.0 + lax.erf(x * (2.0 ** -0.5)))


def _layer_norm(z, g, b):
    mu = jnp.mean(z, axis=-1, keepdims=True)
    d = z - mu
    var = jnp.mean(d * d, axis=-1, keepdims=True)
    return d * lax.rsqrt(var + LN_EPS) * g + b


def _ln_in_kernel(x_ref, g_ref, b_ref, h_ref, hb_ref):
    h = _layer_norm(x_ref[...], g_ref[...], b_ref[...])
    h_ref[...] = h
    hb_ref[...] = h.astype(CDT)


def _ln_in(x, g, b, tm=512):
    T, D = x.shape
    row = pl.BlockSpec((tm, D), lambda i: (i, 0))
    vec = pl.BlockSpec((1, D), lambda i: (0, 0))
    return pl.pallas_call(
        _ln_in_kernel, grid=(T // tm,), in_specs=[row, vec, vec], out_specs=[row, row],
        out_shape=[jax.ShapeDtypeStruct((T, D), F32), jax.ShapeDtypeStruct((T, D), CDT)],
        compiler_params=_params("parallel"), name="ln_in")(x, g.reshape(1, D), b.reshape(1, D))


def _proj_kernel(*refs, rope, act, n_rep):
    if rope:
        hb_ref, w_ref, wr_ref, cos_ref, sin_ref, o_ref = refs
    else:
        hb_ref, w_ref, o_ref = refs
    hb = hb_ref[...]
    y = _dot(hb, w_ref[...])
    if rope:
        yr = _dot(hb, wr_ref[...])
        c, s = cos_ref[...], sin_ref[...]
        if n_rep > 1:
            c = jnp.concatenate([c] * n_rep, axis=1)
            s = jnp.concatenate([s] * n_rep, axis=1)
        y = y * c + yr * s
    if act == "sigmoid":
        y = 1.0 / (1.0 + jnp.exp(-y))
    o_ref[...] = y.astype(o_ref.dtype)


def _proj(hb, w, out_dtype, w_rot=None, tables=None, act=None, tm=1024, name="proj"):
    T, K = hb.shape
    N = w.shape[1]
    tn = N if N <= 768 else 512
    assert N % tn == 0 and tn % LANES == 0 and T % tm == 0
    rope = w_rot is not None
    x_spec = pl.BlockSpec((tm, K), lambda j, i: (i, 0))
    w_spec = pl.BlockSpec((K, tn), lambda j, i: (0, j))
    t_spec = pl.BlockSpec((tm, LANES), lambda j, i: (i, 0))
    o_spec = pl.BlockSpec((tm, tn), lambda j, i: (i, j))
    if rope:
        args = (hb, w, w_rot, tables[0], tables[1])
        in_specs = [x_spec, w_spec, w_spec, t_spec, t_spec]
    else:
        args = (hb, w)
        in_specs = [x_spec, w_spec]
    kern = functools.partial(_proj_kernel, rope=rope, act=act, n_rep=tn // LANES)
    return pl.pallas_call(
        kern, grid=(N // tn, T // tm), in_specs=in_specs, out_specs=o_spec,
        out_shape=jax.ShapeDtypeStruct((T, N), out_dtype),
        compiler_params=_params("parallel", "parallel"), name=name)(*args)


def _rms(x, g):
    return x * lax.rsqrt(jnp.mean(x * x, axis=-1, keepdims=True) + RMS_EPS) * g


def _mla_up_kernel(c_ref, kr_ref, qg_ref, kvg_ref, wqn_ref, wqr_ref, wqrr_ref, pq_ref,
                   wkn_ref, wv_ref, pk_ref, cos_ref, sin_ref, q_ref, k_ref, v_ref):
    c = c_ref[...].astype(F32)
    cqn = _rms(c[:, :MLA_Q_RANK], qg_ref[...]).astype(CDT)
    ckvn = _rms(c[:, MLA_Q_RANK:], kvg_ref[...]).astype(CDT)
    cos = jnp.concatenate([cos_ref[...]] * 2, axis=1)
    sin = jnp.concatenate([sin_ref[...]] * 2, axis=1)
    roped = (_dot(cqn, wqr_ref[...]) * cos + _dot(cqn, wqrr_ref[...]) * sin).astype(CDT)
    q_ref[...] = (_dot(cqn, wqn_ref[...]) + _dot(roped, pq_ref[...])).astype(CDT)
    k_ref[...] = (_dot(ckvn, wkn_ref[...]) + _dot(kr_ref[...], pk_ref[...])).astype(CDT)
    v_ref[...] = _dot(ckvn, wv_ref[...]).astype(CDT)


def _mla_up(cqkv, kr, lw, tables, tm=512):
    T = cqkv.shape[0]
    row = lambda n: pl.BlockSpec((tm, n), lambda i: (i, 0))
    full = lambda a: pl.BlockSpec(a.shape, lambda i: (0, 0))
    ws = (lw["q_gain"], lw["kv_gain"], lw["wqn"], lw["wqr"], lw["wqr_rot"], lw["place_q"],
          lw["wkn"], lw["wv"], lw["place_k"])
    hp = MLA_HEADS * LANES
    return pl.pallas_call(
        _mla_up_kernel, grid=(T // tm,),
        in_specs=[row(cqkv.shape[1]), row(LANES)] + [full(a) for a in ws] + [row(LANES), row(LANES)],
        out_specs=[row(hp), row(hp), row(MLA_HEADS * MLA_V_DIM)],
        out_shape=[jax.ShapeDtypeStruct((T, hp), CDT), jax.ShapeDtypeStruct((T, hp), CDT),
                   jax.ShapeDtypeStruct((T, MLA_HEADS * MLA_V_DIM), CDT)],
        compiler_params=_params("parallel"), name="mla_up")(cqkv, kr, *ws, tables[0], tables[1])


def _mla_attn_kernel(q_ref, k_ref, v_ref, o_ref, *, tq, scale):
    i = pl.program_id(2)
    outs = []
    for hd in range(2):
        q = q_ref[0, :, hd * LANES:(hd + 1) * LANES]

        def step(kt, carry, diagonal, hd=hd, q=q):
            m, l, acc = carry
            k0 = pl.multiple_of(kt * tq, tq)
            k = k_ref[0, pl.ds(k0, tq), hd * LANES:(hd + 1) * LANES]
            v = v_ref[0, pl.ds(k0, tq), :]
            s = _dot_nt(q, k) * scale
            if diagonal:
                row = lax.broadcasted_iota(jnp.int32, (tq, tq), 0)
                col = lax.broadcasted_iota(jnp.int32, (tq, tq), 1)
                s = jnp.where(col <= row, s, NEG_BIG)
            m_new = jnp.maximum(m, jnp.max(s, axis=1, keepdims=True))
            alpha = jnp.exp(m - m_new)
            p = jnp.exp(s - m_new)
            l = l * alpha + jnp.sum(p, axis=1, keepdims=True)
            acc = acc * alpha + _dot(p.astype(CDT), v)
            return m_new, l, acc

        init = (jnp.full((tq, 1), NEG_BIG, F32), jnp.zeros((tq, 1), F32), jnp.zeros((tq, LANES), F32))
        carry = lax.fori_loop(0, i, lambda kt, c: step(kt, c, False), init)
        _, l, acc = step(i, carry, True)
        outs.append(acc / l)
    lane = lax.broadcasted_iota(jnp.int32, (tq, LANES), 1)
    o_ref[0] = jnp.where(lane < MLA_V_DIM, outs[0], outs[1]).astype(o_ref.dtype)


def _mla_attn(q, k, v, tq=512):
    B, S, _ = q.shape
    scale = (MLA_NOPE_DIM + MLA_ROPE_DIM) ** -0.5
    kern = functools.partial(_mla_attn_kernel, tq=tq, scale=scale)
    return pl.pallas_call(
        kern, grid=(B, MLA_HEADS // 2, S // tq),
        in_specs=[pl.BlockSpec((1, tq, 2 * LANES), lambda b, h, i: (b, i, h)),
                  pl.BlockSpec((1, S, 2 * LANES), lambda b, h, i: (b, 0, h)),
                  pl.BlockSpec((1, S, LANES), lambda b, h, i: (b, 0, h))],
        out_specs=pl.BlockSpec((1, tq, LANES), lambda b, h, i: (b, i, h)),
        out_shape=jax.ShapeDtypeStruct((B, S, MLA_HEADS * MLA_V_DIM), CDT),
        compiler_params=_params("parallel", "parallel", "arbitrary"), name="mla_attn")(q, k, v)


def _compress_kernel(c_ref, pe_ref, w1a_ref, w1b_ref, w2_ref, o_ref, *, n_valid):
    c = c_ref[0].astype(F32)
    n = c.shape[0]
    a = _dot((c + pe_ref[0:1, :]).astype(CDT), w1a_ref[...])
    b = _dot((c + pe_ref[1:2, :]).astype(CDT), w1b_ref[...])
    hid = _gelu(a + pltpu.roll(b, n - 1, 0))
    out = _dot(hid.astype(CDT), w2_ref[...])
    row = lax.broadcasted_iota(jnp.int32, out.shape, 0)
    o_ref[0] = jnp.where(row < n_valid, out, 0.0).astype(o_ref.dtype)


def _compress(chunks, pe, w1, w2):
    BG, n, width = chunks.shape
    pe2 = pe.reshape(2, width).astype(F32)
    w1a = w1[:width].astype(CDT)
    w1b = w1[width:].astype(CDT)
    w2p = jnp.pad(w2, ((0, 0), (0, LANES - w2.shape[1]))).astype(CDT)
    full = lambda a: pl.BlockSpec(a.shape, lambda i: (0, 0))
    return pl.pallas_call(
        functools.partial(_compress_kernel, n_valid=n - 1), grid=(BG,),
        in_specs=[pl.BlockSpec((1, n, width), lambda i: (i, 0, 0)), full(pe2), full(w1a), full(w1b), full(w2p)],
        out_specs=pl.BlockSpec((1, n, LANES), lambda i: (i, 0, 0)),
        out_shape=jax.ShapeDtypeStruct((BG, n, LANES), CDT),
        compiler_params=_params("parallel"), name="nsa_compress")(chunks, pe2, w1a, w1b, w2p)


def _nsa_kernel(q_ref, kc_ref, vc_ref, ks_ref, vs_ref, kw_ref, vw_ref, g_ref, ovl_ref, o_ref,
                *, tq, n_sel, n_cmp, scale):
    i = pl.program_id(2)
    q0 = i * tq
    R = NSA_HPG
    ncp = kc_ref.shape[1]
    Q = jnp.concatenate([q_ref[0, :, r * LANES:(r + 1) * LANES] for r in range(R)], axis=0)

    s = (_dot_nt(Q, kc_ref[0]) * scale).reshape(R, tq, ncp)
    t_c = q0 + lax.broadcasted_iota(jnp.int32, (tq, ncp), 0)
    n_c = lax.broadcasted_iota(jnp.int32, (tq, ncp), 1)
    mask_c = ((n_c * CMP_STRIDE + (CMP_BLOCK - 1) <= t_c) & (n_c < n_cmp))[None]
    s = jnp.where(mask_c, s, NEG_BIG)
    p = jnp.where(mask_c, jnp.exp(s - jnp.max(s, axis=2, keepdims=True)), 0.0)
    l = jnp.sum(p, axis=2, keepdims=True)
    p = p * jnp.where(l > 0.0, 1.0 / l, 0.0)
    o_c = _dot(p.reshape(R * tq, ncp).astype(CDT), vc_ref[0])

    psum = jnp.sum(p, axis=0)
    p_hi = psum.astype(CDT)
    p_lo = (psum - p_hi.astype(F32)).astype(CDT)
    imp = (_dot_nt(ovl_ref[...], p_hi) + _dot_nt(ovl_ref[...], p_lo))[0:n_sel]
    jj = lax.broadcasted_iota(jnp.int32, (n_sel, tq), 0)
    cur = (q0 + lax.broadcasted_iota(jnp.int32, (n_sel, tq), 1)) // SEL_BLOCK
    forced = (jj == 0) | (jj == cur) | (jj == cur - 1)
    val = jnp.where(forced, FORCE_SCORE, jnp.where(jj > cur, -FORCE_SCORE, imp))
    rank = jnp.zeros((n_sel, tq), F32)
    for jp in range(n_sel):
        rowv = val[jp:jp + 1, :]
        gt = jnp.where(rowv > val, 1.0, 0.0)
        ge = jnp.where(rowv >= val, 1.0, 0.0)
        rank = rank + jnp.where(jj > jp, ge, gt)
    top_n = min(SEL_TOPN, n_sel)
    sel_t = jnp.where((rank < top_n) & (jj <= cur), 1.0, 0.0)
    if n_sel < LANES:
        sel_t = jnp.concatenate([sel_t, jnp.zeros((LANES - n_sel, tq), F32)], axis=0)
    sel = sel_t.T.astype(CDT)

    def attend(k_ref, v_ref, lo, hi, mask_fn):
        def step(kt, carry):
            m, l, acc = carry
            k0 = pl.multiple_of(kt * tq, tq)
            kb = k_ref[0, pl.ds(k0, tq), :]
            vb = v_ref[0, pl.ds(k0, tq), :]
            mask = mask_fn(kt)[None]
            s = jnp.where(mask, (_dot_nt(Q, kb) * scale).reshape(R, tq, tq), NEG_BIG)
            m_new = jnp.maximum(m, jnp.max(s, axis=2, keepdims=True))
            alpha = jnp.exp(m - m_new)
            p = jnp.where(mask, jnp.exp(s - m_new), 0.0)
            l = l * alpha + jnp.sum(p, axis=2, keepdims=True)
            pv = _dot(p.reshape(R * tq, tq).astype(CDT), vb).reshape(R, tq, LANES)
            return m_new, l, acc * alpha + pv
        init = (jnp.full((R, tq, 1), NEG_BIG, F32), jnp.zeros((R, tq, 1), F32), jnp.zeros((R, tq, LANES), F32))
        _, l, acc = lax.fori_loop(lo, hi, step, init)
        return acc * jnp.where(l > 0.0, 1.0 / l, 0.0)

    row = lax.broadcasted_iota(jnp.int32, (tq, tq), 0)
    col = lax.broadcasted_iota(jnp.int32, (tq, tq), 1)
    blk_per_tile = tq // SEL_BLOCK

    def mask_sel(kt):
        ej = lax.broadcasted_iota(jnp.int32, (LANES, tq), 0)
        ec = lax.broadcasted_iota(jnp.int32, (LANES, tq), 1)
        expand = jnp.where(ej == kt * blk_per_tile + ec // SEL_BLOCK, 1.0, 0.0).astype(CDT)
        chosen = _dot(sel, expand) > 0.5
        return chosen & (kt * tq + col <= q0 + row)

    def mask_win(kt):
        d = (q0 + row) - (kt * tq + col)
        return (d >= 0) & (d < WINDOW)

    o_s = attend(ks_ref, vs_ref, 0, i + 1, mask_sel)
    o_w = attend(kw_ref, vw_ref, jnp.maximum(i - WINDOW // tq, 0), i + 1, mask_win)

    g = g_ref[0]
    o_c = o_c.reshape(R, tq, LANES)
    for r in range(R):
        o = (g[:, 3 * r:3 * r + 1] * o_c[r] + g[:, 3 * r + 1:3 * r + 2] * o_s[r]
             + g[:, 3 * r + 2:3 * r + 3] * o_w[r])
        o_ref[0, :, r * LANES:(r + 1) * LANES] = o.astype(o_ref.dtype)


def _nsa(q, kc, vc, ks, vs, kw, vw, gates, tq=256):
    B, S, _ = q.shape
    G = NSA_GROUPS
    n_sel = S // SEL_BLOCK
    n_cmp = (S - CMP_BLOCK) // CMP_STRIDE + 1
    ncp = kc.shape[1]
    assert n_sel <= LANES and n_sel % 8 == 0 and ncp % LANES == 0 and WINDOW % tq == 0
    jn = np.arange(LANES)[:, None] * SEL_BLOCK
    cn = np.arange(ncp)[None, :] * CMP_STRIDE
    ovl = ((cn <= jn + SEL_BLOCK - 1) & (cn + CMP_BLOCK - 1 >= jn)
           & (np.arange(LANES)[:, None] < n_sel) & (np.arange(ncp)[None, :] < n_cmp))
    ovl = jnp.asarray(ovl, CDT)
    kern = functools.partial(_nsa_kernel, tq=tq, n_sel=n_sel, n_cmp=n_cmp, scale=NSA_HEAD_DIM ** -0.5)
    cmp_spec = pl.BlockSpec((1, ncp, LANES), lambda b, g, i: (b * G + g, 0, 0))
    kv_spec = pl.BlockSpec((1, S, LANES), lambda b, g, i: (b, 0, g))
    qo_spec = pl.BlockSpec((1, tq, NSA_HPG * LANES), lambda b, g, i: (b, i, g))
    return pl.pallas_call(
        kern, grid=(B, G, S // tq),
        in_specs=[qo_spec, cmp_spec, cmp_spec, kv_spec, kv_spec, kv_spec, kv_spec,
                  pl.BlockSpec((1, tq, LANES), lambda b, g, i: (b, i, g)),
                  pl.BlockSpec(ovl.shape, lambda b, g, i: (0, 0))],
        out_specs=qo_spec,
        out_shape=jax.ShapeDtypeStruct((B, S, NSA_HEADS * LANES), CDT),
        compiler_params=_params("parallel", "parallel", "arbitrary"), name="nsa_attn")(
            q, kc, vc, ks, vs, kw, vw, gates, ovl)


def _merge_kernel(om_ref, on_ref, mg_ref, h_ref, wbm_ref, wbn_ref, wo_ref, g_ref, b_ref, x_ref, xb_ref):
    D = D_MODEL
    y_mla = _dot(om_ref[...], wbm_ref[...])
    y_nsa = _dot(on_ref[...], wbn_ref[...])
    mg = mg_ref[...].astype(F32)
    mixed = _dot((mg[:, :D] * y_mla + mg[:, D:] * y_nsa).astype(CDT), wo_ref[...])
    x = _layer_norm(DEEPNORM_ALPHA * h_ref[...] + mixed, g_ref[...], b_ref[...])
    x_ref[...] = x
    xb_ref[...] = x.astype(CDT)


def _merge(o_mla, o_nsa, mg, h, lw, tm=512):
    T, D = h.shape
    row = lambda n: pl.BlockSpec((tm, n), lambda i: (i, 0))
    full = lambda a: pl.BlockSpec(a.shape, lambda i: (0, 0))
    ws = (lw["w_bm"], lw["w_bn"], lw["w_out"], lw["ln1_g"], lw["ln1_b"])
    return pl.pallas_call(
        _merge_kernel, grid=(T // tm,),
        in_specs=[row(o_mla.shape[1]), row(o_nsa.shape[1]), row(2 * D), row(D)] + [full(a) for a in ws],
        out_specs=[row(D), row(D)],
        out_shape=[jax.ShapeDtypeStruct((T, D), F32), jax.ShapeDtypeStruct((T, D), CDT)],
        compiler_params=_params("parallel"), name="merge_ln1")(o_mla, o_nsa, mg, h, *ws)


def _rows_per_vreg():
    return 8 * (4 // jnp.dtype(CDT).itemsize)


def _pack_words(x):
    if jnp.dtype(CDT).itemsize == 4:
        return pltpu.bitcast(x, jnp.uint32)
    bits = pltpu.bitcast(x.astype(CDT).astype(F32), jnp.uint32)
    return (bits >> 16) | (bits & jnp.uint32(0xFFFF0000))


def _unpack_words(row):
    return pltpu.bitcast(jnp.broadcast_to(row, (8, LANES)), CDT)


_PEER_PAIRS = [(a, b) for a in range(PEER_TOPK) for b in range(PEER_TOPK) if (a + 1) * (b + 1) <= PEER_TOPK]
_N_CAND = -(-len(_PEER_PAIRS) // 8) * 8


def _top_values(v, want_rank):
    tops = []
    rank = jnp.full(v.shape, 127.0, F32) if want_rank else None
    for r in range(PEER_TOPK):
        m = jnp.max(v, axis=0, keepdims=True)
        tops.append(m)
        eq = v == m
        if want_rank:
            rank = jnp.where(eq, float(r), rank)
        v = jnp.where(eq, -jnp.inf, v)
    return tops, rank


def _peer_stats_kernel(xb_ref, wq_ref, keys_ref, c1_ref, e1_ref, r2_ref, e2_ref, s_ref, cand_ref):
    qp = _dot(xb_ref[...], wq_ref[...]).astype(CDT)
    s_ref[...] = _dot_nt(keys_ref[...], qp)
    n = PEER_N_KEYS
    cand_ref[...] = jnp.full(cand_ref.shape, -jnp.inf, F32)
    for h in range(PEER_HEADS):
        s1 = s_ref[(2 * h) * n:(2 * h + 1) * n, :]
        s2 = s_ref[(2 * h + 1) * n:(2 * h + 2) * n, :]
        t1, _ = _top_values(s1, False)
        t2, rank2 = _top_values(s2, True)
        for c, (a, b) in enumerate(_PEER_PAIRS):
            cand_ref[c:c + 1, :] = t1[a] + t2[b]
        cand = cand_ref[...]
        tc, _ = _top_values(cand, False)
        tau = tc[PEER_TOPK - 1]
        top = t1[0] + t2[0]
        z = jnp.sum(jnp.where(cand >= tau, jnp.exp(cand - top), 0.0), axis=0, keepdims=True)
        cnt = jnp.zeros(s1.shape, F32)
        for b in range(PEER_TOPK):
            cnt = cnt + jnp.where(s1 + t2[b] >= tau, 1.0, 0.0)
        c1_ref[h] = _pack_words(cnt)
        e1_ref[h] = _pack_words(jnp.exp(s1 - t1[0]))
        r2_ref[h] = rank2.astype(r2_ref.dtype)
        e2_ref[h] = (jnp.exp(s2 - t2[0]) / z).astype(e2_ref.dtype)


def _peer_stats(xb, wq, keys_bd_t, tm=256):
    T, D = xb.shape
    H, n = PEER_HEADS, PEER_N_KEYS
    full = lambda a: pl.BlockSpec(a.shape, lambda i: (0, 0))
    o_spec = pl.BlockSpec((H, n, tm), lambda i: (0, 0, i))
    o_u32 = jax.ShapeDtypeStruct((H, n, T), jnp.uint32)
    o_cdt = jax.ShapeDtypeStruct((H, n, T), CDT)
    return pl.pallas_call(
        _peer_stats_kernel, grid=(T // tm,),
        in_specs=[pl.BlockSpec((tm, D), lambda i: (i, 0)), full(wq), full(keys_bd_t)],
        out_specs=[o_spec] * 4, out_shape=[o_u32, o_u32, o_cdt, o_cdt],
        scratch_shapes=[pltpu.VMEM((2 * H * n, tm), F32), pltpu.VMEM((_N_CAND, tm), F32)],
        compiler_params=_params("parallel"), name="peer_stats")(xb, wq, keys_bd_t)


def _peer_dense_kernel(xb_ref, x_ref, u_ref, vt_ref, c1_ref, e1_ref, r2_in_ref, e2_in_ref, g_ref, b_ref,
                       o_ref, ob_ref, acc_ref, a_ref, h_ref, r2_ref, e2_ref, *, tm, te):
    j = pl.program_id(1)
    n = PEER_N_KEYS
    rows_per_tile = te // n

    @pl.when(j == 0)
    def _():
        acc_ref[...] = jnp.zeros(acc_ref.shape, F32)
        r2_ref[...] = r2_in_ref[...]
        e2_ref[...] = e2_in_ref[...]

    a_ref[...] = _gelu(_dot_nt(u_ref[...], xb_ref[...])).astype(CDT)
    ic = n // 2
    sub = _rows_per_vreg()
    nv = ic // sub
    zero = jnp.zeros((nv, sub, LANES), CDT)
    n_i2 = n // ic

    def gate_block(idx, carry):
        sl = pl.ds(pl.multiple_of((idx // n_i2) * LANES, LANES), LANES)
        i2 = pl.multiple_of((idx % n_i2) * ic, ic)
        ws = [zero] * rows_per_tile
        for h in range(PEER_HEADS):
            r2 = r2_ref[h, pl.ds(i2, ic), sl].reshape(nv, sub, LANES)
            e2 = e2_ref[h, pl.ds(i2, ic), sl].reshape(nv, sub, LANES)
            for r in range(rows_per_tile):
                c1 = _unpack_words(c1_ref[h, r:r + 1, sl])[None]
                e1 = _unpack_words(e1_ref[h, r:r + 1, sl])[None]
                ws[r] = ws[r] + jnp.where(r2 < c1, e1 * e2, zero)
        for r in range(rows_per_tile):
            rows = pl.ds(r * n + i2, ic)
            h_ref[rows, sl] = ws[r].reshape(ic, LANES) * a_ref[rows, sl]
        return carry

    lax.fori_loop(0, (tm // LANES) * n_i2, gate_block, 0)
    acc_ref[...] += _dot(vt_ref[...], h_ref[...])

    @pl.when(j == pl.num_programs(1) - 1)
    def _():
        z = DEEPNORM_ALPHA * x_ref[...] + acc_ref[...].T
        out = _layer_norm(z, g_ref[...], b_ref[...])
        o_ref[...] = out
        ob_ref[...] = out.astype(CDT)


def _peer_dense(xb, x, u, vt, stats, ln_g, ln_b, tm=512, te=1024):
    T, D = x.shape
    E = u.shape[0]
    H, n = PEER_HEADS, PEER_N_KEYS
    row = pl.BlockSpec((tm, D), lambda i, j: (i, 0))
    st = pl.BlockSpec((H, n, tm), lambda i, j: (0, 0, i))
    st1 = pl.BlockSpec((H, te // n, tm), lambda i, j: (0, j, i))
    vec = pl.BlockSpec((1, D), lambda i, j: (0, 0))
    assert (te // n) % 8 == 0
    kern = functools.partial(_peer_dense_kernel, tm=tm, te=te)
    return pl.pallas_call(
        kern, grid=(T // tm, E // te),
        in_specs=[row, row, pl.BlockSpec((te, D), lambda i, j: (j, 0)), pl.BlockSpec((D, te), lambda i, j: (0, j)),
                  st1, st1, st, st, vec, vec],
        out_specs=[row, row],
        out_shape=[jax.ShapeDtypeStruct((T, D), F32), jax.ShapeDtypeStruct((T, D), CDT)],
        scratch_shapes=[pltpu.VMEM((D, tm), F32), pltpu.VMEM((te, tm), CDT), pltpu.VMEM((te, tm), CDT),
                        pltpu.VMEM((H, n, tm), CDT), pltpu.VMEM((H, n, tm), CDT)],
        compiler_params=_params("parallel", "arbitrary"), name="peer_dense")(
            xb, x, u, vt, *stats, ln_g, ln_b)


def _rot_cols(w, dim):
    k, n = w.shape
    w4 = w.reshape(k, n // dim, 2, dim // 2)
    return jnp.stack([-w4[:, :, 1], w4[:, :, 0]], axis=2).reshape(k, n)


def _pad_heads(w, dim):
    k, n = w.shape
    w3 = w.reshape(k, n // dim, dim)
    return jnp.pad(w3, ((0, 0), (0, 0), (0, LANES - dim))).reshape(k, (n // dim) * LANES)


def _pad_cols(w, n):
    return jnp.pad(w, ((0, 0), (0, n - w.shape[1])))


def _layer_weights(l, w_in, mla_q_norm, mla_w_uq, mla_kv_norm, mla_w_ukv, w_branch_mla, w_branch_nsa, w_out,
                   ln1_g, ln1_b, peer_w_query, peer_sub_keys, peer_u, peer_v, ln2_g, ln2_b):
    off = np.cumsum((0,) + IN_WIDTHS)
    wi = w_in[l]
    seg = lambda k: wi[:, off[k]:off[k + 1]]
    G, dk = NSA_GROUPS, NSA_HEAD_DIM
    kv = seg(4).reshape(D_MODEL, 6, G * dk)
    lw = {}
    lw["w_cqkv"] = jnp.concatenate([seg(0), seg(1)], axis=1).astype(CDT)
    kr = seg(2)
    lw["w_kr"] = _pad_cols(kr, LANES).astype(CDT)
    lw["w_kr_rot"] = _pad_cols(_rot_cols(kr, MLA_ROPE_DIM), LANES).astype(CDT)
    lw["w_nq"] = _pad_heads(seg(3), dk).astype(CDT)
    lw["w_nq_rot"] = _pad_heads(_rot_cols(seg(3), dk), dk).astype(CDT)
    lw["w_kcmp"] = kv[:, 0].astype(CDT)
    lw["w_kcmp_rot"] = _rot_cols(kv[:, 0], dk).astype(CDT)
    for name, idx in (("ksel", 2), ("kwin", 4)):
        lw["w_" + name] = _pad_heads(kv[:, idx], dk).astype(CDT)
        lw["w_" + name + "_rot"] = _pad_heads(_rot_cols(kv[:, idx], dk), dk).astype(CDT)
    lw["w_vall"] = jnp.concatenate([kv[:, 1], _pad_heads(kv[:, 3], dk), _pad_heads(kv[:, 5], dk)], axis=1).astype(CDT)
    ng = seg(5).reshape(D_MODEL, G, NSA_HPG * 3)
    lw["w_ngate"] = jnp.pad(ng, ((0, 0), (0, 0), (0, LANES - NSA_HPG * 3))).reshape(D_MODEL, G * LANES).astype(CDT)
    lw["w_mgate"] = seg(6).astype(CDT)

    H = MLA_HEADS
    dq = MLA_NOPE_DIM + MLA_ROPE_DIM
    uq = mla_w_uq[l].reshape(MLA_Q_RANK, H, dq)
    lw["wqn"] = _pad_heads(uq[:, :, :MLA_NOPE_DIM].reshape(MLA_Q_RANK, -1), MLA_NOPE_DIM).astype(CDT)
    wqr = uq[:, :, MLA_NOPE_DIM:].reshape(MLA_Q_RANK, H * MLA_ROPE_DIM)
    lw["wqr"] = wqr.astype(CDT)
    lw["wqr_rot"] = _rot_cols(wqr, MLA_ROPE_DIM).astype(CDT)
    pq = np.zeros((H * MLA_ROPE_DIM, H * LANES), np.float32)
    pk = np.zeros((LANES, H * LANES), np.float32)
    for h in range(H):
        for j in range(MLA_ROPE_DIM):
            pq[h * MLA_ROPE_DIM + j, h * LANES + MLA_NOPE_DIM + j] = 1.0
            pk[j, h * LANES + MLA_NOPE_DIM + j] = 1.0
    lw["place_q"] = jnp.asarray(pq, CDT)
    lw["place_k"] = jnp.asarray(pk, CDT)
    ukv = mla_w_ukv[l].reshape(MLA_KV_RANK, H, MLA_NOPE_DIM + MLA_V_DIM)
    lw["wkn"] = _pad_heads(ukv[:, :, :MLA_NOPE_DIM].reshape(MLA_KV_RANK, -1), MLA_NOPE_DIM).astype(CDT)
    lw["wv"] = ukv[:, :, MLA_NOPE_DIM:].reshape(MLA_KV_RANK, H * MLA_V_DIM).astype(CDT)
    lw["q_gain"] = mla_q_norm[l].reshape(1, -1)
    lw["kv_gain"] = mla_kv_norm[l].reshape(1, -1)

    lw["w_bm"] = w_branch_mla[l].astype(CDT)
    wbn = w_branch_nsa[l].reshape(NSA_HEADS, dk, D_MODEL)
    lw["w_bn"] = jnp.pad(wbn, ((0, 0), (0, LANES - dk), (0, 0))).reshape(NSA_HEADS * LANES, D_MODEL).astype(CDT)
    lw["w_out"] = w_out[l].astype(CDT)
    lw["ln1_g"] = ln1_g[l].reshape(1, -1)
    lw["ln1_b"] = ln1_b[l].reshape(1, -1)

    lw["w_pq"] = peer_w_query[l].astype(CDT)
    sk = peer_sub_keys[l].reshape(PEER_HEADS * 2, PEER_N_KEYS, PEER_KEY_DIM // 2)
    eye = jnp.eye(PEER_HEADS * 2, dtype=sk.dtype)
    lw["keys_bd_t"] = jnp.einsum("gnd,gf->gnfd", sk, eye).reshape(
        PEER_HEADS * 2 * PEER_N_KEYS, PEER_HEADS * PEER_KEY_DIM).astype(CDT)
    lw["u"] = peer_u[l].astype(CDT)
    lw["vt"] = peer_v[l].T.astype(CDT)
    lw["ln2_g"] = ln2_g[l].reshape(1, -1)
    lw["ln2_b"] = ln2_b[l].reshape(1, -1)
    return lw


def _rope_tables(positions, dim):
    inv_freq = ROPE_THETA ** (-jnp.arange(0, dim, 2, dtype=F32) / dim)
    ang = positions.astype(F32).reshape(-1, 1) * inv_freq
    rep = lambda t: jnp.tile(jnp.concatenate([t, t], axis=-1), (1, LANES // dim))
    return rep(jnp.cos(ang)), rep(jnp.sin(ang))


def _hybrid_layer(h, hb, B, S, lw, cmp_w, tab_mla, tab_nsa):
    T = B * S
    G, dk = NSA_GROUPS, NSA_HEAD_DIM
    cqkv = _proj(hb, lw["w_cqkv"], CDT, name="proj_cqkv")
    kr = _proj(hb, lw["w_kr"], CDT, lw["w_kr_rot"], tab_mla, name="proj_kr")
    nq = _proj(hb, lw["w_nq"], CDT, lw["w_nq_rot"], tab_nsa, name="proj_nq")
    kcmp = _proj(hb, lw["w_kcmp"], CDT, lw["w_kcmp_rot"], tab_nsa, name="proj_kcmp")
    ksel = _proj(hb, lw["w_ksel"], CDT, lw["w_ksel_rot"], tab_nsa, name="proj_ksel")
    kwin = _proj(hb, lw["w_kwin"], CDT, lw["w_kwin_rot"], tab_nsa, name="proj_kwin")
    vall = _proj(hb, lw["w_vall"], CDT, name="proj_v")
    ngate = _proj(hb, lw["w_ngate"], F32, act="sigmoid", name="proj_ngate")
    mgate = _proj(hb, lw["w_mgate"], CDT, act="sigmoid", name="proj_mgate")

    q, k, v = _mla_up(cqkv, kr, lw, tab_mla)
    o_mla = _mla_attn(q.reshape(B, S, -1), k.reshape(B, S, -1), v.reshape(B, S, -1))

    def chunks(a):
        return a.reshape(B, S, G, dk).transpose(0, 2, 1, 3).reshape(B * G, S // CMP_STRIDE, CMP_STRIDE * dk)

    kc = _compress(chunks(kcmp), cmp_w["k_pe"], cmp_w["k_w1"], cmp_w["k_w2"])
    vc = _compress(chunks(vall[:, :G * dk]), cmp_w["v_pe"], cmp_w["v_w1"], cmp_w["v_w2"])
    vsel = vall[:, G * dk:G * dk + G * LANES]
    vwin = vall[:, G * dk + G * LANES:]
    r3 = lambda a: a.reshape(B, S, -1)
    o_nsa = _nsa(r3(nq), kc, vc, r3(ksel), r3(vsel), r3(kwin), r3(vwin), r3(ngate))

    x1, x1b = _merge(o_mla.reshape(T, -1), o_nsa.reshape(T, -1), mgate, h, lw)
    stats = _peer_stats(x1b, lw["w_pq"], lw["keys_bd_t"])
    return _peer_dense(x1b, x1, lw["u"], lw["vt"], stats, lw["ln2_g"], lw["ln2_b"])


def kernel(x, positions, ln_in_g, ln_in_b, w_in, mla_q_norm, mla_w_uq, mla_kv_norm, mla_w_ukv, nsa_cmp_k_pe, nsa_cmp_k_w1, nsa_cmp_k_w2, nsa_cmp_v_pe, nsa_cmp_v_w1, nsa_cmp_v_w2, w_branch_mla, w_branch_nsa, w_out, ln1_g, ln1_b, peer_w_query, peer_sub_keys, peer_u, peer_v, ln2_g, ln2_b):
    B, S, D = x.shape
    tab_mla = _rope_tables(positions, MLA_ROPE_DIM)
    tab_nsa = _rope_tables(positions, NSA_HEAD_DIM)
    h, hb = _ln_in(x.reshape(B * S, D), ln_in_g, ln_in_b)
    for l in range(DEPTH):
        lw = _layer_weights(l, w_in, mla_q_norm, mla_w_uq, mla_kv_norm, mla_w_ukv, w_branch_mla, w_branch_nsa,
                            w_out, ln1_g, ln1_b, peer_w_query, peer_sub_keys, peer_u, peer_v, ln2_g, ln2_b)
        cmp_w = {"k_pe": nsa_cmp_k_pe[l], "k_w1": nsa_cmp_k_w1[l], "k_w2": nsa_cmp_k_w2[l],
                 "v_pe": nsa_cmp_v_pe[l], "v_w1": nsa_cmp_v_w1[l], "v_w2": nsa_cmp_v_w2[l]}
        h, hb = _hybrid_layer(h, hb, B, S, lw, cmp_w, tab_mla, tab_nsa)
    return h.reshape(B, S, D)
```

```python
import functools
import math

import numpy as np
import jax
import jax.numpy as jnp
from jax import lax
from jax.experimental import pallas as pl
from jax.experimental.pallas import tpu as pltpu

D_MODEL = 1024
DEPTH = 2
ROPE_THETA = 10000.0

MLA_HEADS = 8
MLA_NOPE_DIM = 64
MLA_ROPE_DIM = 32
MLA_V_DIM = 64
MLA_Q_RANK = 768
MLA_KV_RANK = 256

NSA_HEADS = 8
NSA_GROUPS = 2
NSA_HPG = NSA_HEADS // NSA_GROUPS
NSA_HEAD_DIM = 64
CMP_BLOCK = 32
CMP_STRIDE = 16
CMP_HIDDEN = 2 * NSA_HEAD_DIM
SEL_BLOCK = 64
SEL_TOPN = 16
WINDOW = 512
FORCE_SCORE = 1e9

PEER_HEADS = 8
PEER_KEY_DIM = 128
PEER_N_KEYS = 128
PEER_N_EXPERTS = PEER_N_KEYS * PEER_N_KEYS
PEER_TOPK = 16

DEEPNORM_ALPHA = (2 * DEPTH) ** 0.25
LN_EPS = 1e-5
RMS_EPS = 1e-6
NEG_BIG = -1e30
MASK_BIAS = 32768.0

IN_WIDTHS = (MLA_Q_RANK, MLA_KV_RANK, MLA_ROPE_DIM, NSA_HEADS * NSA_HEAD_DIM,
             6 * NSA_GROUPS * NSA_HEAD_DIM, 3 * NSA_HEADS, 2 * D_MODEL)

LANES = 128
VMEM_LIMIT = 56 * 1024 * 1024

F32 = jnp.float32
CDT = jnp.bfloat16

_NT = (((1,), (1,)), ((), ()))


def _dot(a, b):
    return jnp.dot(a, b, preferred_element_type=F32)


def _dot_nt(a, b):
    return lax.dot_general(a, b, _NT, preferred_element_type=F32)


def _params(*sem):
    return pltpu.CompilerParams(dimension_semantics=sem, vmem_limit_bytes=VMEM_LIMIT)


def _gelu(x):
    return 0.5 * x * (1.0 + lax.erf(x * (2.0 ** -0.5)))


def _layer_norm(z, g, b):
    mu = jnp.mean(z, axis=-1, keepdims=True)
    d = z - mu
    var = jnp.mean(d * d, axis=-1, keepdims=True)
    return d * lax.rsqrt(var + LN_EPS) * g + b


def _ln_in_kernel(x_ref, g_ref, b_ref, h_ref, hb_ref):
    h = _layer_norm(x_ref[...], g_ref[...], b_ref[...])
    h_ref[...] = h
    hb_ref[...] = h.astype(CDT)


def _ln_in(x, g, b, tm=512):
    T, D = x.shape
    row = pl.BlockSpec((tm, D), lambda i: (i, 0))
    vec = pl.BlockSpec((1, D), lambda i: (0, 0))
    return pl.pallas_call(
        _ln_in_kernel, grid=(T // tm,), in_specs=[row, vec, vec], out_specs=[row, row],
        out_shape=[jax.ShapeDtypeStruct((T, D), F32), jax.ShapeDtypeStruct((T, D), CDT)],
        compiler_params=_params("parallel"), name="ln_in")(x, g.reshape(1, D), b.reshape(1, D))


def _proj_kernel(*refs, rope, act, n_rep, has_bias):
    refs = list(refs)
    o_ref = refs.pop()
    bias_ref = refs.pop() if has_bias else None
    if rope:
        hb_ref, w_ref, wr_ref, cos_ref, sin_ref = refs
    else:
        hb_ref, w_ref = refs
    hb = hb_ref[...]
    y = _dot(hb, w_ref[...])
    if rope:
        yr = _dot(hb, wr_ref[...])
        c, s = cos_ref[...], sin_ref[...]
        if n_rep > 1:
            c = jnp.concatenate([c] * n_rep, axis=1)
            s = jnp.concatenate([s] * n_rep, axis=1)
        y = y * c + yr * s
    if act == "sigmoid":
        y = 1.0 / (1.0 + jnp.exp(-y))
    if has_bias:
        y = y + bias_ref[...]
    o_ref[...] = y.astype(o_ref.dtype)


def _proj(hb, w, out_dtype, w_rot=None, tables=None, act=None, bias=None, tm=1024, name="proj"):
    T, K = hb.shape
    N = w.shape[1]
    tn = N if N <= 768 else 512
    assert N % tn == 0 and tn % LANES == 0 and T % tm == 0
    rope = w_rot is not None
    x_spec = pl.BlockSpec((tm, K), lambda j, i: (i, 0))
    w_spec = pl.BlockSpec((K, tn), lambda j, i: (0, j))
    t_spec = pl.BlockSpec((tm, LANES), lambda j, i: (i, 0))
    o_spec = pl.BlockSpec((tm, tn), lambda j, i: (i, j))
    if rope:
        args = (hb, w, w_rot, tables[0], tables[1])
        in_specs = [x_spec, w_spec, w_spec, t_spec, t_spec]
    else:
        args = (hb, w)
        in_specs = [x_spec, w_spec]
    if bias is not None:
        args += (bias,)
        per_row = bias.shape[0] != 1
        in_specs.append(pl.BlockSpec((tm if per_row else 1, tn), lambda j, i: (i if per_row else 0, j)))
    kern = functools.partial(_proj_kernel, rope=rope, act=act, n_rep=tn // LANES, has_bias=bias is not None)
    return pl.pallas_call(
        kern, grid=(N // tn, T // tm), in_specs=in_specs, out_specs=o_spec,
        out_shape=jax.ShapeDtypeStruct((T, N), out_dtype),
        compiler_params=_params("parallel", "parallel"), name=name)(*args)


def _rms(x, g):
    return x * lax.rsqrt(jnp.mean(x * x, axis=-1, keepdims=True) + RMS_EPS) * g


def _mla_up_kernel(c_ref, kr_ref, qg_ref, kvg_ref, wqn_ref, wqr_ref, wqrr_ref, pq_ref,
                   wkn_ref, wv_ref, pk_ref, cos_ref, sin_ref, q_ref, k_ref, v_ref):
    c = c_ref[...].astype(F32)
    cqn = _rms(c[:, :MLA_Q_RANK], qg_ref[...]).astype(CDT)
    ckvn = _rms(c[:, MLA_Q_RANK:], kvg_ref[...]).astype(CDT)
    cos = jnp.concatenate([cos_ref[...]] * 2, axis=1)
    sin = jnp.concatenate([sin_ref[...]] * 2, axis=1)
    roped = (_dot(cqn, wqr_ref[...]) * cos + _dot(cqn, wqrr_ref[...]) * sin).astype(CDT)
    q_ref[...] = (_dot(cqn, wqn_ref[...]) + _dot(roped, pq_ref[...])).astype(CDT)
    k_ref[...] = (_dot(ckvn, wkn_ref[...]) + _dot(kr_ref[...], pk_ref[...])).astype(CDT)
    v_ref[...] = _dot(ckvn, wv_ref[...]).astype(CDT)


def _mla_up(cqkv, kr, lw, tables, tm=512):
    T = cqkv.shape[0]
    row = lambda n: pl.BlockSpec((tm, n), lambda i: (i, 0))
    full = lambda a: pl.BlockSpec(a.shape, lambda i: (0, 0))
    ws = (lw["q_gain"], lw["kv_gain"], lw["wqn"], lw["wqr"], lw["wqr_rot"], lw["place_q"],
          lw["wkn"], lw["wv"], lw["place_k"])
    hp = MLA_HEADS * LANES
    return pl.pallas_call(
        _mla_up_kernel, grid=(T // tm,),
        in_specs=[row(cqkv.shape[1]), row(LANES)] + [full(a) for a in ws] + [row(LANES), row(LANES)],
        out_specs=[row(hp), row(hp), row(MLA_HEADS * MLA_V_DIM)],
        out_shape=[jax.ShapeDtypeStruct((T, hp), CDT), jax.ShapeDtypeStruct((T, hp), CDT),
                   jax.ShapeDtypeStruct((T, MLA_HEADS * MLA_V_DIM), CDT)],
        compiler_params=_params("parallel"), name="mla_up")(cqkv, kr, *ws, tables[0], tables[1])


def _mla_attn_kernel(q_ref, k_ref, v_ref, o_ref, *, tq, scale):
    i = pl.program_id(2)
    outs = []
    for hd in range(2):
        q = q_ref[0, :, hd * LANES:(hd + 1) * LANES]

        def step(kt, carry, diagonal, hd=hd, q=q):
            m, l, acc = carry
            k0 = pl.multiple_of(kt * tq, tq)
            k = k_ref[0, pl.ds(k0, tq), hd * LANES:(hd + 1) * LANES]
            v = v_ref[0, pl.ds(k0, tq), :]
            s = _dot_nt(q, k) * scale
            if diagonal:
                row = lax.broadcasted_iota(jnp.int32, (tq, tq), 0)
                col = lax.broadcasted_iota(jnp.int32, (tq, tq), 1)
                s = jnp.where(col <= row, s, NEG_BIG)
            m_new = jnp.maximum(m, jnp.max(s, axis=1, keepdims=True))
            alpha = jnp.exp(m - m_new)
            p = jnp.exp(s - m_new)
            l = l * alpha + jnp.sum(p, axis=1, keepdims=True)
            acc = acc * alpha + _dot(p.astype(CDT), v)
            return m_new, l, acc

        init = (jnp.full((tq, 1), NEG_BIG, F32), jnp.zeros((tq, 1), F32), jnp.zeros((tq, LANES), F32))
        carry = lax.fori_loop(0, i, lambda kt, c: step(kt, c, False), init)
        _, l, acc = step(i, carry, True)
        outs.append(acc / l)
    lane = lax.broadcasted_iota(jnp.int32, (tq, LANES), 1)
    o_ref[0] = jnp.where(lane < MLA_V_DIM, outs[0], outs[1]).astype(o_ref.dtype)


def _mla_attn(q, k, v, tq=512):
    B, S, _ = q.shape
    scale = (MLA_NOPE_DIM + MLA_ROPE_DIM) ** -0.5
    kern = functools.partial(_mla_attn_kernel, tq=tq, scale=scale)
    return pl.pallas_call(
        kern, grid=(B, MLA_HEADS // 2, S // tq),
        in_specs=[pl.BlockSpec((1, tq, 2 * LANES), lambda b, h, i: (b, i, h)),
                  pl.BlockSpec((1, S, 2 * LANES), lambda b, h, i: (b, 0, h)),
                  pl.BlockSpec((1, S, LANES), lambda b, h, i: (b, 0, h))],
        out_specs=pl.BlockSpec((1, tq, LANES), lambda b, h, i: (b, i, h)),
        out_shape=jax.ShapeDtypeStruct((B, S, MLA_HEADS * MLA_V_DIM), CDT),
        compiler_params=_params("parallel", "parallel", "arbitrary"), name="mla_attn")(q, k, v)


def _compress_kernel(c_ref, pe_ref, w1a_ref, w1b_ref, w2_ref, o_ref, *, n_valid):
    c = c_ref[0].astype(F32)
    n = c.shape[0]
    a = _dot((c + pe_ref[0:1, :]).astype(CDT), w1a_ref[...])
    b = _dot((c + pe_ref[1:2, :]).astype(CDT), w1b_ref[...])
    hid = _gelu(a + pltpu.roll(b, n - 1, 0))
    out = _dot(hid.astype(CDT), w2_ref[...])
    row = lax.broadcasted_iota(jnp.int32, out.shape, 0)
    o_ref[0] = jnp.where(row < n_valid, out, 0.0).astype(o_ref.dtype)


def _compress(chunks, pe, w1, w2):
    BG, n, width = chunks.shape
    pe2 = pe.reshape(2, width).astype(F32)
    w1a = w1[:width].astype(CDT)
    w1b = w1[width:].astype(CDT)
    w2p = jnp.pad(w2, ((0, 0), (0, LANES - w2.shape[1]))).astype(CDT)
    full = lambda a: pl.BlockSpec(a.shape, lambda i: (0, 0))
    return pl.pallas_call(
        functools.partial(_compress_kernel, n_valid=n - 1), grid=(BG,),
        in_specs=[pl.BlockSpec((1, n, width), lambda i: (i, 0, 0)), full(pe2), full(w1a), full(w1b), full(w2p)],
        out_specs=pl.BlockSpec((1, n, LANES), lambda i: (i, 0, 0)),
        out_shape=jax.ShapeDtypeStruct((BG, n, LANES), CDT),
        compiler_params=_params("parallel"), name="nsa_compress")(chunks, pe2, w1a, w1b, w2p)


def _nsa_kernel(q_ref, kc_ref, vc_ref, ks_ref, vs_ref, kw_ref, vw_ref, g_ref, ovl_ref, o_ref,
                *, tq, n_sel, n_cmp, scale):
    i = pl.program_id(2)
    q0 = i * tq
    R = NSA_HPG
    ncp = kc_ref.shape[1]
    Q = jnp.concatenate([q_ref[0, :, r * LANES:(r + 1) * LANES] for r in range(R)], axis=0)

    s = (_dot_nt(Q, kc_ref[0]) * scale).reshape(R, tq, ncp)
    t_c = q0 + lax.broadcasted_iota(jnp.int32, (tq, ncp), 0)
    n_c = lax.broadcasted_iota(jnp.int32, (tq, ncp), 1)
    mask_c = ((n_c * CMP_STRIDE + (CMP_BLOCK - 1) <= t_c) & (n_c < n_cmp))[None]
    s = jnp.where(mask_c, s, NEG_BIG)
    p = jnp.where(mask_c, jnp.exp(s - jnp.max(s, axis=2, keepdims=True)), 0.0)
    l = jnp.sum(p, axis=2, keepdims=True)
    p = p * jnp.where(l > 0.0, 1.0 / l, 0.0)
    o_c = _dot(p.reshape(R * tq, ncp).astype(CDT), vc_ref[0])

    psum = jnp.sum(p, axis=0)
    p_hi = psum.astype(CDT)
    p_lo = (psum - p_hi.astype(F32)).astype(CDT)
    imp = (_dot_nt(ovl_ref[...], p_hi) + _dot_nt(ovl_ref[...], p_lo))[0:n_sel]
    jj = lax.broadcasted_iota(jnp.int32, (n_sel, tq), 0)
    cur = (q0 + lax.broadcasted_iota(jnp.int32, (n_sel, tq), 1)) // SEL_BLOCK
    forced = (jj == 0) | (jj == cur) | (jj == cur - 1)
    top_n = min(SEL_TOPN, n_sel)
    assert top_n > 3
    free = jnp.where(forced | (jj > cur), -FORCE_SCORE, imp)
    tops, v = [], free
    for _ in range(top_n - 1):
        m = jnp.max(v, axis=0, keepdims=True)
        tops.append(m)
        v = jnp.where(v == m, -jnp.inf, v)
    cur_row = cur[0:1, :]
    tau = jnp.where(cur_row >= 2, tops[top_n - 4], jnp.where(cur_row == 1, tops[top_n - 3], tops[top_n - 2]))
    half = LANES // 2
    parts = [jnp.zeros((half, tq), F32), jnp.where(forced | (free >= tau), 0.0, -MASK_BIAS)]
    if n_sel < half:
        parts.append(jnp.zeros((half - n_sel, tq), F32))
    q_bias = jnp.concatenate(parts, axis=0).T.astype(CDT)
    Qb = Q + jnp.concatenate([q_bias] * R, axis=0)

    hidden = -MASK_BIAS * scale
    row = lax.broadcasted_iota(jnp.int32, (tq, tq), 0)
    col = lax.broadcasted_iota(jnp.int32, (tq, tq), 1)
    causal_bias = jnp.where(col <= row, 0.0, hidden)
    window_bias = jnp.where(col > row, 0.0, hidden)

    def step(k_ref, v_ref, kt, carry, bias):
        m, acc = carry
        k0 = pl.multiple_of(kt * tq, tq)
        s = (_dot_nt(Qb, k_ref[0, pl.ds(k0, tq), :]) * scale).reshape(R, tq, tq) + bias[None]
        m_new = jnp.maximum(m, jnp.max(s, axis=2, keepdims=True))
        p = jnp.exp(s - m_new).reshape(R * tq, tq).astype(CDT)
        pv = _dot(p, v_ref[0, pl.ds(k0, tq), :]).reshape(R, tq, LANES)
        return m_new, acc * jnp.exp(m - m_new) + pv

    def finish(carry):
        acc = carry[1]
        return acc * (1.0 / acc[:, :, half:half + 1])

    def flag(cond):
        return jnp.where(cond, 1.0, 0.0)

    init = (jnp.full((R, tq, 1), NEG_BIG, F32), jnp.zeros((R, tq, LANES), F32))

    def sel_step(kt, c):
        return step(ks_ref, vs_ref, kt, c, causal_bias * flag(kt == i))

    o_s = finish(lax.fori_loop(0, i + 1, sel_step, init))

    n_back = WINDOW // tq

    def win_step(it, c):
        kt = i - n_back + it
        bias = window_bias * flag(it == 0) + causal_bias * flag(it == n_back) + hidden * flag(kt < 0)
        return step(kw_ref, vw_ref, jnp.maximum(kt, 0), c, bias)

    o_w = finish(lax.fori_loop(0, n_back + 1, win_step, init))

    g = g_ref[0]
    o_c = o_c.reshape(R, tq, LANES)
    for r in range(R):
        o = (g[:, 3 * r:3 * r + 1] * o_c[r] + g[:, 3 * r + 1:3 * r + 2] * o_s[r]
             + g[:, 3 * r + 2:3 * r + 3] * o_w[r])
        o_ref[0, :, r * LANES:(r + 1) * LANES] = o.astype(o_ref.dtype)


def _nsa(q, kc, vc, ks, vs, kw, vw, gates, tq=512):
    B, S, _ = q.shape
    G = NSA_GROUPS
    n_sel = S // SEL_BLOCK
    n_cmp = (S - CMP_BLOCK) // CMP_STRIDE + 1
    ncp = kc.shape[1]
    assert n_sel <= LANES // 2 and n_sel % 8 == 0 and ncp % LANES == 0 and WINDOW % tq == 0
    jn = np.arange(LANES)[:, None] * SEL_BLOCK
    cn = np.arange(ncp)[None, :] * CMP_STRIDE
    ovl = ((cn <= jn + SEL_BLOCK - 1) & (cn + CMP_BLOCK - 1 >= jn)
           & (np.arange(LANES)[:, None] < n_sel) & (np.arange(ncp)[None, :] < n_cmp))
    ovl = jnp.asarray(ovl, CDT)
    kern = functools.partial(_nsa_kernel, tq=tq, n_sel=n_sel, n_cmp=n_cmp, scale=NSA_HEAD_DIM ** -0.5)
    cmp_spec = pl.BlockSpec((1, ncp, LANES), lambda b, g, i: (b * G + g, 0, 0))
    kv_spec = pl.BlockSpec((1, S, LANES), lambda b, g, i: (b, 0, g))
    qo_spec = pl.BlockSpec((1, tq, NSA_HPG * LANES), lambda b, g, i: (b, i, g))
    return pl.pallas_call(
        kern, grid=(B, G, S // tq),
        in_specs=[qo_spec, cmp_spec, cmp_spec, kv_spec, kv_spec, kv_spec, kv_spec,
                  pl.BlockSpec((1, tq, LANES), lambda b, g, i: (b, i, g)),
                  pl.BlockSpec(ovl.shape, lambda b, g, i: (0, 0))],
        out_specs=qo_spec,
        out_shape=jax.ShapeDtypeStruct((B, S, NSA_HEADS * LANES), CDT),
        compiler_params=_params("parallel", "parallel", "arbitrary"), name="nsa_attn")(
            q, kc, vc, ks, vs, kw, vw, gates, ovl)


def _merge_kernel(om_ref, on_ref, mg_ref, h_ref, wbm_ref, wbn_ref, wo_ref, g_ref, b_ref, x_ref, xb_ref):
    D = D_MODEL
    y_mla = _dot(om_ref[...], wbm_ref[...])
    y_nsa = _dot(on_ref[...], wbn_ref[...])
    mg = mg_ref[...].astype(F32)
    mixed = _dot((mg[:, :D] * y_mla + mg[:, D:] * y_nsa).astype(CDT), wo_ref[...])
    x = _layer_norm(DEEPNORM_ALPHA * h_ref[...] + mixed, g_ref[...], b_ref[...])
    x_ref[...] = x
    xb_ref[...] = x.astype(CDT)


def _merge(o_mla, o_nsa, mg, h, lw, tm=512):
    T, D = h.shape
    row = lambda n: pl.BlockSpec((tm, n), lambda i: (i, 0))
    full = lambda a: pl.BlockSpec(a.shape, lambda i: (0, 0))
    ws = (lw["w_bm"], lw["w_bn"], lw["w_out"], lw["ln1_g"], lw["ln1_b"])
    return pl.pallas_call(
        _merge_kernel, grid=(T // tm,),
        in_specs=[row(o_mla.shape[1]), row(o_nsa.shape[1]), row(2 * D), row(D)] + [full(a) for a in ws],
        out_specs=[row(D), row(D)],
        out_shape=[jax.ShapeDtypeStruct((T, D), F32), jax.ShapeDtypeStruct((T, D), CDT)],
        compiler_params=_params("parallel"), name="merge_ln1")(o_mla, o_nsa, mg, h, *ws)


def _rows_per_vreg():
    return 8 * (4 // jnp.dtype(CDT).itemsize)


def _pack_words(x):
    if jnp.dtype(CDT).itemsize == 4:
        return pltpu.bitcast(x, jnp.uint32)
    bits = pltpu.bitcast(x.astype(CDT).astype(F32), jnp.uint32)
    return (bits >> 16) | (bits & jnp.uint32(0xFFFF0000))


def _unpack_words(row):
    return pltpu.bitcast(jnp.broadcast_to(row, (8, LANES)), CDT)


_PEER_PAIRS = [(a, b) for a in range(PEER_TOPK) for b in range(PEER_TOPK) if (a + 1) * (b + 1) <= PEER_TOPK]
_N_CAND = -(-len(_PEER_PAIRS) // 8) * 8


def _top_values(v, want_rank):
    tops = []
    rank = jnp.full(v.shape, 127.0, F32) if want_rank else None
    for r in range(PEER_TOPK):
        m = jnp.max(v, axis=0, keepdims=True)
        tops.append(m)
        eq = v == m
        if want_rank:
            rank = jnp.where(eq, float(r), rank)
        v = jnp.where(eq, -jnp.inf, v)
    return tops, rank


def _peer_stats_kernel(xb_ref, wq_ref, keys_ref, c1_ref, e1_ref, r2_ref, e2_ref, s_ref, cand_ref):
    qp = _dot(xb_ref[...], wq_ref[...]).astype(CDT)
    s_ref[...] = _dot_nt(keys_ref[...], qp)
    n = PEER_N_KEYS
    cand_ref[...] = jnp.full(cand_ref.shape, -jnp.inf, F32)
    for h in range(PEER_HEADS):
        s1 = s_ref[(2 * h) * n:(2 * h + 1) * n, :]
        s2 = s_ref[(2 * h + 1) * n:(2 * h + 2) * n, :]
        t1, _ = _top_values(s1, False)
        t2, rank2 = _top_values(s2, True)
        for c, (a, b) in enumerate(_PEER_PAIRS):
            cand_ref[c:c + 1, :] = t1[a] + t2[b]
        cand = cand_ref[...]
        tc, _ = _top_values(cand, False)
        tau = tc[PEER_TOPK - 1]
        top = t1[0] + t2[0]
        z = jnp.sum(jnp.where(cand >= tau, jnp.exp(cand - top), 0.0), axis=0, keepdims=True)
        cnt = jnp.zeros(s1.shape, F32)
        for b in range(PEER_TOPK):
            cnt = cnt + jnp.where(s1 + t2[b] >= tau, 1.0, 0.0)
        c1_ref[h] = _pack_words(cnt)
        e1_ref[h] = _pack_words(jnp.exp(s1 - t1[0]))
        r2_ref[h] = rank2.astype(r2_ref.dtype)
        e2_ref[h] = (jnp.exp(s2 - t2[0]) / z).astype(e2_ref.dtype)


def _peer_stats(xb, wq, keys_bd_t, tm=256):
    T, D = xb.shape
    H, n = PEER_HEADS, PEER_N_KEYS
    full = lambda a: pl.BlockSpec(a.shape, lambda i: (0, 0))
    o_spec = pl.BlockSpec((H, n, tm), lambda i: (0, 0, i))
    o_u32 = jax.ShapeDtypeStruct((H, n, T), jnp.uint32)
    o_cdt = jax.ShapeDtypeStruct((H, n, T), CDT)
    return pl.pallas_call(
        _peer_stats_kernel, grid=(T // tm,),
        in_specs=[pl.BlockSpec((tm, D), lambda i: (i, 0)), full(wq), full(keys_bd_t)],
        out_specs=[o_spec] * 4, out_shape=[o_u32, o_u32, o_cdt, o_cdt],
        scratch_shapes=[pltpu.VMEM((2 * H * n, tm), F32), pltpu.VMEM((_N_CAND, tm), F32)],
        compiler_params=_params("parallel"), name="peer_stats")(xb, wq, keys_bd_t)


def _peer_dense_kernel(xb_ref, x_ref, u_ref, vt_ref, c1_ref, e1_ref, r2_in_ref, e2_in_ref, g_ref, b_ref,
                       o_ref, ob_ref, acc_ref, a_ref, h_ref, r2_ref, e2_ref, *, tm, te):
    j = pl.program_id(1)
    n = PEER_N_KEYS
    rows_per_tile = te // n

    @pl.when(j == 0)
    def _():
        acc_ref[...] = jnp.zeros(acc_ref.shape, F32)
        r2_ref[...] = r2_in_ref[...]
        e2_ref[...] = e2_in_ref[...]

    a_ref[...] = _gelu(_dot_nt(u_ref[...], xb_ref[...])).astype(CDT)
    ic = n // 2
    sub = _rows_per_vreg()
    nv = ic // sub
    zero = jnp.zeros((nv, sub, LANES), CDT)
    n_i2 = n // ic

    def gate_block(idx):
        sl = pl.ds((idx // n_i2) * LANES, LANES)
        i2 = (idx % n_i2) * ic
        ws = [zero] * rows_per_tile
        for h in range(PEER_HEADS):
            r2 = r2_ref[h, pl.ds(i2, ic), sl].reshape(nv, sub, LANES)
            e2 = e2_ref[h, pl.ds(i2, ic), sl].reshape(nv, sub, LANES)
            for r in range(rows_per_tile):
                c1 = _unpack_words(c1_ref[h, r:r + 1, sl])[None]
                e1 = _unpack_words(e1_ref[h, r:r + 1, sl])[None]
                ws[r] = ws[r] + jnp.maximum(jnp.minimum(e1 * e2, c1 - r2), zero)
        for r in range(rows_per_tile):
            rows = pl.ds(r * n + i2, ic)
            h_ref[rows, sl] = ws[r].reshape(ic, LANES) * a_ref[rows, sl]

    for idx in range((tm // LANES) * n_i2):
        gate_block(idx)
    acc_ref[...] += _dot(vt_ref[...], h_ref[...])

    @pl.when(j == pl.num_programs(1) - 1)
    def _():
        z = DEEPNORM_ALPHA * x_ref[...] + acc_ref[...].T
        out = _layer_norm(z, g_ref[...], b_ref[...])
        o_ref[...] = out
        ob_ref[...] = out.astype(CDT)


def _peer_dense(xb, x, u, vt, stats, ln_g, ln_b, tm=512, te=1024):
    T, D = x.shape
    E = u.shape[0]
    H, n = PEER_HEADS, PEER_N_KEYS
    row = pl.BlockSpec((tm, D), lambda i, j: (i, 0))
    st = pl.BlockSpec((H, n, tm), lambda i, j: (0, 0, i))
    st1 = pl.BlockSpec((H, te // n, tm), lambda i, j: (0, j, i))
    vec = pl.BlockSpec((1, D), lambda i, j: (0, 0))
    assert (te // n) % 8 == 0
    kern = functools.partial(_peer_dense_kernel, tm=tm, te=te)
    return pl.pallas_call(
        kern, grid=(T // tm, E // te),
        in_specs=[row, row, pl.BlockSpec((te, D), lambda i, j: (j, 0)), pl.BlockSpec((D, te), lambda i, j: (0, j)),
                  st1, st1, st, st, vec, vec],
        out_specs=[row, row],
        out_shape=[jax.ShapeDtypeStruct((T, D), F32), jax.ShapeDtypeStruct((T, D), CDT)],
        scratch_shapes=[pltpu.VMEM((D, tm), F32), pltpu.VMEM((te, tm), CDT), pltpu.VMEM((te, tm), CDT),
                        pltpu.VMEM((H, n, tm), CDT), pltpu.VMEM((H, n, tm), CDT)],
        compiler_params=_params("parallel", "arbitrary"), name="peer_dense")(
            xb, x, u, vt, *stats, ln_g, ln_b)


def _rot_cols(w, dim):
    k, n = w.shape
    w4 = w.reshape(k, n // dim, 2, dim // 2)
    return jnp.stack([-w4[:, :, 1], w4[:, :, 0]], axis=2).reshape(k, n)


def _pad_heads(w, dim):
    k, n = w.shape
    w3 = w.reshape(k, n // dim, dim)
    return jnp.pad(w3, ((0, 0), (0, 0), (0, LANES - dim))).reshape(k, (n // dim) * LANES)


def _pad_cols(w, n):
    return jnp.pad(w, ((0, 0), (0, n - w.shape[1])))


def _layer_weights(l, w_in, mla_q_norm, mla_w_uq, mla_kv_norm, mla_w_ukv, w_branch_mla, w_branch_nsa, w_out,
                   ln1_g, ln1_b, peer_w_query, peer_sub_keys, peer_u, peer_v, ln2_g, ln2_b):
    off = np.cumsum((0,) + IN_WIDTHS)
    wi = w_in[l]
    seg = lambda k: wi[:, off[k]:off[k + 1]]
    G, dk = NSA_GROUPS, NSA_HEAD_DIM
    kv = seg(4).reshape(D_MODEL, 6, G * dk)
    lw = {}
    lw["w_cqkv"] = jnp.concatenate([seg(0), seg(1)], axis=1).astype(CDT)
    kr = seg(2)
    lw["w_kr"] = _pad_cols(kr, LANES).astype(CDT)
    lw["w_kr_rot"] = _pad_cols(_rot_cols(kr, MLA_ROPE_DIM), LANES).astype(CDT)
    lw["w_nq"] = _pad_heads(seg(3), dk).astype(CDT)
    lw["w_nq_rot"] = _pad_heads(_rot_cols(seg(3), dk), dk).astype(CDT)
    lw["w_kcmp"] = kv[:, 0].astype(CDT)
    lw["w_kcmp_rot"] = _rot_cols(kv[:, 0], dk).astype(CDT)
    for name, idx in (("ksel", 2), ("kwin", 4)):
        lw["w_" + name] = _pad_heads(kv[:, idx], dk).astype(CDT)
        lw["w_" + name + "_rot"] = _pad_heads(_rot_cols(kv[:, idx], dk), dk).astype(CDT)
    lw["w_vall"] = jnp.concatenate([kv[:, 1], _pad_heads(kv[:, 3], dk), _pad_heads(kv[:, 5], dk)], axis=1).astype(CDT)
    ng = seg(5).reshape(D_MODEL, G, NSA_HPG * 3)
    lw["w_ngate"] = jnp.pad(ng, ((0, 0), (0, 0), (0, LANES - NSA_HPG * 3))).reshape(D_MODEL, G * LANES).astype(CDT)
    lw["w_mgate"] = seg(6).astype(CDT)

    H = MLA_HEADS
    dq = MLA_NOPE_DIM + MLA_ROPE_DIM
    uq = mla_w_uq[l].reshape(MLA_Q_RANK, H, dq)
    lw["wqn"] = _pad_heads(uq[:, :, :MLA_NOPE_DIM].reshape(MLA_Q_RANK, -1), MLA_NOPE_DIM).astype(CDT)
    wqr = uq[:, :, MLA_NOPE_DIM:].reshape(MLA_Q_RANK, H * MLA_ROPE_DIM)
    lw["wqr"] = wqr.astype(CDT)
    lw["wqr_rot"] = _rot_cols(wqr, MLA_ROPE_DIM).astype(CDT)
    pq = np.zeros((H * MLA_ROPE_DIM, H * LANES), np.float32)
    pk = np.zeros((LANES, H * LANES), np.float32)
    for h in range(H):
        for j in range(MLA_ROPE_DIM):
            pq[h * MLA_ROPE_DIM + j, h * LANES + MLA_NOPE_DIM + j] = 1.0
            pk[j, h * LANES + MLA_NOPE_DIM + j] = 1.0
    lw["place_q"] = jnp.asarray(pq, CDT)
    lw["place_k"] = jnp.asarray(pk, CDT)
    ukv = mla_w_ukv[l].reshape(MLA_KV_RANK, H, MLA_NOPE_DIM + MLA_V_DIM)
    lw["wkn"] = _pad_heads(ukv[:, :, :MLA_NOPE_DIM].reshape(MLA_KV_RANK, -1), MLA_NOPE_DIM).astype(CDT)
    lw["wv"] = ukv[:, :, MLA_NOPE_DIM:].reshape(MLA_KV_RANK, H * MLA_V_DIM).astype(CDT)
    lw["q_gain"] = mla_q_norm[l].reshape(1, -1)
    lw["kv_gain"] = mla_kv_norm[l].reshape(1, -1)

    lw["w_bm"] = w_branch_mla[l].astype(CDT)
    wbn = w_branch_nsa[l].reshape(NSA_HEADS, dk, D_MODEL)
    lw["w_bn"] = jnp.pad(wbn, ((0, 0), (0, LANES - dk), (0, 0))).reshape(NSA_HEADS * LANES, D_MODEL).astype(CDT)
    lw["w_out"] = w_out[l].astype(CDT)
    lw["ln1_g"] = ln1_g[l].reshape(1, -1)
    lw["ln1_b"] = ln1_b[l].reshape(1, -1)

    lw["w_pq"] = peer_w_query[l].astype(CDT)
    sk = peer_sub_keys[l].reshape(PEER_HEADS * 2, PEER_N_KEYS, PEER_KEY_DIM // 2)
    eye = jnp.eye(PEER_HEADS * 2, dtype=sk.dtype)
    lw["keys_bd_t"] = jnp.einsum("gnd,gf->gnfd", sk, eye).reshape(
        PEER_HEADS * 2 * PEER_N_KEYS, PEER_HEADS * PEER_KEY_DIM).astype(CDT)
    lw["u"] = peer_u[l].astype(CDT)
    lw["vt"] = peer_v[l].T.astype(CDT)
    lw["ln2_g"] = ln2_g[l].reshape(1, -1)
    lw["ln2_b"] = ln2_b[l].reshape(1, -1)
    return lw


def _rope_tables(positions, dim):
    inv_freq = ROPE_THETA ** (-jnp.arange(0, dim, 2, dtype=F32) / dim)
    ang = positions.astype(F32).reshape(-1, 1) * inv_freq
    rep = lambda t: jnp.tile(jnp.concatenate([t, t], axis=-1), (1, LANES // dim))
    return rep(jnp.cos(ang)), rep(jnp.sin(ang))


def _hybrid_layer(h, hb, B, S, lw, cmp_w, tab_mla, tab_nsa):
    T = B * S
    G, dk = NSA_GROUPS, NSA_HEAD_DIM
    cqkv = _proj(hb, lw["w_cqkv"], CDT, name="proj_cqkv")
    kr = _proj(hb, lw["w_kr"], CDT, lw["w_kr_rot"], tab_mla, name="proj_kr")
    nq = _proj(hb, lw["w_nq"], CDT, lw["w_nq_rot"], tab_nsa, name="proj_nq")
    kcmp = _proj(hb, lw["w_kcmp"], CDT, lw["w_kcmp_rot"], tab_nsa, name="proj_kcmp")
    lane = jnp.arange(G * LANES) % LANES - LANES // 2
    blk_onehot = (jnp.arange(S)[:, None] // SEL_BLOCK == lane[None, :]).astype(F32)
    ksel = _proj(hb, lw["w_ksel"], CDT, lw["w_ksel_rot"], tab_nsa, bias=jnp.tile(blk_onehot, (B, 1)),
                 name="proj_ksel")
    kwin = _proj(hb, lw["w_kwin"], CDT, lw["w_kwin_rot"], tab_nsa, name="proj_kwin")
    ones_lane = np.zeros((1, G * dk + 2 * G * LANES), np.float32)
    ones_lane[0, [G * dk + g * LANES + dk for g in range(2 * G)]] = 1.0
    vall = _proj(hb, lw["w_vall"], CDT, bias=jnp.asarray(ones_lane), name="proj_v")
    ngate = _proj(hb, lw["w_ngate"], F32, act="sigmoid", name="proj_ngate")
    mgate = _proj(hb, lw["w_mgate"], CDT, act="sigmoid", name="proj_mgate")

    q, k, v = _mla_up(cqkv, kr, lw, tab_mla)
    o_mla = _mla_attn(q.reshape(B, S, -1), k.reshape(B, S, -1), v.reshape(B, S, -1))

    def chunks(a):
        return a.reshape(B, S, G, dk).transpose(0, 2, 1, 3).reshape(B * G, S // CMP_STRIDE, CMP_STRIDE * dk)

    kc = _compress(chunks(kcmp), cmp_w["k_pe"], cmp_w["k_w1"], cmp_w["k_w2"])
    vc = _compress(chunks(vall[:, :G * dk]), cmp_w["v_pe"], cmp_w["v_w1"], cmp_w["v_w2"])
    vsel = vall[:, G * dk:G * dk + G * LANES]
    vwin = vall[:, G * dk + G * LANES:]
    r3 = lambda a: a.reshape(B, S, -1)
    o_nsa = _nsa(r3(nq), kc, vc, r3(ksel), r3(vsel), r3(kwin), r3(vwin), r3(ngate))

    x1, x1b = _merge(o_mla.reshape(T, -1), o_nsa.reshape(T, -1), mgate, h, lw)
    stats = _peer_stats(x1b, lw["w_pq"], lw["keys_bd_t"])
    return _peer_dense(x1b, x1, lw["u"], lw["vt"], stats, lw["ln2_g"], lw["ln2_b"])


def kernel(x, positions, ln_in_g, ln_in_b, w_in, mla_q_norm, mla_w_uq, mla_kv_norm, mla_w_ukv, nsa_cmp_k_pe, nsa_cmp_k_w1, nsa_cmp_k_w2, nsa_cmp_v_pe, nsa_cmp_v_w1, nsa_cmp_v_w2, w_branch_mla, w_branch_nsa, w_out, ln1_g, ln1_b, peer_w_query, peer_sub_keys, peer_u, peer_v, ln2_g, ln2_b):
    B, S, D = x.shape
    tab_mla = _rope_tables(positions, MLA_ROPE_DIM)
    tab_nsa = _rope_tables(positions, NSA_HEAD_DIM)
    h, hb = _ln_in(x.reshape(B * S, D), ln_in_g, ln_in_b)
    for l in range(DEPTH):
        lw = _layer_weights(l, w_in, mla_q_norm, mla_w_uq, mla_kv_norm, mla_w_ukv, w_branch_mla, w_branch_nsa,
                            w_out, ln1_g, ln1_b, peer_w_query, peer_sub_keys, peer_u, peer_v, ln2_g, ln2_b)
        cmp_w = {"k_pe": nsa_cmp_k_pe[l], "k_w1": nsa_cmp_k_w1[l], "k_w2": nsa_cmp_k_w2[l],
                 "v_pe": nsa_cmp_v_pe[l], "v_w1": nsa_cmp_v_w1[l], "v_w2": nsa_cmp_v_w2[l]}
        h, hb = _hybrid_layer(h, hb, B, S, lw, cmp_w, tab_mla, tab_nsa)
    return h.reshape(B, S, D)
```

```python
import functools
import math

import numpy as np
import jax
import jax.numpy as jnp
from jax import lax
from jax.experimental import pallas as pl
from jax.experimental.pallas import tpu as pltpu

D_MODEL = 1024
DEPTH = 2
ROPE_THETA = 10000.0

MLA_HEADS = 8
MLA_NOPE_DIM = 64
MLA_ROPE_DIM = 32
MLA_V_DIM = 64
MLA_Q_RANK = 768
MLA_KV_RANK = 256

NSA_HEADS = 8
NSA_GROUPS = 2
NSA_HPG = NSA_HEADS // NSA_GROUPS
NSA_HEAD_DIM = 64
CMP_BLOCK = 32
CMP_STRIDE = 16
CMP_HIDDEN = 2 * NSA_HEAD_DIM
SEL_BLOCK = 64
SEL_TOPN = 16
WINDOW = 512
FORCE_SCORE = 1e9

PEER_HEADS = 8
PEER_KEY_DIM = 128
PEER_N_KEYS = 128
PEER_N_EXPERTS = PEER_N_KEYS * PEER_N_KEYS
PEER_TOPK = 16

DEEPNORM_ALPHA = (2 * DEPTH) ** 0.25
LN_EPS = 1e-5
RMS_EPS = 1e-6
NEG_BIG = -1e30
LOG2E = math.log2(math.e)
MASK_BIAS = 32768.0

IN_WIDTHS = (MLA_Q_RANK, MLA_KV_RANK, MLA_ROPE_DIM, NSA_HEADS * NSA_HEAD_DIM,
             6 * NSA_GROUPS * NSA_HEAD_DIM, 3 * NSA_HEADS, 2 * D_MODEL)

LANES = 128
VMEM_LIMIT = 56 * 1024 * 1024

F32 = jnp.float32
CDT = jnp.bfloat16

_NT = (((1,), (1,)), ((), ()))


def _dot(a, b):
    return jnp.dot(a, b, preferred_element_type=F32)


def _dot_nt(a, b):
    return lax.dot_general(a, b, _NT, preferred_element_type=F32)


def _params(*sem):
    return pltpu.CompilerParams(dimension_semantics=sem, vmem_limit_bytes=VMEM_LIMIT)


def _gelu(x):
    return 0.5 * x * (1.0 + lax.erf(x * (2.0 ** -0.5)))


def _layer_norm(z, g, b):
    mu = jnp.mean(z, axis=-1, keepdims=True)
    d = z - mu
    var = jnp.mean(d * d, axis=-1, keepdims=True)
    return d * lax.rsqrt(var + LN_EPS) * g + b


def _ln_in_kernel(x_ref, g_ref, b_ref, h_ref, hb_ref):
    h = _layer_norm(x_ref[...], g_ref[...], b_ref[...])
    h_ref[...] = h
    hb_ref[...] = h.astype(CDT)


def _ln_in(x, g, b, tm=512):
    T, D = x.shape
    row = pl.BlockSpec((tm, D), lambda i: (i, 0))
    vec = pl.BlockSpec((1, D), lambda i: (0, 0))
    return pl.pallas_call(
        _ln_in_kernel, grid=(T // tm,), in_specs=[row, vec, vec], out_specs=[row, row],
        out_shape=[jax.ShapeDtypeStruct((T, D), F32), jax.ShapeDtypeStruct((T, D), CDT)],
        compiler_params=_params("parallel"), name="ln_in")(x, g.reshape(1, D), b.reshape(1, D))


def _proj_kernel(*refs, rope, act, n_rep, has_bias):
    refs = list(refs)
    o_ref = refs.pop()
    bias_ref = refs.pop() if has_bias else None
    if rope:
        hb_ref, w_ref, wr_ref, cos_ref, sin_ref = refs
    else:
        hb_ref, w_ref = refs
    hb = hb_ref[...]
    y = _dot(hb, w_ref[...])
    if rope:
        yr = _dot(hb, wr_ref[...])
        c, s = cos_ref[...], sin_ref[...]
        if n_rep > 1:
            c = jnp.concatenate([c] * n_rep, axis=1)
            s = jnp.concatenate([s] * n_rep, axis=1)
        y = y * c + yr * s
    if act == "sigmoid":
        y = 1.0 / (1.0 + jnp.exp(-y))
    if has_bias:
        y = y + bias_ref[...]
    o_ref[...] = y.astype(o_ref.dtype)


def _proj(hb, w, out_dtype, w_rot=None, tables=None, act=None, bias=None, tm=1024, name="proj"):
    T, K = hb.shape
    N = w.shape[1]
    tn = N if N <= 768 else 512
    assert N % tn == 0 and tn % LANES == 0 and T % tm == 0
    rope = w_rot is not None
    x_spec = pl.BlockSpec((tm, K), lambda j, i: (i, 0))
    w_spec = pl.BlockSpec((K, tn), lambda j, i: (0, j))
    t_spec = pl.BlockSpec((tm, LANES), lambda j, i: (i, 0))
    o_spec = pl.BlockSpec((tm, tn), lambda j, i: (i, j))
    if rope:
        args = (hb, w, w_rot, tables[0], tables[1])
        in_specs = [x_spec, w_spec, w_spec, t_spec, t_spec]
    else:
        args = (hb, w)
        in_specs = [x_spec, w_spec]
    if bias is not None:
        args += (bias,)
        per_row = bias.shape[0] != 1
        in_specs.append(pl.BlockSpec((tm if per_row else 1, tn), lambda j, i: (i if per_row else 0, j)))
    kern = functools.partial(_proj_kernel, rope=rope, act=act, n_rep=tn // LANES, has_bias=bias is not None)
    return pl.pallas_call(
        kern, grid=(N // tn, T // tm), in_specs=in_specs, out_specs=o_spec,
        out_shape=jax.ShapeDtypeStruct((T, N), out_dtype),
        compiler_params=_params("parallel", "parallel"), name=name)(*args)


def _rms(x, g):
    return x * lax.rsqrt(jnp.mean(x * x, axis=-1, keepdims=True) + RMS_EPS) * g


def _mla_up_kernel(c_ref, kr_ref, qg_ref, kvg_ref, wqn_ref, wqr_ref, wqrr_ref, pq_ref,
                   wkn_ref, wv_ref, pk_ref, cos_ref, sin_ref, q_ref, k_ref, v_ref):
    c = c_ref[...].astype(F32)
    cqn = _rms(c[:, :MLA_Q_RANK], qg_ref[...]).astype(CDT)
    ckvn = _rms(c[:, MLA_Q_RANK:], kvg_ref[...]).astype(CDT)
    cos = jnp.concatenate([cos_ref[...]] * 2, axis=1)
    sin = jnp.concatenate([sin_ref[...]] * 2, axis=1)
    roped = (_dot(cqn, wqr_ref[...]) * cos + _dot(cqn, wqrr_ref[...]) * sin).astype(CDT)
    q_ref[...] = (_dot(cqn, wqn_ref[...]) + _dot(roped, pq_ref[...])).astype(CDT)
    k_ref[...] = (_dot(ckvn, wkn_ref[...]) + _dot(kr_ref[...], pk_ref[...])).astype(CDT)
    v_ref[...] = _dot(ckvn, wv_ref[...]).astype(CDT)


def _mla_up(cqkv, kr, lw, tables, tm=512):
    T = cqkv.shape[0]
    row = lambda n: pl.BlockSpec((tm, n), lambda i: (i, 0))
    full = lambda a: pl.BlockSpec(a.shape, lambda i: (0, 0))
    ws = (lw["q_gain"], lw["kv_gain"], lw["wqn"], lw["wqr"], lw["wqr_rot"], lw["place_q"],
          lw["wkn"], lw["wv"], lw["place_k"])
    hp = MLA_HEADS * LANES
    return pl.pallas_call(
        _mla_up_kernel, grid=(T // tm,),
        in_specs=[row(cqkv.shape[1]), row(LANES)] + [full(a) for a in ws] + [row(LANES), row(LANES)],
        out_specs=[row(hp), row(hp), row(MLA_HEADS * MLA_V_DIM)],
        out_shape=[jax.ShapeDtypeStruct((T, hp), CDT), jax.ShapeDtypeStruct((T, hp), CDT),
                   jax.ShapeDtypeStruct((T, MLA_HEADS * MLA_V_DIM), CDT)],
        compiler_params=_params("parallel"), name="mla_up")(cqkv, kr, *ws, tables[0], tables[1])


def _mla_attn_kernel(q_ref, k_ref, v_ref, o_ref, *, tq, scale):
    i = pl.program_id(2)
    scale2 = scale * LOG2E
    row = lax.broadcasted_iota(jnp.int32, (tq, tq), 0)
    col = lax.broadcasted_iota(jnp.int32, (tq, tq), 1)
    causal_bias = jnp.where(col <= row, 0.0, NEG_BIG)
    outs = []
    for hd in range(2):
        q = q_ref[0, :, hd * LANES:(hd + 1) * LANES]

        def step(kt, carry, bias, hd=hd, q=q):
            m, l, acc = carry
            k0 = pl.multiple_of(kt * tq, tq)
            k = k_ref[0, pl.ds(k0, tq), hd * LANES:(hd + 1) * LANES]
            v = v_ref[0, pl.ds(k0, tq), :]
            s = _dot_nt(q, k) * scale2
            if bias is not None:
                s = s + bias
            m_new = jnp.maximum(m, jnp.max(s, axis=1, keepdims=True))
            alpha = jnp.exp2(m - m_new)
            p = jnp.exp2(s - m_new)
            l = l * alpha + jnp.sum(p, axis=1, keepdims=True)
            acc = acc * alpha + _dot(p.astype(CDT), v)
            return m_new, l, acc

        init = (jnp.full((tq, 1), NEG_BIG, F32), jnp.zeros((tq, 1), F32), jnp.zeros((tq, LANES), F32))
        carry = lax.fori_loop(0, i, lambda kt, c: step(kt, c, None), init)
        _, l, acc = lax.fori_loop(i, i + 1, lambda kt, c: step(kt, c, causal_bias), carry)
        outs.append(acc / l)
    lane = lax.broadcasted_iota(jnp.int32, (tq, LANES), 1)
    o_ref[0] = jnp.where(lane < MLA_V_DIM, outs[0], outs[1]).astype(o_ref.dtype)


def _mla_attn(q, k, v, tq=512):
    B, S, _ = q.shape
    scale = (MLA_NOPE_DIM + MLA_ROPE_DIM) ** -0.5
    kern = functools.partial(_mla_attn_kernel, tq=tq, scale=scale)
    return pl.pallas_call(
        kern, grid=(B, MLA_HEADS // 2, S // tq),
        in_specs=[pl.BlockSpec((1, tq, 2 * LANES), lambda b, h, i: (b, i, h)),
                  pl.BlockSpec((1, S, 2 * LANES), lambda b, h, i: (b, 0, h)),
                  pl.BlockSpec((1, S, LANES), lambda b, h, i: (b, 0, h))],
        out_specs=pl.BlockSpec((1, tq, LANES), lambda b, h, i: (b, i, h)),
        out_shape=jax.ShapeDtypeStruct((B, S, MLA_HEADS * MLA_V_DIM), CDT),
        compiler_params=_params("parallel", "parallel", "arbitrary"), name="mla_attn")(q, k, v)


def _compress_kernel(c_ref, pe_ref, w1a_ref, w1b_ref, w2_ref, o_ref, *, n_valid):
    c = c_ref[0].astype(F32)
    n = c.shape[0]
    a = _dot((c + pe_ref[0:1, :]).astype(CDT), w1a_ref[...])
    b = _dot((c + pe_ref[1:2, :]).astype(CDT), w1b_ref[...])
    hid = _gelu(a + pltpu.roll(b, n - 1, 0))
    out = _dot(hid.astype(CDT), w2_ref[...])
    row = lax.broadcasted_iota(jnp.int32, out.shape, 0)
    o_ref[0] = jnp.where(row < n_valid, out, 0.0).astype(o_ref.dtype)


def _compress(chunks, pe, w1, w2):
    BG, n, width = chunks.shape
    pe2 = pe.reshape(2, width).astype(F32)
    w1a = w1[:width].astype(CDT)
    w1b = w1[width:].astype(CDT)
    w2p = jnp.pad(w2, ((0, 0), (0, LANES - w2.shape[1]))).astype(CDT)
    full = lambda a: pl.BlockSpec(a.shape, lambda i: (0, 0))
    return pl.pallas_call(
        functools.partial(_compress_kernel, n_valid=n - 1), grid=(BG,),
        in_specs=[pl.BlockSpec((1, n, width), lambda i: (i, 0, 0)), full(pe2), full(w1a), full(w1b), full(w2p)],
        out_specs=pl.BlockSpec((1, n, LANES), lambda i: (i, 0, 0)),
        out_shape=jax.ShapeDtypeStruct((BG, n, LANES), CDT),
        compiler_params=_params("parallel"), name="nsa_compress")(chunks, pe2, w1a, w1b, w2p)


def _nsa_kernel(q_ref, kc_ref, vc_ref, ks_ref, vs_ref, kw_ref, vw_ref, g_ref, ovl_ref, o_ref,
                *, tq, n_sel, n_cmp, scale):
    i = pl.program_id(2)
    q0 = i * tq
    R = NSA_HPG
    ncp = kc_ref.shape[1]
    Q = jnp.concatenate([q_ref[0, :, r * LANES:(r + 1) * LANES] for r in range(R)], axis=0)

    s = (_dot_nt(Q, kc_ref[0]) * scale).reshape(R, tq, ncp)
    t_c = q0 + lax.broadcasted_iota(jnp.int32, (tq, ncp), 0)
    n_c = lax.broadcasted_iota(jnp.int32, (tq, ncp), 1)
    mask_c = ((n_c * CMP_STRIDE + (CMP_BLOCK - 1) <= t_c) & (n_c < n_cmp))[None]
    s = jnp.where(mask_c, s, NEG_BIG)
    p = jnp.where(mask_c, jnp.exp(s - jnp.max(s, axis=2, keepdims=True)), 0.0)
    l = jnp.sum(p, axis=2, keepdims=True)
    p = p * jnp.where(l > 0.0, 1.0 / l, 0.0)
    o_c = _dot(p.reshape(R * tq, ncp).astype(CDT), vc_ref[0])

    psum = jnp.sum(p, axis=0)
    p_hi = psum.astype(CDT)
    p_lo = (psum - p_hi.astype(F32)).astype(CDT)
    imp = (_dot_nt(ovl_ref[...], p_hi) + _dot_nt(ovl_ref[...], p_lo))[0:n_sel]
    jj = lax.broadcasted_iota(jnp.int32, (n_sel, tq), 0)
    cur = (q0 + lax.broadcasted_iota(jnp.int32, (n_sel, tq), 1)) // SEL_BLOCK
    forced = (jj == 0) | (jj == cur) | (jj == cur - 1)
    top_n = min(SEL_TOPN, n_sel)
    assert top_n > 3
    free = jnp.where(forced | (jj > cur), -FORCE_SCORE, imp)
    tops, v = [], free
    for _ in range(top_n - 1):
        m = jnp.max(v, axis=0, keepdims=True)
        tops.append(m)
        v = jnp.where(v == m, -jnp.inf, v)
    cur_row = cur[0:1, :]
    tau = jnp.where(cur_row >= 2, tops[top_n - 4], jnp.where(cur_row == 1, tops[top_n - 3], tops[top_n - 2]))
    half = LANES // 2
    parts = [jnp.zeros((half, tq), F32), jnp.where(forced | (free >= tau), 0.0, -MASK_BIAS)]
    if n_sel < half:
        parts.append(jnp.zeros((half - n_sel, tq), F32))
    q_bias = jnp.concatenate(parts, axis=0).T.astype(CDT)
    Qb = Q + jnp.concatenate([q_bias] * R, axis=0)

    scale2 = scale * LOG2E
    hidden = -MASK_BIAS * scale2
    row = lax.broadcasted_iota(jnp.int32, (tq, tq), 0)
    col = lax.broadcasted_iota(jnp.int32, (tq, tq), 1)
    causal_bias = jnp.where(col <= row, 0.0, hidden)
    window_bias = jnp.where(col > row, 0.0, hidden)

    def step(k_ref, v_ref, kt, carry, bias):
        m, acc = carry
        k0 = pl.multiple_of(kt * tq, tq)
        s = (_dot_nt(Qb, k_ref[0, pl.ds(k0, tq), :]) * scale2).reshape(R, tq, tq)
        if bias is not None:
            s = s + bias[None]
        m_new = jnp.maximum(m, jnp.max(s, axis=2, keepdims=True))
        p = jnp.exp2(s - m_new).reshape(R * tq, tq).astype(CDT)
        pv = _dot(p, v_ref[0, pl.ds(k0, tq), :]).reshape(R, tq, LANES)
        return m_new, acc * jnp.exp2(m - m_new) + pv

    def finish(carry):
        acc = carry[1]
        return acc * (1.0 / acc[:, :, half:half + 1])

    def flag(cond):
        return jnp.where(cond, 1.0, 0.0)

    init = (jnp.full((R, tq, 1), NEG_BIG, F32), jnp.zeros((R, tq, LANES), F32))

    carry = lax.fori_loop(0, i, lambda kt, c: step(ks_ref, vs_ref, kt, c, None), init)
    o_s = finish(lax.fori_loop(i, i + 1, lambda kt, c: step(ks_ref, vs_ref, kt, c, causal_bias), carry))

    n_back = WINDOW // tq

    def win_step(it, c):
        kt = i - n_back + it
        bias = window_bias * flag(it == 0) + causal_bias * flag(it == n_back) + hidden * flag(kt < 0)
        return step(kw_ref, vw_ref, jnp.maximum(kt, 0), c, bias)

    o_w = finish(lax.fori_loop(0, n_back + 1, win_step, init))

    g = g_ref[0]
    o_c = o_c.reshape(R, tq, LANES)
    for r in range(R):
        o = (g[:, 3 * r:3 * r + 1] * o_c[r] + g[:, 3 * r + 1:3 * r + 2] * o_s[r]
             + g[:, 3 * r + 2:3 * r + 3] * o_w[r])
        o_ref[0, :, r * LANES:(r + 1) * LANES] = o.astype(o_ref.dtype)


def _nsa(q, kc, vc, ks, vs, kw, vw, gates, tq=512):
    B, S, _ = q.shape
    G = NSA_GROUPS
    n_sel = S // SEL_BLOCK
    n_cmp = (S - CMP_BLOCK) // CMP_STRIDE + 1
    ncp = kc.shape[1]
    assert n_sel <= LANES // 2 and n_sel % 8 == 0 and ncp % LANES == 0 and WINDOW % tq == 0
    jn = np.arange(LANES)[:, None] * SEL_BLOCK
    cn = np.arange(ncp)[None, :] * CMP_STRIDE
    ovl = ((cn <= jn + SEL_BLOCK - 1) & (cn + CMP_BLOCK - 1 >= jn)
           & (np.arange(LANES)[:, None] < n_sel) & (np.arange(ncp)[None, :] < n_cmp))
    ovl = jnp.asarray(ovl, CDT)
    kern = functools.partial(_nsa_kernel, tq=tq, n_sel=n_sel, n_cmp=n_cmp, scale=NSA_HEAD_DIM ** -0.5)
    cmp_spec = pl.BlockSpec((1, ncp, LANES), lambda b, g, i: (b * G + g, 0, 0))
    kv_spec = pl.BlockSpec((1, S, LANES), lambda b, g, i: (b, 0, g))
    qo_spec = pl.BlockSpec((1, tq, NSA_HPG * LANES), lambda b, g, i: (b, i, g))
    return pl.pallas_call(
        kern, grid=(B, G, S // tq),
        in_specs=[qo_spec, cmp_spec, cmp_spec, kv_spec, kv_spec, kv_spec, kv_spec,
                  pl.BlockSpec((1, tq, LANES), lambda b, g, i: (b, i, g)),
                  pl.BlockSpec(ovl.shape, lambda b, g, i: (0, 0))],
        out_specs=qo_spec,
        out_shape=jax.ShapeDtypeStruct((B, S, NSA_HEADS * LANES), CDT),
        compiler_params=_params("parallel", "parallel", "arbitrary"), name="nsa_attn")(
            q, kc, vc, ks, vs, kw, vw, gates, ovl)


def _merge_kernel(om_ref, on_ref, mg_ref, h_ref, wbm_ref, wbn_ref, wo_ref, g_ref, b_ref, x_ref, xb_ref):
    D = D_MODEL
    y_mla = _dot(om_ref[...], wbm_ref[...])
    y_nsa = _dot(on_ref[...], wbn_ref[...])
    mg = mg_ref[...].astype(F32)
    mixed = _dot((mg[:, :D] * y_mla + mg[:, D:] * y_nsa).astype(CDT), wo_ref[...])
    x = _layer_norm(DEEPNORM_ALPHA * h_ref[...] + mixed, g_ref[...], b_ref[...])
    x_ref[...] = x
    xb_ref[...] = x.astype(CDT)


def _merge(o_mla, o_nsa, mg, h, lw, tm=512):
    T, D = h.shape
    row = lambda n: pl.BlockSpec((tm, n), lambda i: (i, 0))
    full = lambda a: pl.BlockSpec(a.shape, lambda i: (0, 0))
    ws = (lw["w_bm"], lw["w_bn"], lw["w_out"], lw["ln1_g"], lw["ln1_b"])
    return pl.pallas_call(
        _merge_kernel, grid=(T // tm,),
        in_specs=[row(o_mla.shape[1]), row(o_nsa.shape[1]), row(2 * D), row(D)] + [full(a) for a in ws],
        out_specs=[row(D), row(D)],
        out_shape=[jax.ShapeDtypeStruct((T, D), F32), jax.ShapeDtypeStruct((T, D), CDT)],
        compiler_params=_params("parallel"), name="merge_ln1")(o_mla, o_nsa, mg, h, *ws)


def _rows_per_vreg():
    return 8 * (4 // jnp.dtype(CDT).itemsize)


def _pack_words(x):
    if jnp.dtype(CDT).itemsize == 4:
        return pltpu.bitcast(x, jnp.uint32)
    bits = pltpu.bitcast(x.astype(CDT).astype(F32), jnp.uint32)
    return (bits >> 16) | (bits & jnp.uint32(0xFFFF0000))


def _unpack_words(row):
    return pltpu.bitcast(jnp.broadcast_to(row, (8, LANES)), CDT)


_PEER_PAIRS = [(a, b) for a in range(PEER_TOPK) for b in range(PEER_TOPK) if (a + 1) * (b + 1) <= PEER_TOPK]
_N_CAND = -(-len(_PEER_PAIRS) // 8) * 8


def _top_values(v, want_rank):
    tops = []
    rank = jnp.full(v.shape, 127.0, F32) if want_rank else None
    for r in range(PEER_TOPK):
        m = jnp.max(v, axis=0, keepdims=True)
        tops.append(m)
        eq = v == m
        if want_rank:
            rank = jnp.where(eq, float(r), rank)
        v = jnp.where(eq, -jnp.inf, v)
    return tops, rank


def _peer_stats_kernel(xb_ref, wq_ref, keys_ref, c1_ref, e1_ref, r2_ref, e2_ref, s_ref, cand_ref):
    qp = _dot(xb_ref[...], wq_ref[...]).astype(CDT)
    s_ref[...] = _dot_nt(keys_ref[...], qp)
    n = PEER_N_KEYS
    cand_ref[...] = jnp.full(cand_ref.shape, -jnp.inf, F32)
    for h in range(PEER_HEADS):
        s1 = s_ref[(2 * h) * n:(2 * h + 1) * n, :]
        s2 = s_ref[(2 * h + 1) * n:(2 * h + 2) * n, :]
        t1, _ = _top_values(s1, False)
        t2, rank2 = _top_values(s2, True)
        for c, (a, b) in enumerate(_PEER_PAIRS):
            cand_ref[c:c + 1, :] = t1[a] + t2[b]
        cand = cand_ref[...]
        tc, _ = _top_values(cand, False)
        tau = tc[PEER_TOPK - 1]
        top = t1[0] + t2[0]
        z = jnp.sum(jnp.where(cand >= tau, jnp.exp(cand - top), 0.0), axis=0, keepdims=True)
        cnt = jnp.zeros(s1.shape, F32)
        for b in range(PEER_TOPK):
            cnt = cnt + jnp.where(s1 + t2[b] >= tau, 1.0, 0.0)
        c1_ref[h] = _pack_words(cnt)
        e1_ref[h] = _pack_words(jnp.exp(s1 - t1[0]))
        r2_ref[h] = rank2.astype(r2_ref.dtype)
        e2_ref[h] = (jnp.exp(s2 - t2[0]) / z).astype(e2_ref.dtype)


def _peer_stats(xb, wq, keys_bd_t, tm=256):
    T, D = xb.shape
    H, n = PEER_HEADS, PEER_N_KEYS
    full = lambda a: pl.BlockSpec(a.shape, lambda i: (0, 0))
    o_spec = pl.BlockSpec((H, n, tm), lambda i: (0, 0, i))
    o_u32 = jax.ShapeDtypeStruct((H, n, T), jnp.uint32)
    o_cdt = jax.ShapeDtypeStruct((H, n, T), CDT)
    return pl.pallas_call(
        _peer_stats_kernel, grid=(T // tm,),
        in_specs=[pl.BlockSpec((tm, D), lambda i: (i, 0)), full(wq), full(keys_bd_t)],
        out_specs=[o_spec] * 4, out_shape=[o_u32, o_u32, o_cdt, o_cdt],
        scratch_shapes=[pltpu.VMEM((2 * H * n, tm), F32), pltpu.VMEM((_N_CAND, tm), F32)],
        compiler_params=_params("parallel"), name="peer_stats")(xb, wq, keys_bd_t)


def _peer_dense_kernel(xb_ref, x_ref, u_ref, vt_ref, c1_ref, e1_ref, r2_in_ref, e2_in_ref, g_ref, b_ref,
                       o_ref, ob_ref, acc_ref, a_ref, h_ref, r2_ref, e2_ref, *, tm, te):
    j = pl.program_id(1)
    n = PEER_N_KEYS
    rows_per_tile = te // n

    @pl.when(j == 0)
    def _():
        acc_ref[...] = jnp.zeros(acc_ref.shape, F32)
        r2_ref[...] = r2_in_ref[...]
        e2_ref[...] = e2_in_ref[...]

    a_ref[...] = _gelu(_dot_nt(u_ref[...], xb_ref[...])).astype(CDT)
    ic = n // 2
    sub = _rows_per_vreg()
    nv = ic // sub
    zero = jnp.zeros((nv, sub, LANES), CDT)
    n_i2 = n // ic

    def gate_block(idx):
        sl = pl.ds((idx // n_i2) * LANES, LANES)
        i2 = (idx % n_i2) * ic
        ws = [zero] * rows_per_tile
        for h in range(PEER_HEADS):
            r2 = r2_ref[h, pl.ds(i2, ic), sl].reshape(nv, sub, LANES)
            e2 = e2_ref[h, pl.ds(i2, ic), sl].reshape(nv, sub, LANES)
            for r in range(rows_per_tile):
                c1 = _unpack_words(c1_ref[h, r:r + 1, sl])[None]
                e1 = _unpack_words(e1_ref[h, r:r + 1, sl])[None]
                ws[r] = ws[r] + jnp.maximum(jnp.minimum(e1 * e2, c1 - r2), zero)
        for r in range(rows_per_tile):
            rows = pl.ds(r * n + i2, ic)
            h_ref[rows, sl] = ws[r].reshape(ic, LANES) * a_ref[rows, sl]

    for idx in range((tm // LANES) * n_i2):
        gate_block(idx)
    acc_ref[...] += _dot(vt_ref[...], h_ref[...])

    @pl.when(j == pl.num_programs(1) - 1)
    def _():
        z = DEEPNORM_ALPHA * x_ref[...] + acc_ref[...].T
        out = _layer_norm(z, g_ref[...], b_ref[...])
        o_ref[...] = out
        ob_ref[...] = out.astype(CDT)


def _peer_dense(xb, x, u, vt, stats, ln_g, ln_b, tm=512, te=1024):
    T, D = x.shape
    E = u.shape[0]
    H, n = PEER_HEADS, PEER_N_KEYS
    row = pl.BlockSpec((tm, D), lambda i, j: (i, 0))
    st = pl.BlockSpec((H, n, tm), lambda i, j: (0, 0, i))
    st1 = pl.BlockSpec((H, te // n, tm), lambda i, j: (0, j, i))
    vec = pl.BlockSpec((1, D), lambda i, j: (0, 0))
    assert (te // n) % 8 == 0
    kern = functools.partial(_peer_dense_kernel, tm=tm, te=te)
    return pl.pallas_call(
        kern, grid=(T // tm, E // te),
        in_specs=[row, row, pl.BlockSpec((te, D), lambda i, j: (j, 0)), pl.BlockSpec((D, te), lambda i, j: (0, j)),
                  st1, st1, st, st, vec, vec],
        out_specs=[row, row],
        out_shape=[jax.ShapeDtypeStruct((T, D), F32), jax.ShapeDtypeStruct((T, D), CDT)],
        scratch_shapes=[pltpu.VMEM((D, tm), F32), pltpu.VMEM((te, tm), CDT), pltpu.VMEM((te, tm), CDT),
                        pltpu.VMEM((H, n, tm), CDT), pltpu.VMEM((H, n, tm), CDT)],
        compiler_params=_params("parallel", "arbitrary"), name="peer_dense")(
            xb, x, u, vt, *stats, ln_g, ln_b)


def _rot_cols(w, dim):
    k, n = w.shape
    w4 = w.reshape(k, n // dim, 2, dim // 2)
    return jnp.stack([-w4[:, :, 1], w4[:, :, 0]], axis=2).reshape(k, n)


def _pad_heads(w, dim):
    k, n = w.shape
    w3 = w.reshape(k, n // dim, dim)
    return jnp.pad(w3, ((0, 0), (0, 0), (0, LANES - dim))).reshape(k, (n // dim) * LANES)


def _pad_cols(w, n):
    return jnp.pad(w, ((0, 0), (0, n - w.shape[1])))


def _layer_weights(l, w_in, mla_q_norm, mla_w_uq, mla_kv_norm, mla_w_ukv, w_branch_mla, w_branch_nsa, w_out,
                   ln1_g, ln1_b, peer_w_query, peer_sub_keys, peer_u, peer_v, ln2_g, ln2_b):
    off = np.cumsum((0,) + IN_WIDTHS)
    wi = w_in[l]
    seg = lambda k: wi[:, off[k]:off[k + 1]]
    G, dk = NSA_GROUPS, NSA_HEAD_DIM
    kv = seg(4).reshape(D_MODEL, 6, G * dk)
    lw = {}
    lw["w_cqkv"] = jnp.concatenate([seg(0), seg(1)], axis=1).astype(CDT)
    kr = seg(2)
    lw["w_kr"] = _pad_cols(kr, LANES).astype(CDT)
    lw["w_kr_rot"] = _pad_cols(_rot_cols(kr, MLA_ROPE_DIM), LANES).astype(CDT)
    lw["w_nq"] = _pad_heads(seg(3), dk).astype(CDT)
    lw["w_nq_rot"] = _pad_heads(_rot_cols(seg(3), dk), dk).astype(CDT)
    lw["w_kcmp"] = kv[:, 0].astype(CDT)
    lw["w_kcmp_rot"] = _rot_cols(kv[:, 0], dk).astype(CDT)
    for name, idx in (("ksel", 2), ("kwin", 4)):
        lw["w_" + name] = _pad_heads(kv[:, idx], dk).astype(CDT)
        lw["w_" + name + "_rot"] = _pad_heads(_rot_cols(kv[:, idx], dk), dk).astype(CDT)
    lw["w_vall"] = jnp.concatenate([kv[:, 1], _pad_heads(kv[:, 3], dk), _pad_heads(kv[:, 5], dk)], axis=1).astype(CDT)
    ng = seg(5).reshape(D_MODEL, G, NSA_HPG * 3)
    lw["w_ngate"] = jnp.pad(ng, ((0, 0), (0, 0), (0, LANES - NSA_HPG * 3))).reshape(D_MODEL, G * LANES).astype(CDT)
    lw["w_mgate"] = seg(6).astype(CDT)

    H = MLA_HEADS
    dq = MLA_NOPE_DIM + MLA_ROPE_DIM
    uq = mla_w_uq[l].reshape(MLA_Q_RANK, H, dq)
    lw["wqn"] = _pad_heads(uq[:, :, :MLA_NOPE_DIM].reshape(MLA_Q_RANK, -1), MLA_NOPE_DIM).astype(CDT)
    wqr = uq[:, :, MLA_NOPE_DIM:].reshape(MLA_Q_RANK, H * MLA_ROPE_DIM)
    lw["wqr"] = wqr.astype(CDT)
    lw["wqr_rot"] = _rot_cols(wqr, MLA_ROPE_DIM).astype(CDT)
    pq = np.zeros((H * MLA_ROPE_DIM, H * LANES), np.float32)
    pk = np.zeros((LANES, H * LANES), np.float32)
    for h in range(H):
        for j in range(MLA_ROPE_DIM):
            pq[h * MLA_ROPE_DIM + j, h * LANES + MLA_NOPE_DIM + j] = 1.0
            pk[j, h * LANES + MLA_NOPE_DIM + j] = 1.0
    lw["place_q"] = jnp.asarray(pq, CDT)
    lw["place_k"] = jnp.asarray(pk, CDT)
    ukv = mla_w_ukv[l].reshape(MLA_KV_RANK, H, MLA_NOPE_DIM + MLA_V_DIM)
    lw["wkn"] = _pad_heads(ukv[:, :, :MLA_NOPE_DIM].reshape(MLA_KV_RANK, -1), MLA_NOPE_DIM).astype(CDT)
    lw["wv"] = ukv[:, :, MLA_NOPE_DIM:].reshape(MLA_KV_RANK, H * MLA_V_DIM).astype(CDT)
    lw["q_gain"] = mla_q_norm[l].reshape(1, -1)
    lw["kv_gain"] = mla_kv_norm[l].reshape(1, -1)

    lw["w_bm"] = w_branch_mla[l].astype(CDT)
    wbn = w_branch_nsa[l].reshape(NSA_HEADS, dk, D_MODEL)
    lw["w_bn"] = jnp.pad(wbn, ((0, 0), (0, LANES - dk), (0, 0))).reshape(NSA_HEADS * LANES, D_MODEL).astype(CDT)
    lw["w_out"] = w_out[l].astype(CDT)
    lw["ln1_g"] = ln1_g[l].reshape(1, -1)
    lw["ln1_b"] = ln1_b[l].reshape(1, -1)

    lw["w_pq"] = peer_w_query[l].astype(CDT)
    sk = peer_sub_keys[l].reshape(PEER_HEADS * 2, PEER_N_KEYS, PEER_KEY_DIM // 2)
    eye = jnp.eye(PEER_HEADS * 2, dtype=sk.dtype)
    lw["keys_bd_t"] = jnp.einsum("gnd,gf->gnfd", sk, eye).reshape(
        PEER_HEADS * 2 * PEER_N_KEYS, PEER_HEADS * PEER_KEY_DIM).astype(CDT)
    lw["u"] = peer_u[l].astype(CDT)
    lw["vt"] = peer_v[l].T.astype(CDT)
    lw["ln2_g"] = ln2_g[l].reshape(1, -1)
    lw["ln2_b"] = ln2_b[l].reshape(1, -1)
    return lw


def _rope_tables(positions, dim):
    inv_freq = ROPE_THETA ** (-jnp.arange(0, dim, 2, dtype=F32) / dim)
    ang = positions.astype(F32).reshape(-1, 1) * inv_freq
    rep = lambda t: jnp.tile(jnp.concatenate([t, t], axis=-1), (1, LANES // dim))
    return rep(jnp.cos(ang)), rep(jnp.sin(ang))


def _hybrid_layer(h, hb, B, S, lw, cmp_w, tab_mla, tab_nsa):
    T = B * S
    G, dk = NSA_GROUPS, NSA_HEAD_DIM
    cqkv = _proj(hb, lw["w_cqkv"], CDT, name="proj_cqkv")
    kr = _proj(hb, lw["w_kr"], CDT, lw["w_kr_rot"], tab_mla, name="proj_kr")
    nq = _proj(hb, lw["w_nq"], CDT, lw["w_nq_rot"], tab_nsa, name="proj_nq")
    kcmp = _proj(hb, lw["w_kcmp"], CDT, lw["w_kcmp_rot"], tab_nsa, name="proj_kcmp")
    lane = jnp.arange(G * LANES) % LANES - LANES // 2
    blk_onehot = (jnp.arange(S)[:, None] // SEL_BLOCK == lane[None, :]).astype(F32)
    ksel = _proj(hb, lw["w_ksel"], CDT, lw["w_ksel_rot"], tab_nsa, bias=jnp.tile(blk_onehot, (B, 1)),
                 name="proj_ksel")
    kwin = _proj(hb, lw["w_kwin"], CDT, lw["w_kwin_rot"], tab_nsa, name="proj_kwin")
    ones_lane = np.zeros((1, G * dk + 2 * G * LANES), np.float32)
    ones_lane[0, [G * dk + g * LANES + dk for g in range(2 * G)]] = 1.0
    vall = _proj(hb, lw["w_vall"], CDT, bias=jnp.asarray(ones_lane), name="proj_v")
    ngate = _proj(hb, lw["w_ngate"], F32, act="sigmoid", name="proj_ngate")
    mgate = _proj(hb, lw["w_mgate"], CDT, act="sigmoid", name="proj_mgate")

    q, k, v = _mla_up(cqkv, kr, lw, tab_mla)
    o_mla = _mla_attn(q.reshape(B, S, -1), k.reshape(B, S, -1), v.reshape(B, S, -1))

    def chunks(a):
        return a.reshape(B, S, G, dk).transpose(0, 2, 1, 3).reshape(B * G, S // CMP_STRIDE, CMP_STRIDE * dk)

    kc = _compress(chunks(kcmp), cmp_w["k_pe"], cmp_w["k_w1"], cmp_w["k_w2"])
    vc = _compress(chunks(vall[:, :G * dk]), cmp_w["v_pe"], cmp_w["v_w1"], cmp_w["v_w2"])
    vsel = vall[:, G * dk:G * dk + G * LANES]
    vwin = vall[:, G * dk + G * LANES:]
    r3 = lambda a: a.reshape(B, S, -1)
    o_nsa = _nsa(r3(nq), kc, vc, r3(ksel), r3(vsel), r3(kwin), r3(vwin), r3(ngate))

    x1, x1b = _merge(o_mla.reshape(T, -1), o_nsa.reshape(T, -1), mgate, h, lw)
    stats = _peer_stats(x1b, lw["w_pq"], lw["keys_bd_t"])
    return _peer_dense(x1b, x1, lw["u"], lw["vt"], stats, lw["ln2_g"], lw["ln2_b"])


def kernel(x, positions, ln_in_g, ln_in_b, w_in, mla_q_norm, mla_w_uq, mla_kv_norm, mla_w_ukv, nsa_cmp_k_pe, nsa_cmp_k_w1, nsa_cmp_k_w2, nsa_cmp_v_pe, nsa_cmp_v_w1, nsa_cmp_v_w2, w_branch_mla, w_branch_nsa, w_out, ln1_g, ln1_b, peer_w_query, peer_sub_keys, peer_u, peer_v, ln2_g, ln2_b):
    B, S, D = x.shape
    tab_mla = _rope_tables(positions, MLA_ROPE_DIM)
    tab_nsa = _rope_tables(positions, NSA_HEAD_DIM)
    h, hb = _ln_in(x.reshape(B * S, D), ln_in_g, ln_in_b)
    for l in range(DEPTH):
        lw = _layer_weights(l, w_in, mla_q_norm, mla_w_uq, mla_kv_norm, mla_w_ukv, w_branch_mla, w_branch_nsa,
                            w_out, ln1_g, ln1_b, peer_w_query, peer_sub_keys, peer_u, peer_v, ln2_g, ln2_b)
        cmp_w = {"k_pe": nsa_cmp_k_pe[l], "k_w1": nsa_cmp_k_w1[l], "k_w2": nsa_cmp_k_w2[l],
                 "v_pe": nsa_cmp_v_pe[l], "v_w1": nsa_cmp_v_w1[l], "v_w2": nsa_cmp_v_w2[l]}
        h, hb = _hybrid_layer(h, hb, B, S, lw, cmp_w, tab_mla, tab_nsa)
    return h.reshape(B, S, D)
```

```python
import functools
import math

import numpy as np
import jax
import jax.numpy as jnp
from jax import lax
from jax.experimental import pallas as pl
from jax.experimental.pallas import tpu as pltpu

D_MODEL = 1024
DEPTH = 2
ROPE_THETA = 10000.0

MLA_HEADS = 8
MLA_NOPE_DIM = 64
MLA_ROPE_DIM = 32
MLA_V_DIM = 64
MLA_Q_RANK = 768
MLA_KV_RANK = 256

NSA_HEADS = 8
NSA_GROUPS = 2
NSA_HPG = NSA_HEADS // NSA_GROUPS
NSA_HEAD_DIM = 64
CMP_BLOCK = 32
CMP_STRIDE = 16
CMP_HIDDEN = 2 * NSA_HEAD_DIM
SEL_BLOCK = 64
SEL_TOPN = 16
WINDOW = 512
FORCE_SCORE = 1e9

PEER_HEADS = 8
PEER_KEY_DIM = 128
PEER_N_KEYS = 128
PEER_N_EXPERTS = PEER_N_KEYS * PEER_N_KEYS
PEER_TOPK = 16

DEEPNORM_ALPHA = (2 * DEPTH) ** 0.25
LN_EPS = 1e-5
RMS_EPS = 1e-6
NEG_BIG = -1e30
LOG2E = math.log2(math.e)
MASK_BIAS = 32768.0

IN_WIDTHS = (MLA_Q_RANK, MLA_KV_RANK, MLA_ROPE_DIM, NSA_HEADS * NSA_HEAD_DIM,
             6 * NSA_GROUPS * NSA_HEAD_DIM, 3 * NSA_HEADS, 2 * D_MODEL)

LANES = 128
VMEM_LIMIT = 56 * 1024 * 1024

F32 = jnp.float32
CDT = jnp.bfloat16

_NT = (((1,), (1,)), ((), ()))


def _dot(a, b):
    return jnp.dot(a, b, preferred_element_type=F32)


def _dot_nt(a, b):
    return lax.dot_general(a, b, _NT, preferred_element_type=F32)


def _params(*sem):
    return pltpu.CompilerParams(dimension_semantics=sem, vmem_limit_bytes=VMEM_LIMIT)


def _gelu(x):
    return 0.5 * x * (1.0 + lax.erf(x * (2.0 ** -0.5)))


def _layer_norm(z, g, b):
    mu = jnp.mean(z, axis=-1, keepdims=True)
    d = z - mu
    var = jnp.mean(d * d, axis=-1, keepdims=True)
    return d * lax.rsqrt(var + LN_EPS) * g + b


def _ln_in_kernel(x_ref, g_ref, b_ref, h_ref, hb_ref):
    h = _layer_norm(x_ref[...], g_ref[...], b_ref[...])
    h_ref[...] = h
    hb_ref[...] = h.astype(CDT)


def _ln_in(x, g, b, tm=512):
    T, D = x.shape
    row = pl.BlockSpec((tm, D), lambda i: (i, 0))
    vec = pl.BlockSpec((1, D), lambda i: (0, 0))
    return pl.pallas_call(
        _ln_in_kernel, grid=(T // tm,), in_specs=[row, vec, vec], out_specs=[row, row],
        out_shape=[jax.ShapeDtypeStruct((T, D), F32), jax.ShapeDtypeStruct((T, D), CDT)],
        compiler_params=_params("parallel"), name="ln_in")(x, g.reshape(1, D), b.reshape(1, D))


def _proj_kernel(*refs, rope, act, n_rep, has_bias):
    refs = list(refs)
    o_ref = refs.pop()
    bias_ref = refs.pop() if has_bias else None
    if rope:
        hb_ref, w_ref, wr_ref, cos_ref, sin_ref = refs
    else:
        hb_ref, w_ref = refs
    hb = hb_ref[...]
    y = _dot(hb, w_ref[...])
    if rope:
        yr = _dot(hb, wr_ref[...])
        c, s = cos_ref[...], sin_ref[...]
        if n_rep > 1:
            c = jnp.concatenate([c] * n_rep, axis=1)
            s = jnp.concatenate([s] * n_rep, axis=1)
        y = y * c + yr * s
    if act == "sigmoid":
        y = 1.0 / (1.0 + jnp.exp(-y))
    if has_bias:
        y = y + bias_ref[...]
    o_ref[...] = y.astype(o_ref.dtype)


def _proj(hb, w, out_dtype, w_rot=None, tables=None, act=None, bias=None, tm=1024, name="proj"):
    T, K = hb.shape
    N = w.shape[1]
    tn = N if N <= 768 else 512
    assert N % tn == 0 and tn % LANES == 0 and T % tm == 0
    rope = w_rot is not None
    x_spec = pl.BlockSpec((tm, K), lambda j, i: (i, 0))
    w_spec = pl.BlockSpec((K, tn), lambda j, i: (0, j))
    t_spec = pl.BlockSpec((tm, LANES), lambda j, i: (i, 0))
    o_spec = pl.BlockSpec((tm, tn), lambda j, i: (i, j))
    if rope:
        args = (hb, w, w_rot, tables[0], tables[1])
        in_specs = [x_spec, w_spec, w_spec, t_spec, t_spec]
    else:
        args = (hb, w)
        in_specs = [x_spec, w_spec]
    if bias is not None:
        args += (bias,)
        per_row = bias.shape[0] != 1
        in_specs.append(pl.BlockSpec((tm if per_row else 1, tn), lambda j, i: (i if per_row else 0, j)))
    kern = functools.partial(_proj_kernel, rope=rope, act=act, n_rep=tn // LANES, has_bias=bias is not None)
    return pl.pallas_call(
        kern, grid=(N // tn, T // tm), in_specs=in_specs, out_specs=o_spec,
        out_shape=jax.ShapeDtypeStruct((T, N), out_dtype),
        compiler_params=_params("parallel", "parallel"), name=name)(*args)


def _rms(x, g):
    return x * lax.rsqrt(jnp.mean(x * x, axis=-1, keepdims=True) + RMS_EPS) * g


def _mla_up_kernel(c_ref, kr_ref, qg_ref, kvg_ref, wqn_ref, wqr_ref, wqrr_ref, pq_ref,
                   wkn_ref, wv_ref, pk_ref, cos_ref, sin_ref, q_ref, k_ref, v_ref):
    c = c_ref[...].astype(F32)
    cqn = _rms(c[:, :MLA_Q_RANK], qg_ref[...]).astype(CDT)
    ckvn = _rms(c[:, MLA_Q_RANK:], kvg_ref[...]).astype(CDT)
    cos = jnp.concatenate([cos_ref[...]] * 2, axis=1)
    sin = jnp.concatenate([sin_ref[...]] * 2, axis=1)
    roped = (_dot(cqn, wqr_ref[...]) * cos + _dot(cqn, wqrr_ref[...]) * sin).astype(CDT)
    q_ref[...] = (_dot(cqn, wqn_ref[...]) + _dot(roped, pq_ref[...])).astype(CDT)
    k_ref[...] = (_dot(ckvn, wkn_ref[...]) + _dot(kr_ref[...], pk_ref[...])).astype(CDT)
    v_ref[...] = _dot(ckvn, wv_ref[...]).astype(CDT)


def _mla_up(cqkv, kr, lw, tables, tm=512):
    T = cqkv.shape[0]
    row = lambda n: pl.BlockSpec((tm, n), lambda i: (i, 0))
    full = lambda a: pl.BlockSpec(a.shape, lambda i: (0, 0))
    ws = (lw["q_gain"], lw["kv_gain"], lw["wqn"], lw["wqr"], lw["wqr_rot"], lw["place_q"],
          lw["wkn"], lw["wv"], lw["place_k"])
    hp = MLA_HEADS * LANES
    return pl.pallas_call(
        _mla_up_kernel, grid=(T // tm,),
        in_specs=[row(cqkv.shape[1]), row(LANES)] + [full(a) for a in ws] + [row(LANES), row(LANES)],
        out_specs=[row(hp), row(hp), row(MLA_HEADS * MLA_V_DIM)],
        out_shape=[jax.ShapeDtypeStruct((T, hp), CDT), jax.ShapeDtypeStruct((T, hp), CDT),
                   jax.ShapeDtypeStruct((T, MLA_HEADS * MLA_V_DIM), CDT)],
        compiler_params=_params("parallel"), name="mla_up")(cqkv, kr, *ws, tables[0], tables[1])


def _mla_attn_kernel(q_ref, k_ref, v_ref, o_ref, *, tq, scale):
    i = pl.program_id(2)
    outs = []
    for hd in range(2):
        q = q_ref[0, :, hd * LANES:(hd + 1) * LANES]

        def step(kt, carry, diagonal, hd=hd, q=q):
            m, l, acc = carry
            k0 = pl.multiple_of(kt * tq, tq)
            k = k_ref[0, pl.ds(k0, tq), hd * LANES:(hd + 1) * LANES]
            v = v_ref[0, pl.ds(k0, tq), :]
            s = _dot_nt(q, k) * scale
            if diagonal:
                row = lax.broadcasted_iota(jnp.int32, (tq, tq), 0)
                col = lax.broadcasted_iota(jnp.int32, (tq, tq), 1)
                s = jnp.where(col <= row, s, NEG_BIG)
            m_new = jnp.maximum(m, jnp.max(s, axis=1, keepdims=True))
            alpha = jnp.exp(m - m_new)
            p = jnp.exp(s - m_new)
            l = l * alpha + jnp.sum(p, axis=1, keepdims=True)
            acc = acc * alpha + _dot(p.astype(CDT), v)
            return m_new, l, acc

        init = (jnp.full((tq, 1), NEG_BIG, F32), jnp.zeros((tq, 1), F32), jnp.zeros((tq, LANES), F32))
        carry = lax.fori_loop(0, i, lambda kt, c: step(kt, c, False), init)
        _, l, acc = step(i, carry, True)
        outs.append(acc / l)
    lane = lax.broadcasted_iota(jnp.int32, (tq, LANES), 1)
    o_ref[0] = jnp.where(lane < MLA_V_DIM, outs[0], outs[1]).astype(o_ref.dtype)


def _mla_attn(q, k, v, tq=512):
    B, S, _ = q.shape
    scale = (MLA_NOPE_DIM + MLA_ROPE_DIM) ** -0.5
    kern = functools.partial(_mla_attn_kernel, tq=tq, scale=scale)
    return pl.pallas_call(
        kern, grid=(B, MLA_HEADS // 2, S // tq),
        in_specs=[pl.BlockSpec((1, tq, 2 * LANES), lambda b, h, i: (b, i, h)),
                  pl.BlockSpec((1, S, 2 * LANES), lambda b, h, i: (b, 0, h)),
                  pl.BlockSpec((1, S, LANES), lambda b, h, i: (b, 0, h))],
        out_specs=pl.BlockSpec((1, tq, LANES), lambda b, h, i: (b, i, h)),
        out_shape=jax.ShapeDtypeStruct((B, S, MLA_HEADS * MLA_V_DIM), CDT),
        compiler_params=_params("parallel", "parallel", "arbitrary"), name="mla_attn")(q, k, v)


def _compress_kernel(c_ref, pe_ref, w1a_ref, w1b_ref, w2_ref, o_ref, *, n_valid):
    c = c_ref[0].astype(F32)
    n = c.shape[0]
    a = _dot((c + pe_ref[0:1, :]).astype(CDT), w1a_ref[...])
    b = _dot((c + pe_ref[1:2, :]).astype(CDT), w1b_ref[...])
    hid = _gelu(a + pltpu.roll(b, n - 1, 0))
    out = _dot(hid.astype(CDT), w2_ref[...])
    row = lax.broadcasted_iota(jnp.int32, out.shape, 0)
    o_ref[0] = jnp.where(row < n_valid, out, 0.0).astype(o_ref.dtype)


def _compress(chunks, pe, w1, w2):
    BG, n, width = chunks.shape
    pe2 = pe.reshape(2, width).astype(F32)
    w1a = w1[:width].astype(CDT)
    w1b = w1[width:].astype(CDT)
    w2p = jnp.pad(w2, ((0, 0), (0, LANES - w2.shape[1]))).astype(CDT)
    full = lambda a: pl.BlockSpec(a.shape, lambda i: (0, 0))
    return pl.pallas_call(
        functools.partial(_compress_kernel, n_valid=n - 1), grid=(BG,),
        in_specs=[pl.BlockSpec((1, n, width), lambda i: (i, 0, 0)), full(pe2), full(w1a), full(w1b), full(w2p)],
        out_specs=pl.BlockSpec((1, n, LANES), lambda i: (i, 0, 0)),
        out_shape=jax.ShapeDtypeStruct((BG, n, LANES), CDT),
        compiler_params=_params("parallel"), name="nsa_compress")(chunks, pe2, w1a, w1b, w2p)


def _nsa_kernel(q_ref, kc_ref, vc_ref, ks_ref, vs_ref, kw_ref, vw_ref, g_ref, ovl_ref, edge_ref, o_ref,
                *, tq, n_sel, n_cmp, scale):
    i = pl.program_id(2)
    q0 = i * tq
    R = NSA_HPG
    ncp = kc_ref.shape[1]
    Q = jnp.concatenate([q_ref[0, :, r * LANES:(r + 1) * LANES] for r in range(R)], axis=0)

    scale2 = scale * LOG2E
    hidden = -MASK_BIAS * scale2
    t_c = q0 + lax.broadcasted_iota(jnp.int32, (tq, ncp), 0)
    n_c = lax.broadcasted_iota(jnp.int32, (tq, ncp), 1)
    bias_c = jnp.where((n_c * CMP_STRIDE + (CMP_BLOCK - 1) <= t_c) & (n_c < n_cmp), 0.0, hidden)
    s = (_dot_nt(Q, kc_ref[0]) * scale2).reshape(R, tq, ncp) + bias_c[None]
    p = jnp.exp2(s - jnp.max(s, axis=2, keepdims=True))
    t_q = q0 + lax.broadcasted_iota(jnp.int32, (tq, 1), 0)
    sees_any = jnp.where(t_q >= CMP_BLOCK - 1, 1.0, 0.0)
    p = p * (sees_any / jnp.sum(p, axis=2, keepdims=True))
    o_c = _dot(p.reshape(R * tq, ncp).astype(CDT), vc_ref[0])

    psum = jnp.sum(p, axis=0)
    p_hi = psum.astype(CDT)
    p_lo = (psum - p_hi.astype(F32)).astype(CDT)
    imp = (_dot_nt(ovl_ref[...], p_hi) + _dot_nt(ovl_ref[...], p_lo))[0:n_sel]
    jj = lax.broadcasted_iota(jnp.int32, (n_sel, tq), 0)
    cur = (q0 + lax.broadcasted_iota(jnp.int32, (n_sel, tq), 1)) // SEL_BLOCK
    forced = (jj == 0) | (jj == cur) | (jj == cur - 1)
    top_n = min(SEL_TOPN, n_sel)
    assert top_n > 3
    free = jnp.where(forced | (jj > cur), -FORCE_SCORE, imp)
    tops, v = [], free
    for _ in range(top_n - 1):
        m = jnp.max(v, axis=0, keepdims=True)
        tops.append(m)
        v = jnp.where(v == m, -jnp.inf, v)
    cur_row = cur[0:1, :]
    tau = jnp.where(cur_row >= 2, tops[top_n - 4], jnp.where(cur_row == 1, tops[top_n - 3], tops[top_n - 2]))
    half = LANES // 2
    parts = [jnp.zeros((half, tq), F32), jnp.where(forced | (free >= tau), 0.0, -MASK_BIAS)]
    if n_sel < half:
        parts.append(jnp.zeros((half - n_sel, tq), F32))
    q_bias = jnp.concatenate(parts, axis=0).T.astype(CDT)
    Qb = Q + jnp.concatenate([q_bias] * R, axis=0)

    causal_bias = edge_ref[0]
    window_bias = edge_ref[1]

    def step(k_ref, v_ref, kt, carry, bias):
        m, acc = carry
        k0 = pl.multiple_of(kt * tq, tq)
        s = (_dot_nt(Qb, k_ref[0, pl.ds(k0, tq), :]) * scale2).reshape(R, tq, tq)
        if bias is not None:
            s = s + bias[None]
        m_new = jnp.maximum(m, jnp.max(s, axis=2, keepdims=True))
        p = jnp.exp2(s - m_new).reshape(R * tq, tq).astype(CDT)
        pv = _dot(p, v_ref[0, pl.ds(k0, tq), :]).reshape(R, tq, LANES)
        return m_new, acc * jnp.exp2(m - m_new) + pv

    def finish(carry):
        acc = carry[1]
        return acc * (1.0 / acc[:, :, half:half + 1])

    def flag(cond):
        return jnp.where(cond, 1.0, 0.0)

    init = (jnp.full((R, tq, 1), NEG_BIG, F32), jnp.zeros((R, tq, LANES), F32))

    carry = lax.fori_loop(0, i, lambda kt, c: step(ks_ref, vs_ref, kt, c, None), init)
    o_s = finish(lax.fori_loop(i, i + 1, lambda kt, c: step(ks_ref, vs_ref, kt, c, causal_bias), carry))

    n_back = WINDOW // tq

    def win_step(it, c):
        kt = i - n_back + it
        bias = window_bias * flag(it == 0) + causal_bias * flag(it == n_back) + hidden * flag(kt < 0)
        return step(kw_ref, vw_ref, jnp.maximum(kt, 0), c, bias)

    o_w = finish(lax.fori_loop(0, n_back + 1, win_step, init))

    g = g_ref[0]
    o_c = o_c.reshape(R, tq, LANES)
    for r in range(R):
        o = (g[:, 3 * r:3 * r + 1] * o_c[r] + g[:, 3 * r + 1:3 * r + 2] * o_s[r]
             + g[:, 3 * r + 2:3 * r + 3] * o_w[r])
        o_ref[0, :, r * LANES:(r + 1) * LANES] = o.astype(o_ref.dtype)


def _nsa(q, kc, vc, ks, vs, kw, vw, gates, tq=512):
    B, S, _ = q.shape
    G = NSA_GROUPS
    n_sel = S // SEL_BLOCK
    n_cmp = (S - CMP_BLOCK) // CMP_STRIDE + 1
    ncp = kc.shape[1]
    assert n_sel <= LANES // 2 and n_sel % 8 == 0 and ncp % LANES == 0 and WINDOW % tq == 0
    jn = np.arange(LANES)[:, None] * SEL_BLOCK
    cn = np.arange(ncp)[None, :] * CMP_STRIDE
    ovl = ((cn <= jn + SEL_BLOCK - 1) & (cn + CMP_BLOCK - 1 >= jn)
           & (np.arange(LANES)[:, None] < n_sel) & (np.arange(ncp)[None, :] < n_cmp))
    ovl = jnp.asarray(ovl, CDT)
    scale = NSA_HEAD_DIM ** -0.5
    hidden = -MASK_BIAS * scale * LOG2E
    row, col = np.arange(tq)[:, None], np.arange(tq)[None, :]
    edge = jnp.asarray(np.stack([np.where(col <= row, 0.0, hidden), np.where(col > row, 0.0, hidden)]), F32)
    kern = functools.partial(_nsa_kernel, tq=tq, n_sel=n_sel, n_cmp=n_cmp, scale=scale)
    cmp_spec = pl.BlockSpec((1, ncp, LANES), lambda b, g, i: (b * G + g, 0, 0))
    kv_spec = pl.BlockSpec((1, S, LANES), lambda b, g, i: (b, 0, g))
    qo_spec = pl.BlockSpec((1, tq, NSA_HPG * LANES), lambda b, g, i: (b, i, g))
    return pl.pallas_call(
        kern, grid=(B, G, S // tq),
        in_specs=[qo_spec, cmp_spec, cmp_spec, kv_spec, kv_spec, kv_spec, kv_spec,
                  pl.BlockSpec((1, tq, LANES), lambda b, g, i: (b, i, g)),
                  pl.BlockSpec(ovl.shape, lambda b, g, i: (0, 0)),
                  pl.BlockSpec(edge.shape, lambda b, g, i: (0, 0, 0))],
        out_specs=qo_spec,
        out_shape=jax.ShapeDtypeStruct((B, S, NSA_HEADS * LANES), CDT),
        compiler_params=_params("parallel", "parallel", "arbitrary"), name="nsa_attn")(
            q, kc, vc, ks, vs, kw, vw, gates, ovl, edge)


def _merge_kernel(om_ref, on_ref, mg_ref, h_ref, wbm_ref, wbn_ref, wo_ref, g_ref, b_ref, x_ref, xb_ref):
    D = D_MODEL
    y_mla = _dot(om_ref[...], wbm_ref[...])
    y_nsa = _dot(on_ref[...], wbn_ref[...])
    mg = mg_ref[...].astype(F32)
    mixed = _dot((mg[:, :D] * y_mla + mg[:, D:] * y_nsa).astype(CDT), wo_ref[...])
    x = _layer_norm(DEEPNORM_ALPHA * h_ref[...] + mixed, g_ref[...], b_ref[...])
    x_ref[...] = x
    xb_ref[...] = x.astype(CDT)


def _merge(o_mla, o_nsa, mg, h, lw, tm=512):
    T, D = h.shape
    row = lambda n: pl.BlockSpec((tm, n), lambda i: (i, 0))
    full = lambda a: pl.BlockSpec(a.shape, lambda i: (0, 0))
    ws = (lw["w_bm"], lw["w_bn"], lw["w_out"], lw["ln1_g"], lw["ln1_b"])
    return pl.pallas_call(
        _merge_kernel, grid=(T // tm,),
        in_specs=[row(o_mla.shape[1]), row(o_nsa.shape[1]), row(2 * D), row(D)] + [full(a) for a in ws],
        out_specs=[row(D), row(D)],
        out_shape=[jax.ShapeDtypeStruct((T, D), F32), jax.ShapeDtypeStruct((T, D), CDT)],
        compiler_params=_params("parallel"), name="merge_ln1")(o_mla, o_nsa, mg, h, *ws)


def _rows_per_vreg():
    return 8 * (4 // jnp.dtype(CDT).itemsize)


def _pack_words(x):
    if jnp.dtype(CDT).itemsize == 4:
        return pltpu.bitcast(x, jnp.uint32)
    bits = pltpu.bitcast(x.astype(CDT).astype(F32), jnp.uint32)
    return (bits >> 16) | (bits & jnp.uint32(0xFFFF0000))


def _unpack_words(row):
    return pltpu.bitcast(jnp.broadcast_to(row, (8, LANES)), CDT)


_PEER_PAIRS = [(a, b) for a in range(PEER_TOPK) for b in range(PEER_TOPK) if (a + 1) * (b + 1) <= PEER_TOPK]
_N_CAND = -(-len(_PEER_PAIRS) // 8) * 8
HEADS_PER_TRIP = 2


def _top_values(v, want_rank):
    tops = []
    rank = jnp.full(v.shape, 127.0, F32) if want_rank else None
    for r in range(PEER_TOPK):
        m = jnp.max(v, axis=0, keepdims=True)
        tops.append(m)
        eq = v == m
        if want_rank:
            rank = jnp.where(eq, float(r), rank)
        v = jnp.where(eq, -jnp.inf, v)
    return tops, rank


def _peer_stats_kernel(xb_ref, wq_ref, keys_ref, c1_ref, e1_ref, r2_ref, e2_ref, s_ref, cand_ref):
    qp = _dot(xb_ref[...], wq_ref[...]).astype(CDT)
    s_ref[...] = _dot_nt(keys_ref[...], qp)
    n = PEER_N_KEYS
    cand_ref[...] = jnp.full(cand_ref.shape, -jnp.inf, F32)

    def one_head(h, cand_ref):
        s1 = s_ref[pl.ds(pl.multiple_of(2 * h * n, n), n), :]
        s2 = s_ref[pl.ds(pl.multiple_of((2 * h + 1) * n, n), n), :]
        t1, _ = _top_values(s1, False)
        t2, rank2 = _top_values(s2, True)
        for c, (a, b) in enumerate(_PEER_PAIRS):
            cand_ref[c:c + 1, :] = t1[a] + t2[b]
        cand = cand_ref[...]
        tc, _ = _top_values(cand, False)
        tau = tc[PEER_TOPK - 1]
        top = t1[0] + t2[0]
        z = jnp.sum(jnp.where(cand >= tau, jnp.exp(cand - top), 0.0), axis=0, keepdims=True)
        cnt = jnp.zeros(s1.shape, F32)
        for b in range(PEER_TOPK):
            cnt = cnt + jnp.where(s1 + t2[b] >= tau, 1.0, 0.0)
        c1_ref[h] = _pack_words(cnt)
        e1_ref[h] = _pack_words(jnp.exp(s1 - t1[0]))
        r2_ref[h] = rank2.astype(r2_ref.dtype)
        e2_ref[h] = (jnp.exp(s2 - t2[0]) / z).astype(e2_ref.dtype)

    def heads(idx, carry):
        for k in range(HEADS_PER_TRIP):
            one_head(idx * HEADS_PER_TRIP + k, cand_ref.at[k])
        return carry

    lax.fori_loop(0, PEER_HEADS // HEADS_PER_TRIP, heads, 0)


def _peer_stats(xb, wq, keys_bd_t, tm=256):
    T, D = xb.shape
    H, n = PEER_HEADS, PEER_N_KEYS
    full = lambda a: pl.BlockSpec(a.shape, lambda i: (0, 0))
    o_spec = pl.BlockSpec((H, n, tm), lambda i: (0, 0, i))
    o_u32 = jax.ShapeDtypeStruct((H, n, T), jnp.uint32)
    o_cdt = jax.ShapeDtypeStruct((H, n, T), CDT)
    return pl.pallas_call(
        _peer_stats_kernel, grid=(T // tm,),
        in_specs=[pl.BlockSpec((tm, D), lambda i: (i, 0)), full(wq), full(keys_bd_t)],
        out_specs=[o_spec] * 4, out_shape=[o_u32, o_u32, o_cdt, o_cdt],
        scratch_shapes=[pltpu.VMEM((2 * H * n, tm), F32), pltpu.VMEM((HEADS_PER_TRIP, _N_CAND, tm), F32)],
        compiler_params=_params("parallel"), name="peer_stats")(xb, wq, keys_bd_t)


def _peer_dense_kernel(xb_ref, x_ref, u_ref, vt_ref, c1_ref, e1_ref, r2_in_ref, e2_in_ref, g_ref, b_ref,
                       o_ref, ob_ref, acc_ref, a_ref, h_ref, r2_ref, e2_ref, *, tm, te):
    j = pl.program_id(1)
    n = PEER_N_KEYS
    rows_per_tile = te // n

    @pl.when(j == 0)
    def _():
        acc_ref[...] = jnp.zeros(acc_ref.shape, F32)
        r2_ref[...] = r2_in_ref[...]
        e2_ref[...] = e2_in_ref[...]

    a_ref[...] = _gelu(_dot_nt(u_ref[...], xb_ref[...])).astype(CDT)
    ic = n // 2
    sub = _rows_per_vreg()
    nv = ic // sub
    zero = jnp.zeros((nv, sub, LANES), CDT)
    n_i2 = n // ic

    def gate_block(idx):
        sl = pl.ds((idx // n_i2) * LANES, LANES)
        i2 = (idx % n_i2) * ic
        ws = [zero] * rows_per_tile
        for h in range(PEER_HEADS):
            r2 = r2_ref[h, pl.ds(i2, ic), sl].reshape(nv, sub, LANES)
            e2 = e2_ref[h, pl.ds(i2, ic), sl].reshape(nv, sub, LANES)
            for r in range(rows_per_tile):
                c1 = _unpack_words(c1_ref[h, r:r + 1, sl])[None]
                e1 = _unpack_words(e1_ref[h, r:r + 1, sl])[None]
                ws[r] = ws[r] + jnp.maximum(jnp.minimum(e1 * e2, c1 - r2), zero)
        for r in range(rows_per_tile):
            rows = pl.ds(r * n + i2, ic)
            h_ref[rows, sl] = ws[r].reshape(ic, LANES) * a_ref[rows, sl]

    for idx in range((tm // LANES) * n_i2):
        gate_block(idx)
    acc_ref[...] += _dot(vt_ref[...], h_ref[...])

    @pl.when(j == pl.num_programs(1) - 1)
    def _():
        z = DEEPNORM_ALPHA * x_ref[...] + acc_ref[...].T
        out = _layer_norm(z, g_ref[...], b_ref[...])
        o_ref[...] = out
        ob_ref[...] = out.astype(CDT)


def _peer_dense(xb, x, u, vt, stats, ln_g, ln_b, tm=512, te=1024):
    T, D = x.shape
    E = u.shape[0]
    H, n = PEER_HEADS, PEER_N_KEYS
    row = pl.BlockSpec((tm, D), lambda i, j: (i, 0))
    st = pl.BlockSpec((H, n, tm), lambda i, j: (0, 0, i))
    st1 = pl.BlockSpec((H, te // n, tm), lambda i, j: (0, j, i))
    vec = pl.BlockSpec((1, D), lambda i, j: (0, 0))
    assert (te // n) % 8 == 0
    kern = functools.partial(_peer_dense_kernel, tm=tm, te=te)
    return pl.pallas_call(
        kern, grid=(T // tm, E // te),
        in_specs=[row, row, pl.BlockSpec((te, D), lambda i, j: (j, 0)), pl.BlockSpec((D, te), lambda i, j: (0, j)),
                  st1, st1, st, st, vec, vec],
        out_specs=[row, row],
        out_shape=[jax.ShapeDtypeStruct((T, D), F32), jax.ShapeDtypeStruct((T, D), CDT)],
        scratch_shapes=[pltpu.VMEM((D, tm), F32), pltpu.VMEM((te, tm), CDT), pltpu.VMEM((te, tm), CDT),
                        pltpu.VMEM((H, n, tm), CDT), pltpu.VMEM((H, n, tm), CDT)],
        compiler_params=_params("parallel", "arbitrary"), name="peer_dense")(
            xb, x, u, vt, *stats, ln_g, ln_b)


def _rot_cols(w, dim):
    k, n = w.shape
    w4 = w.reshape(k, n // dim, 2, dim // 2)
    return jnp.stack([-w4[:, :, 1], w4[:, :, 0]], axis=2).reshape(k, n)


def _pad_heads(w, dim):
    k, n = w.shape
    w3 = w.reshape(k, n // dim, dim)
    return jnp.pad(w3, ((0, 0), (0, 0), (0, LANES - dim))).reshape(k, (n // dim) * LANES)


def _pad_cols(w, n):
    return jnp.pad(w, ((0, 0), (0, n - w.shape[1])))


def _layer_weights(l, w_in, mla_q_norm, mla_w_uq, mla_kv_norm, mla_w_ukv, w_branch_mla, w_branch_nsa, w_out,
                   ln1_g, ln1_b, peer_w_query, peer_sub_keys, peer_u, peer_v, ln2_g, ln2_b):
    off = np.cumsum((0,) + IN_WIDTHS)
    wi = w_in[l]
    seg = lambda k: wi[:, off[k]:off[k + 1]]
    G, dk = NSA_GROUPS, NSA_HEAD_DIM
    kv = seg(4).reshape(D_MODEL, 6, G * dk)
    lw = {}
    lw["w_cqkv"] = jnp.concatenate([seg(0), seg(1)], axis=1).astype(CDT)
    kr = seg(2)
    lw["w_kr"] = _pad_cols(kr, LANES).astype(CDT)
    lw["w_kr_rot"] = _pad_cols(_rot_cols(kr, MLA_ROPE_DIM), LANES).astype(CDT)
    lw["w_nq"] = _pad_heads(seg(3), dk).astype(CDT)
    lw["w_nq_rot"] = _pad_heads(_rot_cols(seg(3), dk), dk).astype(CDT)
    lw["w_kcmp"] = kv[:, 0].astype(CDT)
    lw["w_kcmp_rot"] = _rot_cols(kv[:, 0], dk).astype(CDT)
    for name, idx in (("ksel", 2), ("kwin", 4)):
        lw["w_" + name] = _pad_heads(kv[:, idx], dk).astype(CDT)
        lw["w_" + name + "_rot"] = _pad_heads(_rot_cols(kv[:, idx], dk), dk).astype(CDT)
    lw["w_vall"] = jnp.concatenate([kv[:, 1], _pad_heads(kv[:, 3], dk), _pad_heads(kv[:, 5], dk)], axis=1).astype(CDT)
    ng = seg(5).reshape(D_MODEL, G, NSA_HPG * 3)
    lw["w_ngate"] = jnp.pad(ng, ((0, 0), (0, 0), (0, LANES - NSA_HPG * 3))).reshape(D_MODEL, G * LANES).astype(CDT)
    lw["w_mgate"] = seg(6).astype(CDT)

    H = MLA_HEADS
    dq = MLA_NOPE_DIM + MLA_ROPE_DIM
    uq = mla_w_uq[l].reshape(MLA_Q_RANK, H, dq)
    lw["wqn"] = _pad_heads(uq[:, :, :MLA_NOPE_DIM].reshape(MLA_Q_RANK, -1), MLA_NOPE_DIM).astype(CDT)
    wqr = uq[:, :, MLA_NOPE_DIM:].reshape(MLA_Q_RANK, H * MLA_ROPE_DIM)
    lw["wqr"] = wqr.astype(CDT)
    lw["wqr_rot"] = _rot_cols(wqr, MLA_ROPE_DIM).astype(CDT)
    pq = np.zeros((H * MLA_ROPE_DIM, H * LANES), np.float32)
    pk = np.zeros((LANES, H * LANES), np.float32)
    for h in range(H):
        for j in range(MLA_ROPE_DIM):
            pq[h * MLA_ROPE_DIM + j, h * LANES + MLA_NOPE_DIM + j] = 1.0
            pk[j, h * LANES + MLA_NOPE_DIM + j] = 1.0
    lw["place_q"] = jnp.asarray(pq, CDT)
    lw["place_k"] = jnp.asarray(pk, CDT)
    ukv = mla_w_ukv[l].reshape(MLA_KV_RANK, H, MLA_NOPE_DIM + MLA_V_DIM)
    lw["wkn"] = _pad_heads(ukv[:, :, :MLA_NOPE_DIM].reshape(MLA_KV_RANK, -1), MLA_NOPE_DIM).astype(CDT)
    lw["wv"] = ukv[:, :, MLA_NOPE_DIM:].reshape(MLA_KV_RANK, H * MLA_V_DIM).astype(CDT)
    lw["q_gain"] = mla_q_norm[l].reshape(1, -1)
    lw["kv_gain"] = mla_kv_norm[l].reshape(1, -1)

    lw["w_bm"] = w_branch_mla[l].astype(CDT)
    wbn = w_branch_nsa[l].reshape(NSA_HEADS, dk, D_MODEL)
    lw["w_bn"] = jnp.pad(wbn, ((0, 0), (0, LANES - dk), (0, 0))).reshape(NSA_HEADS * LANES, D_MODEL).astype(CDT)
    lw["w_out"] = w_out[l].astype(CDT)
    lw["ln1_g"] = ln1_g[l].reshape(1, -1)
    lw["ln1_b"] = ln1_b[l].reshape(1, -1)

    lw["w_pq"] = peer_w_query[l].astype(CDT)
    sk = peer_sub_keys[l].reshape(PEER_HEADS * 2, PEER_N_KEYS, PEER_KEY_DIM // 2)
    eye = jnp.eye(PEER_HEADS * 2, dtype=sk.dtype)
    lw["keys_bd_t"] = jnp.einsum("gnd,gf->gnfd", sk, eye).reshape(
        PEER_HEADS * 2 * PEER_N_KEYS, PEER_HEADS * PEER_KEY_DIM).astype(CDT)
    lw["u"] = peer_u[l].astype(CDT)
    lw["vt"] = peer_v[l].T.astype(CDT)
    lw["ln2_g"] = ln2_g[l].reshape(1, -1)
    lw["ln2_b"] = ln2_b[l].reshape(1, -1)
    return lw


def _rope_tables(positions, dim):
    inv_freq = ROPE_THETA ** (-jnp.arange(0, dim, 2, dtype=F32) / dim)
    ang = positions.astype(F32).reshape(-1, 1) * inv_freq
    rep = lambda t: jnp.tile(jnp.concatenate([t, t], axis=-1), (1, LANES // dim))
    return rep(jnp.cos(ang)), rep(jnp.sin(ang))


def _hybrid_layer(h, hb, B, S, lw, cmp_w, tab_mla, tab_nsa):
    T = B * S
    G, dk = NSA_GROUPS, NSA_HEAD_DIM
    cqkv = _proj(hb, lw["w_cqkv"], CDT, name="proj_cqkv")
    kr = _proj(hb, lw["w_kr"], CDT, lw["w_kr_rot"], tab_mla, name="proj_kr")
    nq = _proj(hb, lw["w_nq"], CDT, lw["w_nq_rot"], tab_nsa, name="proj_nq")
    kcmp = _proj(hb, lw["w_kcmp"], CDT, lw["w_kcmp_rot"], tab_nsa, name="proj_kcmp")
    lane = jnp.arange(G * LANES) % LANES - LANES // 2
    blk_onehot = (jnp.arange(S)[:, None] // SEL_BLOCK == lane[None, :]).astype(F32)
    ksel = _proj(hb, lw["w_ksel"], CDT, lw["w_ksel_rot"], tab_nsa, bias=jnp.tile(blk_onehot, (B, 1)),
                 name="proj_ksel")
    kwin = _proj(hb, lw["w_kwin"], CDT, lw["w_kwin_rot"], tab_nsa, name="proj_kwin")
    ones_lane = np.zeros((1, G * dk + 2 * G * LANES), np.float32)
    ones_lane[0, [G * dk + g * LANES + dk for g in range(2 * G)]] = 1.0
    vall = _proj(hb, lw["w_vall"], CDT, bias=jnp.asarray(ones_lane), name="proj_v")
    ngate = _proj(hb, lw["w_ngate"], F32, act="sigmoid", name="proj_ngate")
    mgate = _proj(hb, lw["w_mgate"], CDT, act="sigmoid", name="proj_mgate")

    q, k, v = _mla_up(cqkv, kr, lw, tab_mla)
    o_mla = _mla_attn(q.reshape(B, S, -1), k.reshape(B, S, -1), v.reshape(B, S, -1))

    def chunks(a):
        return a.reshape(B, S, G, dk).transpose(0, 2, 1, 3).reshape(B * G, S // CMP_STRIDE, CMP_STRIDE * dk)

    kc = _compress(chunks(kcmp), cmp_w["k_pe"], cmp_w["k_w1"], cmp_w["k_w2"])
    vc = _compress(chunks(vall[:, :G * dk]), cmp_w["v_pe"], cmp_w["v_w1"], cmp_w["v_w2"])
    vsel = vall[:, G * dk:G * dk + G * LANES]
    vwin = vall[:, G * dk + G * LANES:]
    r3 = lambda a: a.reshape(B, S, -1)
    o_nsa = _nsa(r3(nq), kc, vc, r3(ksel), r3(vsel), r3(kwin), r3(vwin), r3(ngate))

    x1, x1b = _merge(o_mla.reshape(T, -1), o_nsa.reshape(T, -1), mgate, h, lw)
    stats = _peer_stats(x1b, lw["w_pq"], lw["keys_bd_t"])
    return _peer_dense(x1b, x1, lw["u"], lw["vt"], stats, lw["ln2_g"], lw["ln2_b"])


def kernel(x, positions, ln_in_g, ln_in_b, w_in, mla_q_norm, mla_w_uq, mla_kv_norm, mla_w_ukv, nsa_cmp_k_pe, nsa_cmp_k_w1, nsa_cmp_k_w2, nsa_cmp_v_pe, nsa_cmp_v_w1, nsa_cmp_v_w2, w_branch_mla, w_branch_nsa, w_out, ln1_g, ln1_b, peer_w_query, peer_sub_keys, peer_u, peer_v, ln2_g, ln2_b):
    B, S, D = x.shape
    tab_mla = _rope_tables(positions, MLA_ROPE_DIM)
    tab_nsa = _rope_tables(positions, NSA_HEAD_DIM)
    h, hb = _ln_in(x.reshape(B * S, D), ln_in_g, ln_in_b)
    for l in range(DEPTH):
        lw = _layer_weights(l, w_in, mla_q_norm, mla_w_uq, mla_kv_norm, mla_w_ukv, w_branch_mla, w_branch_nsa,
                            w_out, ln1_g, ln1_b, peer_w_query, peer_sub_keys, peer_u, peer_v, ln2_g, ln2_b)
        cmp_w = {"k_pe": nsa_cmp_k_pe[l], "k_w1": nsa_cmp_k_w1[l], "k_w2": nsa_cmp_k_w2[l],
                 "v_pe": nsa_cmp_v_pe[l], "v_w1": nsa_cmp_v_w1[l], "v_w2": nsa_cmp_v_w2[l]}
        h, hb = _hybrid_layer(h, hb, B, S, lw, cmp_w, tab_mla, tab_nsa)
    return h.reshape(B, S, D)
```

```python
import functools
import math

import numpy as np
import jax
import jax.numpy as jnp
from jax import lax
from jax.experimental import pallas as pl
from jax.experimental.pallas import tpu as pltpu

D_MODEL = 1024
DEPTH = 2
ROPE_THETA = 10000.0

MLA_HEADS = 8
MLA_NOPE_DIM = 64
MLA_ROPE_DIM = 32
MLA_V_DIM = 64
MLA_Q_RANK = 768
MLA_KV_RANK = 256

NSA_HEADS = 8
NSA_GROUPS = 2
NSA_HPG = NSA_HEADS // NSA_GROUPS
NSA_HEAD_DIM = 64
CMP_BLOCK = 32
CMP_STRIDE = 16
CMP_HIDDEN = 2 * NSA_HEAD_DIM
SEL_BLOCK = 64
SEL_TOPN = 16
WINDOW = 512
FORCE_SCORE = 1e9

PEER_HEADS = 8
PEER_KEY_DIM = 128
PEER_N_KEYS = 128
PEER_N_EXPERTS = PEER_N_KEYS * PEER_N_KEYS
PEER_TOPK = 16

DEEPNORM_ALPHA = (2 * DEPTH) ** 0.25
LN_EPS = 1e-5
RMS_EPS = 1e-6
NEG_BIG = -1e30
LOG2E = math.log2(math.e)
MASK_BIAS = 32768.0

IN_WIDTHS = (MLA_Q_RANK, MLA_KV_RANK, MLA_ROPE_DIM, NSA_HEADS * NSA_HEAD_DIM,
             6 * NSA_GROUPS * NSA_HEAD_DIM, 3 * NSA_HEADS, 2 * D_MODEL)

LANES = 128
VMEM_LIMIT = 56 * 1024 * 1024

F32 = jnp.float32
CDT = jnp.bfloat16

_NT = (((1,), (1,)), ((), ()))


def _dot(a, b):
    return jnp.dot(a, b, preferred_element_type=F32)


def _dot_nt(a, b):
    return lax.dot_general(a, b, _NT, preferred_element_type=F32)


def _params(*sem):
    return pltpu.CompilerParams(dimension_semantics=sem, vmem_limit_bytes=VMEM_LIMIT)


def _gelu(x):
    return 0.5 * x * (1.0 + lax.erf(x * (2.0 ** -0.5)))


def _layer_norm(z, g, b):
    mu = jnp.mean(z, axis=-1, keepdims=True)
    d = z - mu
    var = jnp.mean(d * d, axis=-1, keepdims=True)
    return d * lax.rsqrt(var + LN_EPS) * g + b


def _ln_in_kernel(x_ref, g_ref, b_ref, h_ref, hb_ref):
    h = _layer_norm(x_ref[...], g_ref[...], b_ref[...])
    h_ref[...] = h
    hb_ref[...] = h.astype(CDT)


def _ln_in(x, g, b, tm=512):
    T, D = x.shape
    row = pl.BlockSpec((tm, D), lambda i: (i, 0))
    vec = pl.BlockSpec((1, D), lambda i: (0, 0))
    return pl.pallas_call(
        _ln_in_kernel, grid=(T // tm,), in_specs=[row, vec, vec], out_specs=[row, row],
        out_shape=[jax.ShapeDtypeStruct((T, D), F32), jax.ShapeDtypeStruct((T, D), CDT)],
        compiler_params=_params("parallel"), name="ln_in")(x, g.reshape(1, D), b.reshape(1, D))


_PROJ_CHUNK = 512


def _proj_all_kernel(hb_ref, w_ref, cm_ref, sm_ref, cn_ref, sn_ref, hot_ref, one_ref, *out_refs, plan):
    hb = hb_ref[...]
    tables = {"mla": (cm_ref, sm_ref), "nsa": (cn_ref, sn_ref)}
    for o_ref, (off, width, rot_off, table, act, bias) in zip(out_refs, plan):
        for c0 in range(0, width, _PROJ_CHUNK):
            cw = min(_PROJ_CHUNK, width - c0)
            y = _dot(hb, w_ref[:, off + c0:off + c0 + cw])
            if rot_off is not None:
                yr = _dot(hb, w_ref[:, rot_off + c0:rot_off + c0 + cw])
                cos = jnp.concatenate([tables[table][0][...]] * (cw // LANES), axis=1)
                sin = jnp.concatenate([tables[table][1][...]] * (cw // LANES), axis=1)
                y = y * cos + yr * sin
            if act == "sigmoid":
                y = 1.0 / (1.0 + jnp.exp(-y))
            if bias == "block_onehot":
                y = y + hot_ref[:, c0:c0 + cw]
            elif bias == "ones_lane":
                y = y + one_ref[:, c0:c0 + cw]
            o_ref[:, c0:c0 + cw] = y.astype(o_ref.dtype)


def _proj_all(hb, lw, tab_mla, tab_nsa, blk_onehot, ones_lane, tm=512):
    segs = [("cqkv", lw["w_cqkv"], None, None, None, None, CDT),
            ("kr", lw["w_kr"], lw["w_kr_rot"], "mla", None, None, CDT),
            ("nq", lw["w_nq"], lw["w_nq_rot"], "nsa", None, None, CDT),
            ("kcmp", lw["w_kcmp"], lw["w_kcmp_rot"], "nsa", None, None, CDT),
            ("ksel", lw["w_ksel"], lw["w_ksel_rot"], "nsa", None, "block_onehot", CDT),
            ("kwin", lw["w_kwin"], lw["w_kwin_rot"], "nsa", None, None, CDT),
            ("vall", lw["w_vall"], None, None, None, "ones_lane", CDT),
            ("ngate", lw["w_ngate"], None, None, "sigmoid", None, F32),
            ("mgate", lw["w_mgate"], None, None, "sigmoid", None, CDT)]
    T, K = hb.shape
    cols, plan, off = [], [], 0
    for _, w, w_rot, table, act, bias, _ in segs:
        width = w.shape[1]
        cols.append(w)
        rot_off = None
        if w_rot is not None:
            cols.append(w_rot)
            rot_off = off + width
        plan.append((off, width, rot_off, table, act, bias))
        off += width * (2 if w_rot is not None else 1)
    w_all = jnp.concatenate(cols, axis=1)
    row = lambda n: pl.BlockSpec((tm, n), lambda i: (i, 0))
    full = lambda a: pl.BlockSpec(a.shape, lambda i: (0, 0))
    outs = pl.pallas_call(
        functools.partial(_proj_all_kernel, plan=tuple(plan)), grid=(T // tm,),
        in_specs=[row(K), full(w_all), row(LANES), row(LANES), row(LANES), row(LANES),
                  row(blk_onehot.shape[1]), full(ones_lane)],
        out_specs=[row(s[1].shape[1]) for s in segs],
        out_shape=[jax.ShapeDtypeStruct((T, s[1].shape[1]), s[6]) for s in segs],
        compiler_params=_params("parallel"), name="proj_all")(
            hb, w_all, tab_mla[0], tab_mla[1], tab_nsa[0], tab_nsa[1], blk_onehot, ones_lane)
    return dict(zip([s[0] for s in segs], outs))


def _rms(x, g):
    return x * lax.rsqrt(jnp.mean(x * x, axis=-1, keepdims=True) + RMS_EPS) * g


def _mla_up_kernel(c_ref, kr_ref, qg_ref, kvg_ref, wqn_ref, wqr_ref, wqrr_ref, pq_ref,
                   wkn_ref, wv_ref, pk_ref, cos_ref, sin_ref, q_ref, k_ref, v_ref):
    c = c_ref[...].astype(F32)
    cqn = _rms(c[:, :MLA_Q_RANK], qg_ref[...]).astype(CDT)
    ckvn = _rms(c[:, MLA_Q_RANK:], kvg_ref[...]).astype(CDT)
    cos = jnp.concatenate([cos_ref[...]] * 2, axis=1)
    sin = jnp.concatenate([sin_ref[...]] * 2, axis=1)
    roped = (_dot(cqn, wqr_ref[...]) * cos + _dot(cqn, wqrr_ref[...]) * sin).astype(CDT)
    q_ref[...] = (_dot(cqn, wqn_ref[...]) + _dot(roped, pq_ref[...])).astype(CDT)
    k_ref[...] = (_dot(ckvn, wkn_ref[...]) + _dot(kr_ref[...], pk_ref[...])).astype(CDT)
    v_ref[...] = _dot(ckvn, wv_ref[...]).astype(CDT)


def _mla_up(cqkv, kr, lw, tables, tm=512):
    T = cqkv.shape[0]
    row = lambda n: pl.BlockSpec((tm, n), lambda i: (i, 0))
    full = lambda a: pl.BlockSpec(a.shape, lambda i: (0, 0))
    ws = (lw["q_gain"], lw["kv_gain"], lw["wqn"], lw["wqr"], lw["wqr_rot"], lw["place_q"],
          lw["wkn"], lw["wv"], lw["place_k"])
    hp = MLA_HEADS * LANES
    return pl.pallas_call(
        _mla_up_kernel, grid=(T // tm,),
        in_specs=[row(cqkv.shape[1]), row(LANES)] + [full(a) for a in ws] + [row(LANES), row(LANES)],
        out_specs=[row(hp), row(hp), row(MLA_HEADS * MLA_V_DIM)],
        out_shape=[jax.ShapeDtypeStruct((T, hp), CDT), jax.ShapeDtypeStruct((T, hp), CDT),
                   jax.ShapeDtypeStruct((T, MLA_HEADS * MLA_V_DIM), CDT)],
        compiler_params=_params("parallel"), name="mla_up")(cqkv, kr, *ws, tables[0], tables[1])


def _mla_attn_kernel(q_ref, k_ref, v_ref, o_ref, *, tq, scale):
    i = pl.program_id(2)
    outs = []
    for hd in range(2):
        q = q_ref[0, :, hd * LANES:(hd + 1) * LANES]

        def step(kt, carry, diagonal, hd=hd, q=q):
            m, l, acc = carry
            k0 = pl.multiple_of(kt * tq, tq)
            k = k_ref[0, pl.ds(k0, tq), hd * LANES:(hd + 1) * LANES]
            v = v_ref[0, pl.ds(k0, tq), :]
            s = _dot_nt(q, k) * scale
            if diagonal:
                row = lax.broadcasted_iota(jnp.int32, (tq, tq), 0)
                col = lax.broadcasted_iota(jnp.int32, (tq, tq), 1)
                s = jnp.where(col <= row, s, NEG_BIG)
            m_new = jnp.maximum(m, jnp.max(s, axis=1, keepdims=True))
            alpha = jnp.exp(m - m_new)
            p = jnp.exp(s - m_new)
            l = l * alpha + jnp.sum(p, axis=1, keepdims=True)
            acc = acc * alpha + _dot(p.astype(CDT), v)
            return m_new, l, acc

        init = (jnp.full((tq, 1), NEG_BIG, F32), jnp.zeros((tq, 1), F32), jnp.zeros((tq, LANES), F32))
        carry = lax.fori_loop(0, i, lambda kt, c: step(kt, c, False), init)
        _, l, acc = step(i, carry, True)
        outs.append(acc / l)
    lane = lax.broadcasted_iota(jnp.int32, (tq, LANES), 1)
    o_ref[0] = jnp.where(lane < MLA_V_DIM, outs[0], outs[1]).astype(o_ref.dtype)


def _mla_attn(q, k, v, tq=512):
    B, S, _ = q.shape
    scale = (MLA_NOPE_DIM + MLA_ROPE_DIM) ** -0.5
    kern = functools.partial(_mla_attn_kernel, tq=tq, scale=scale)
    return pl.pallas_call(
        kern, grid=(B, MLA_HEADS // 2, S // tq),
        in_specs=[pl.BlockSpec((1, tq, 2 * LANES), lambda b, h, i: (b, i, h)),
                  pl.BlockSpec((1, S, 2 * LANES), lambda b, h, i: (b, 0, h)),
                  pl.BlockSpec((1, S, LANES), lambda b, h, i: (b, 0, h))],
        out_specs=pl.BlockSpec((1, tq, LANES), lambda b, h, i: (b, i, h)),
        out_shape=jax.ShapeDtypeStruct((B, S, MLA_HEADS * MLA_V_DIM), CDT),
        compiler_params=_params("parallel", "parallel", "arbitrary"), name="mla_attn")(q, k, v)


def _compress_kernel(c_ref, pe_ref, w1a_ref, w1b_ref, w2_ref, o_ref, *, n_valid):
    c = c_ref[0].astype(F32)
    n = c.shape[0]
    a = _dot((c + pe_ref[0:1, :]).astype(CDT), w1a_ref[...])
    b = _dot((c + pe_ref[1:2, :]).astype(CDT), w1b_ref[...])
    hid = _gelu(a + pltpu.roll(b, n - 1, 0))
    out = _dot(hid.astype(CDT), w2_ref[...])
    row = lax.broadcasted_iota(jnp.int32, out.shape, 0)
    o_ref[0] = jnp.where(row < n_valid, out, 0.0).astype(o_ref.dtype)


def _compress(chunks, pe, w1, w2):
    BG, n, width = chunks.shape
    pe2 = pe.reshape(2, width).astype(F32)
    w1a = w1[:width].astype(CDT)
    w1b = w1[width:].astype(CDT)
    w2p = jnp.pad(w2, ((0, 0), (0, LANES - w2.shape[1]))).astype(CDT)
    full = lambda a: pl.BlockSpec(a.shape, lambda i: (0, 0))
    return pl.pallas_call(
        functools.partial(_compress_kernel, n_valid=n - 1), grid=(BG,),
        in_specs=[pl.BlockSpec((1, n, width), lambda i: (i, 0, 0)), full(pe2), full(w1a), full(w1b), full(w2p)],
        out_specs=pl.BlockSpec((1, n, LANES), lambda i: (i, 0, 0)),
        out_shape=jax.ShapeDtypeStruct((BG, n, LANES), CDT),
        compiler_params=_params("parallel"), name="nsa_compress")(chunks, pe2, w1a, w1b, w2p)


def _nsa_kernel(q_ref, kc_ref, vc_ref, ks_ref, vs_ref, kw_ref, vw_ref, g_ref, ovl_ref, edge_ref, o_ref,
                *, tq, n_sel, n_cmp, scale):
    i = pl.program_id(2)
    q0 = i * tq
    R = NSA_HPG
    ncp = kc_ref.shape[1]
    Q = jnp.concatenate([q_ref[0, :, r * LANES:(r + 1) * LANES] for r in range(R)], axis=0)

    scale2 = scale * LOG2E
    hidden = -MASK_BIAS * scale2
    t_c = q0 + lax.broadcasted_iota(jnp.int32, (tq, ncp), 0)
    n_c = lax.broadcasted_iota(jnp.int32, (tq, ncp), 1)
    bias_c = jnp.where((n_c * CMP_STRIDE + (CMP_BLOCK - 1) <= t_c) & (n_c < n_cmp), 0.0, hidden)
    s = (_dot_nt(Q, kc_ref[0]) * scale2).reshape(R, tq, ncp) + bias_c[None]
    p = jnp.exp2(s - jnp.max(s, axis=2, keepdims=True))
    t_q = q0 + lax.broadcasted_iota(jnp.int32, (tq, 1), 0)
    sees_any = jnp.where(t_q >= CMP_BLOCK - 1, 1.0, 0.0)
    p = p * (sees_any / jnp.sum(p, axis=2, keepdims=True))
    o_c = _dot(p.reshape(R * tq, ncp).astype(CDT), vc_ref[0])

    psum = jnp.sum(p, axis=0)
    p_hi = psum.astype(CDT)
    p_lo = (psum - p_hi.astype(F32)).astype(CDT)
    imp = (_dot_nt(ovl_ref[...], p_hi) + _dot_nt(ovl_ref[...], p_lo))[0:n_sel]
    jj = lax.broadcasted_iota(jnp.int32, (n_sel, tq), 0)
    cur = (q0 + lax.broadcasted_iota(jnp.int32, (n_sel, tq), 1)) // SEL_BLOCK
    forced = (jj == 0) | (jj == cur) | (jj == cur - 1)
    top_n = min(SEL_TOPN, n_sel)
    assert top_n > 3
    free = jnp.where(forced | (jj > cur), -FORCE_SCORE, imp)
    tops, v = [], free
    for _ in range(top_n - 1):
        m = jnp.max(v, axis=0, keepdims=True)
        tops.append(m)
        v = jnp.where(v == m, -jnp.inf, v)
    cur_row = cur[0:1, :]
    tau = jnp.where(cur_row >= 2, tops[top_n - 4], jnp.where(cur_row == 1, tops[top_n - 3], tops[top_n - 2]))
    half = LANES // 2
    parts = [jnp.zeros((half, tq), F32), jnp.where(forced | (free >= tau), 0.0, -MASK_BIAS)]
    if n_sel < half:
        parts.append(jnp.zeros((half - n_sel, tq), F32))
    q_bias = jnp.concatenate(parts, axis=0).T.astype(CDT)
    Qb = Q + jnp.concatenate([q_bias] * R, axis=0)

    causal_bias = edge_ref[0]
    window_bias = edge_ref[1]

    def step(k_ref, v_ref, kt, carry, bias):
        m, acc = carry
        k0 = pl.multiple_of(kt * tq, tq)
        s = (_dot_nt(Qb, k_ref[0, pl.ds(k0, tq), :]) * scale2).reshape(R, tq, tq)
        if bias is not None:
            s = s + bias[None]
        m_new = jnp.maximum(m, jnp.max(s, axis=2, keepdims=True))
        p = jnp.exp2(s - m_new).reshape(R * tq, tq).astype(CDT)
        pv = _dot(p, v_ref[0, pl.ds(k0, tq), :]).reshape(R, tq, LANES)
        return m_new, acc * jnp.exp2(m - m_new) + pv

    def finish(carry):
        acc = carry[1]
        return acc * (1.0 / acc[:, :, half:half + 1])

    def flag(cond):
        return jnp.where(cond, 1.0, 0.0)

    init = (jnp.full((R, tq, 1), NEG_BIG, F32), jnp.zeros((R, tq, LANES), F32))

    carry = lax.fori_loop(0, i, lambda kt, c: step(ks_ref, vs_ref, kt, c, None), init)
    o_s = finish(lax.fori_loop(i, i + 1, lambda kt, c: step(ks_ref, vs_ref, kt, c, causal_bias), carry))

    n_back = WINDOW // tq

    def win_step(it, c):
        kt = i - n_back + it
        bias = window_bias * flag(it == 0) + causal_bias * flag(it == n_back) + hidden * flag(kt < 0)
        return step(kw_ref, vw_ref, jnp.maximum(kt, 0), c, bias)

    o_w = finish(lax.fori_loop(0, n_back + 1, win_step, init))

    g = g_ref[0]
    o_c = o_c.reshape(R, tq, LANES)
    for r in range(R):
        o = (g[:, 3 * r:3 * r + 1] * o_c[r] + g[:, 3 * r + 1:3 * r + 2] * o_s[r]
             + g[:, 3 * r + 2:3 * r + 3] * o_w[r])
        o_ref[0, :, r * LANES:(r + 1) * LANES] = o.astype(o_ref.dtype)


def _nsa(q, kc, vc, ks, vs, kw, vw, gates, tq=512):
    B, S, _ = q.shape
    G = NSA_GROUPS
    n_sel = S // SEL_BLOCK
    n_cmp = (S - CMP_BLOCK) // CMP_STRIDE + 1
    ncp = kc.shape[1]
    assert n_sel <= LANES // 2 and n_sel % 8 == 0 and ncp % LANES == 0 and WINDOW % tq == 0
    jn = np.arange(LANES)[:, None] * SEL_BLOCK
    cn = np.arange(ncp)[None, :] * CMP_STRIDE
    ovl = ((cn <= jn + SEL_BLOCK - 1) & (cn + CMP_BLOCK - 1 >= jn)
           & (np.arange(LANES)[:, None] < n_sel) & (np.arange(ncp)[None, :] < n_cmp))
    ovl = jnp.asarray(ovl, CDT)
    scale = NSA_HEAD_DIM ** -0.5
    hidden = -MASK_BIAS * scale * LOG2E
    row, col = np.arange(tq)[:, None], np.arange(tq)[None, :]
    edge = jnp.asarray(np.stack([np.where(col <= row, 0.0, hidden), np.where(col > row, 0.0, hidden)]), F32)
    kern = functools.partial(_nsa_kernel, tq=tq, n_sel=n_sel, n_cmp=n_cmp, scale=scale)
    cmp_spec = pl.BlockSpec((1, ncp, LANES), lambda b, g, i: (b * G + g, 0, 0))
    kv_spec = pl.BlockSpec((1, S, LANES), lambda b, g, i: (b, 0, g))
    qo_spec = pl.BlockSpec((1, tq, NSA_HPG * LANES), lambda b, g, i: (b, i, g))
    return pl.pallas_call(
        kern, grid=(B, G, S // tq),
        in_specs=[qo_spec, cmp_spec, cmp_spec, kv_spec, kv_spec, kv_spec, kv_spec,
                  pl.BlockSpec((1, tq, LANES), lambda b, g, i: (b, i, g)),
                  pl.BlockSpec(ovl.shape, lambda b, g, i: (0, 0)),
                  pl.BlockSpec(edge.shape, lambda b, g, i: (0, 0, 0))],
        out_specs=qo_spec,
        out_shape=jax.ShapeDtypeStruct((B, S, NSA_HEADS * LANES), CDT),
        compiler_params=_params("parallel", "parallel", "arbitrary"), name="nsa_attn")(
            q, kc, vc, ks, vs, kw, vw, gates, ovl, edge)


def _merge_kernel(om_ref, on_ref, mg_ref, h_ref, wbm_ref, wbn_ref, wo_ref, g_ref, b_ref, x_ref, xb_ref):
    D = D_MODEL
    y_mla = _dot(om_ref[...], wbm_ref[...])
    y_nsa = _dot(on_ref[...], wbn_ref[...])
    mg = mg_ref[...].astype(F32)
    mixed = _dot((mg[:, :D] * y_mla + mg[:, D:] * y_nsa).astype(CDT), wo_ref[...])
    x = _layer_norm(DEEPNORM_ALPHA * h_ref[...] + mixed, g_ref[...], b_ref[...])
    x_ref[...] = x
    xb_ref[...] = x.astype(CDT)


def _merge(o_mla, o_nsa, mg, h, lw, tm=512):
    T, D = h.shape
    row = lambda n: pl.BlockSpec((tm, n), lambda i: (i, 0))
    full = lambda a: pl.BlockSpec(a.shape, lambda i: (0, 0))
    ws = (lw["w_bm"], lw["w_bn"], lw["w_out"], lw["ln1_g"], lw["ln1_b"])
    return pl.pallas_call(
        _merge_kernel, grid=(T // tm,),
        in_specs=[row(o_mla.shape[1]), row(o_nsa.shape[1]), row(2 * D), row(D)] + [full(a) for a in ws],
        out_specs=[row(D), row(D)],
        out_shape=[jax.ShapeDtypeStruct((T, D), F32), jax.ShapeDtypeStruct((T, D), CDT)],
        compiler_params=_params("parallel"), name="merge_ln1")(o_mla, o_nsa, mg, h, *ws)


def _rows_per_vreg():
    return 8 * (4 // jnp.dtype(CDT).itemsize)


def _pack_words(x):
    if jnp.dtype(CDT).itemsize == 4:
        return pltpu.bitcast(x, jnp.uint32)
    bits = pltpu.bitcast(x.astype(CDT).astype(F32), jnp.uint32)
    return (bits >> 16) | (bits & jnp.uint32(0xFFFF0000))


def _unpack_words(row):
    return pltpu.bitcast(jnp.broadcast_to(row, (8, LANES)), CDT)


_PEER_PAIRS = [(a, b) for a in range(PEER_TOPK) for b in range(PEER_TOPK) if (a + 1) * (b + 1) <= PEER_TOPK]
_N_CAND = -(-len(_PEER_PAIRS) // 8) * 8
HEADS_PER_TRIP = 2


def _top_values(v, want_rank):
    tops = []
    rank = jnp.full(v.shape, 127.0, F32) if want_rank else None
    for r in range(PEER_TOPK):
        m = jnp.max(v, axis=0, keepdims=True)
        tops.append(m)
        eq = v == m
        if want_rank:
            rank = jnp.where(eq, float(r), rank)
        v = jnp.where(eq, -jnp.inf, v)
    return tops, rank


def _peer_stats_kernel(xb_ref, wq_ref, keys_ref, c1_ref, e1_ref, r2_ref, e2_ref, s_ref, cand_ref):
    qp = _dot(xb_ref[...], wq_ref[...]).astype(CDT)
    s_ref[...] = _dot_nt(keys_ref[...], qp)
    n = PEER_N_KEYS
    cand_ref[...] = jnp.full(cand_ref.shape, -jnp.inf, F32)

    def one_head(h, cand_ref):
        s1 = s_ref[pl.ds(pl.multiple_of(2 * h * n, n), n), :]
        s2 = s_ref[pl.ds(pl.multiple_of((2 * h + 1) * n, n), n), :]
        t1, _ = _top_values(s1, False)
        t2, rank2 = _top_values(s2, True)
        for c, (a, b) in enumerate(_PEER_PAIRS):
            cand_ref[c:c + 1, :] = t1[a] + t2[b]
        cand = cand_ref[...]
        tc, _ = _top_values(cand, False)
        tau = tc[PEER_TOPK - 1]
        top = t1[0] + t2[0]
        z = jnp.sum(jnp.where(cand >= tau, jnp.exp(cand - top), 0.0), axis=0, keepdims=True)
        cnt = jnp.zeros(s1.shape, F32)
        for b in range(PEER_TOPK):
            cnt = cnt + jnp.where(s1 + t2[b] >= tau, 1.0, 0.0)
        c1_ref[h] = _pack_words(cnt)
        e1_ref[h] = _pack_words(jnp.exp(s1 - t1[0]))
        r2_ref[h] = rank2.astype(r2_ref.dtype)
        e2_ref[h] = (jnp.exp(s2 - t2[0]) / z).astype(e2_ref.dtype)

    def heads(idx, carry):
        for k in range(HEADS_PER_TRIP):
            one_head(idx * HEADS_PER_TRIP + k, cand_ref.at[k])
        return carry

    lax.fori_loop(0, PEER_HEADS // HEADS_PER_TRIP, heads, 0)


def _peer_stats(xb, wq, keys_bd_t, tm=256):
    T, D = xb.shape
    H, n = PEER_HEADS, PEER_N_KEYS
    full = lambda a: pl.BlockSpec(a.shape, lambda i: (0, 0))
    o_spec = pl.BlockSpec((H, n, tm), lambda i: (0, 0, i))
    o_u32 = jax.ShapeDtypeStruct((H, n, T), jnp.uint32)
    o_cdt = jax.ShapeDtypeStruct((H, n, T), CDT)
    return pl.pallas_call(
        _peer_stats_kernel, grid=(T // tm,),
        in_specs=[pl.BlockSpec((tm, D), lambda i: (i, 0)), full(wq), full(keys_bd_t)],
        out_specs=[o_spec] * 4, out_shape=[o_u32, o_u32, o_cdt, o_cdt],
        scratch_shapes=[pltpu.VMEM((2 * H * n, tm), F32), pltpu.VMEM((HEADS_PER_TRIP, _N_CAND, tm), F32)],
        compiler_params=_params("parallel"), name="peer_stats")(xb, wq, keys_bd_t)


def _peer_dense_kernel(xb_ref, x_ref, u_ref, vt_ref, c1_ref, e1_ref, r2_in_ref, e2_in_ref, g_ref, b_ref,
                       o_ref, ob_ref, acc_ref, a_ref, h_ref, r2_ref, e2_ref, *, tm, te):
    j = pl.program_id(1)
    n = PEER_N_KEYS
    rows_per_tile = te // n

    @pl.when(j == 0)
    def _():
        acc_ref[...] = jnp.zeros(acc_ref.shape, F32)
        r2_ref[...] = r2_in_ref[...]
        e2_ref[...] = e2_in_ref[...]

    a_ref[...] = _gelu(_dot_nt(u_ref[...], xb_ref[...])).astype(CDT)
    ic = n // 2
    sub = _rows_per_vreg()
    nv = ic // sub
    zero = jnp.zeros((nv, sub, LANES), CDT)
    n_i2 = n // ic

    def gate_block(idx):
        sl = pl.ds((idx // n_i2) * LANES, LANES)
        i2 = (idx % n_i2) * ic
        ws = [zero] * rows_per_tile
        for h in range(PEER_HEADS):
            r2 = r2_ref[h, pl.ds(i2, ic), sl].reshape(nv, sub, LANES)
            e2 = e2_ref[h, pl.ds(i2, ic), sl].reshape(nv, sub, LANES)
            for r in range(rows_per_tile):
                c1 = _unpack_words(c1_ref[h, r:r + 1, sl])[None]
                e1 = _unpack_words(e1_ref[h, r:r + 1, sl])[None]
                ws[r] = ws[r] + jnp.maximum(jnp.minimum(e1 * e2, c1 - r2), zero)
        for r in range(rows_per_tile):
            rows = pl.ds(r * n + i2, ic)
            h_ref[rows, sl] = ws[r].reshape(ic, LANES) * a_ref[rows, sl]

    for idx in range((tm // LANES) * n_i2):
        gate_block(idx)
    acc_ref[...] += _dot(vt_ref[...], h_ref[...])

    @pl.when(j == pl.num_programs(1) - 1)
    def _():
        z = DEEPNORM_ALPHA * x_ref[...] + acc_ref[...].T
        out = _layer_norm(z, g_ref[...], b_ref[...])
        o_ref[...] = out
        ob_ref[...] = out.astype(CDT)


def _peer_dense(xb, x, u, vt, stats, ln_g, ln_b, tm=512, te=1024):
    T, D = x.shape
    E = u.shape[0]
    H, n = PEER_HEADS, PEER_N_KEYS
    row = pl.BlockSpec((tm, D), lambda i, j: (i, 0))
    st = pl.BlockSpec((H, n, tm), lambda i, j: (0, 0, i))
    st1 = pl.BlockSpec((H, te // n, tm), lambda i, j: (0, j, i))
    vec = pl.BlockSpec((1, D), lambda i, j: (0, 0))
    assert (te // n) % 8 == 0
    kern = functools.partial(_peer_dense_kernel, tm=tm, te=te)
    return pl.pallas_call(
        kern, grid=(T // tm, E // te),
        in_specs=[row, row, pl.BlockSpec((te, D), lambda i, j: (j, 0)), pl.BlockSpec((D, te), lambda i, j: (0, j)),
                  st1, st1, st, st, vec, vec],
        out_specs=[row, row],
        out_shape=[jax.ShapeDtypeStruct((T, D), F32), jax.ShapeDtypeStruct((T, D), CDT)],
        scratch_shapes=[pltpu.VMEM((D, tm), F32), pltpu.VMEM((te, tm), CDT), pltpu.VMEM((te, tm), CDT),
                        pltpu.VMEM((H, n, tm), CDT), pltpu.VMEM((H, n, tm), CDT)],
        compiler_params=_params("parallel", "arbitrary"), name="peer_dense")(
            xb, x, u, vt, *stats, ln_g, ln_b)


def _rot_cols(w, dim):
    k, n = w.shape
    w4 = w.reshape(k, n // dim, 2, dim // 2)
    return jnp.stack([-w4[:, :, 1], w4[:, :, 0]], axis=2).reshape(k, n)


def _pad_heads(w, dim):
    k, n = w.shape
    w3 = w.reshape(k, n // dim, dim)
    return jnp.pad(w3, ((0, 0), (0, 0), (0, LANES - dim))).reshape(k, (n // dim) * LANES)


def _pad_cols(w, n):
    return jnp.pad(w, ((0, 0), (0, n - w.shape[1])))


def _layer_weights(l, w_in, mla_q_norm, mla_w_uq, mla_kv_norm, mla_w_ukv, w_branch_mla, w_branch_nsa, w_out,
                   ln1_g, ln1_b, peer_w_query, peer_sub_keys, peer_u, peer_v, ln2_g, ln2_b):
    off = np.cumsum((0,) + IN_WIDTHS)
    wi = w_in[l]
    seg = lambda k: wi[:, off[k]:off[k + 1]]
    G, dk = NSA_GROUPS, NSA_HEAD_DIM
    kv = seg(4).reshape(D_MODEL, 6, G * dk)
    lw = {}
    lw["w_cqkv"] = jnp.concatenate([seg(0), seg(1)], axis=1).astype(CDT)
    kr = seg(2)
    lw["w_kr"] = _pad_cols(kr, LANES).astype(CDT)
    lw["w_kr_rot"] = _pad_cols(_rot_cols(kr, MLA_ROPE_DIM), LANES).astype(CDT)
    lw["w_nq"] = _pad_heads(seg(3), dk).astype(CDT)
    lw["w_nq_rot"] = _pad_heads(_rot_cols(seg(3), dk), dk).astype(CDT)
    lw["w_kcmp"] = kv[:, 0].astype(CDT)
    lw["w_kcmp_rot"] = _rot_cols(kv[:, 0], dk).astype(CDT)
    for name, idx in (("ksel", 2), ("kwin", 4)):
        lw["w_" + name] = _pad_heads(kv[:, idx], dk).astype(CDT)
        lw["w_" + name + "_rot"] = _pad_heads(_rot_cols(kv[:, idx], dk), dk).astype(CDT)
    lw["w_vall"] = jnp.concatenate([kv[:, 1], _pad_heads(kv[:, 3], dk), _pad_heads(kv[:, 5], dk)], axis=1).astype(CDT)
    ng = seg(5).reshape(D_MODEL, G, NSA_HPG * 3)
    lw["w_ngate"] = jnp.pad(ng, ((0, 0), (0, 0), (0, LANES - NSA_HPG * 3))).reshape(D_MODEL, G * LANES).astype(CDT)
    lw["w_mgate"] = seg(6).astype(CDT)

    H = MLA_HEADS
    dq = MLA_NOPE_DIM + MLA_ROPE_DIM
    uq = mla_w_uq[l].reshape(MLA_Q_RANK, H, dq)
    lw["wqn"] = _pad_heads(uq[:, :, :MLA_NOPE_DIM].reshape(MLA_Q_RANK, -1), MLA_NOPE_DIM).astype(CDT)
    wqr = uq[:, :, MLA_NOPE_DIM:].reshape(MLA_Q_RANK, H * MLA_ROPE_DIM)
    lw["wqr"] = wqr.astype(CDT)
    lw["wqr_rot"] = _rot_cols(wqr, MLA_ROPE_DIM).astype(CDT)
    pq = np.zeros((H * MLA_ROPE_DIM, H * LANES), np.float32)
    pk = np.zeros((LANES, H * LANES), np.float32)
    for h in range(H):
        for j in range(MLA_ROPE_DIM):
            pq[h * MLA_ROPE_DIM + j, h * LANES + MLA_NOPE_DIM + j] = 1.0
            pk[j, h * LANES + MLA_NOPE_DIM + j] = 1.0
    lw["place_q"] = jnp.asarray(pq, CDT)
    lw["place_k"] = jnp.asarray(pk, CDT)
    ukv = mla_w_ukv[l].reshape(MLA_KV_RANK, H, MLA_NOPE_DIM + MLA_V_DIM)
    lw["wkn"] = _pad_heads(ukv[:, :, :MLA_NOPE_DIM].reshape(MLA_KV_RANK, -1), MLA_NOPE_DIM).astype(CDT)
    lw["wv"] = ukv[:, :, MLA_NOPE_DIM:].reshape(MLA_KV_RANK, H * MLA_V_DIM).astype(CDT)
    lw["q_gain"] = mla_q_norm[l].reshape(1, -1)
    lw["kv_gain"] = mla_kv_norm[l].reshape(1, -1)

    lw["w_bm"] = w_branch_mla[l].astype(CDT)
    wbn = w_branch_nsa[l].reshape(NSA_HEADS, dk, D_MODEL)
    lw["w_bn"] = jnp.pad(wbn, ((0, 0), (0, LANES - dk), (0, 0))).reshape(NSA_HEADS * LANES, D_MODEL).astype(CDT)
    lw["w_out"] = w_out[l].astype(CDT)
    lw["ln1_g"] = ln1_g[l].reshape(1, -1)
    lw["ln1_b"] = ln1_b[l].reshape(1, -1)

    lw["w_pq"] = peer_w_query[l].astype(CDT)
    sk = peer_sub_keys[l].reshape(PEER_HEADS * 2, PEER_N_KEYS, PEER_KEY_DIM // 2)
    eye = jnp.eye(PEER_HEADS * 2, dtype=sk.dtype)
    lw["keys_bd_t"] = jnp.einsum("gnd,gf->gnfd", sk, eye).reshape(
        PEER_HEADS * 2 * PEER_N_KEYS, PEER_HEADS * PEER_KEY_DIM).astype(CDT)
    lw["u"] = peer_u[l].astype(CDT)
    lw["vt"] = peer_v[l].T.astype(CDT)
    lw["ln2_g"] = ln2_g[l].reshape(1, -1)
    lw["ln2_b"] = ln2_b[l].reshape(1, -1)
    return lw


def _rope_tables(positions, dim):
    inv_freq = ROPE_THETA ** (-jnp.arange(0, dim, 2, dtype=F32) / dim)
    ang = positions.astype(F32).reshape(-1, 1) * inv_freq
    rep = lambda t: jnp.tile(jnp.concatenate([t, t], axis=-1), (1, LANES // dim))
    return rep(jnp.cos(ang)), rep(jnp.sin(ang))


def _hybrid_layer(h, hb, B, S, lw, cmp_w, tab_mla, tab_nsa):
    T = B * S
    G, dk = NSA_GROUPS, NSA_HEAD_DIM
    lane = jnp.arange(G * LANES) % LANES - LANES // 2
    blk_onehot = jnp.tile((jnp.arange(S)[:, None] // SEL_BLOCK == lane[None, :]).astype(F32), (B, 1))
    ones_lane = np.zeros((1, G * dk + 2 * G * LANES), np.float32)
    ones_lane[0, [G * dk + g * LANES + dk for g in range(2 * G)]] = 1.0
    p = _proj_all(hb, lw, tab_mla, tab_nsa, blk_onehot, jnp.asarray(ones_lane))
    cqkv, kr, nq, kcmp, ksel, kwin = p["cqkv"], p["kr"], p["nq"], p["kcmp"], p["ksel"], p["kwin"]
    vall, ngate, mgate = p["vall"], p["ngate"], p["mgate"]

    q, k, v = _mla_up(cqkv, kr, lw, tab_mla)
    o_mla = _mla_attn(q.reshape(B, S, -1), k.reshape(B, S, -1), v.reshape(B, S, -1))

    def chunks(a):
        return a.reshape(B, S, G, dk).transpose(0, 2, 1, 3).reshape(B * G, S // CMP_STRIDE, CMP_STRIDE * dk)

    kc = _compress(chunks(kcmp), cmp_w["k_pe"], cmp_w["k_w1"], cmp_w["k_w2"])
    vc = _compress(chunks(vall[:, :G * dk]), cmp_w["v_pe"], cmp_w["v_w1"], cmp_w["v_w2"])
    vsel = vall[:, G * dk:G * dk + G * LANES]
    vwin = vall[:, G * dk + G * LANES:]
    r3 = lambda a: a.reshape(B, S, -1)
    o_nsa = _nsa(r3(nq), kc, vc, r3(ksel), r3(vsel), r3(kwin), r3(vwin), r3(ngate))

    x1, x1b = _merge(o_mla.reshape(T, -1), o_nsa.reshape(T, -1), mgate, h, lw)
    stats = _peer_stats(x1b, lw["w_pq"], lw["keys_bd_t"])
    return _peer_dense(x1b, x1, lw["u"], lw["vt"], stats, lw["ln2_g"], lw["ln2_b"])


def kernel(x, positions, ln_in_g, ln_in_b, w_in, mla_q_norm, mla_w_uq, mla_kv_norm, mla_w_ukv, nsa_cmp_k_pe, nsa_cmp_k_w1, nsa_cmp_k_w2, nsa_cmp_v_pe, nsa_cmp_v_w1, nsa_cmp_v_w2, w_branch_mla, w_branch_nsa, w_out, ln1_g, ln1_b, peer_w_query, peer_sub_keys, peer_u, peer_v, ln2_g, ln2_b):
    B, S, D = x.shape
    tab_mla = _rope_tables(positions, MLA_ROPE_DIM)
    tab_nsa = _rope_tables(positions, NSA_HEAD_DIM)
    h, hb = _ln_in(x.reshape(B * S, D), ln_in_g, ln_in_b)
    for l in range(DEPTH):
        lw = _layer_weights(l, w_in, mla_q_norm, mla_w_uq, mla_kv_norm, mla_w_ukv, w_branch_mla, w_branch_nsa,
                            w_out, ln1_g, ln1_b, peer_w_query, peer_sub_keys, peer_u, peer_v, ln2_g, ln2_b)
        cmp_w = {"k_pe": nsa_cmp_k_pe[l], "k_w1": nsa_cmp_k_w1[l], "k_w2": nsa_cmp_k_w2[l],
                 "v_pe": nsa_cmp_v_pe[l], "v_w1": nsa_cmp_v_w1[l], "v_w2": nsa_cmp_v_w2[l]}
        h, hb = _hybrid_layer(h, hb, B, S, lw, cmp_w, tab_mla, tab_nsa)
    return h.reshape(B, S, D)
```

```python
import functools
import math

import numpy as np
import jax
import jax.numpy as jnp
from jax import lax
from jax.experimental import pallas as pl
from jax.experimental.pallas import tpu as pltpu

D_MODEL = 1024
DEPTH = 2
ROPE_THETA = 10000.0

MLA_HEADS = 8
MLA_NOPE_DIM = 64
MLA_ROPE_DIM = 32
MLA_V_DIM = 64
MLA_Q_RANK = 768
MLA_KV_RANK = 256

NSA_HEADS = 8
NSA_GROUPS = 2
NSA_HPG = NSA_HEADS // NSA_GROUPS
NSA_HEAD_DIM = 64
CMP_BLOCK = 32
CMP_STRIDE = 16
CMP_HIDDEN = 2 * NSA_HEAD_DIM
SEL_BLOCK = 64
SEL_TOPN = 16
WINDOW = 512
FORCE_SCORE = 1e9

PEER_HEADS = 8
PEER_KEY_DIM = 128
PEER_N_KEYS = 128
PEER_N_EXPERTS = PEER_N_KEYS * PEER_N_KEYS
PEER_TOPK = 16

DEEPNORM_ALPHA = (2 * DEPTH) ** 0.25
LN_EPS = 1e-5
RMS_EPS = 1e-6
NEG_BIG = -1e30
LOG2E = math.log2(math.e)
MASK_BIAS = 32768.0

IN_WIDTHS = (MLA_Q_RANK, MLA_KV_RANK, MLA_ROPE_DIM, NSA_HEADS * NSA_HEAD_DIM,
             6 * NSA_GROUPS * NSA_HEAD_DIM, 3 * NSA_HEADS, 2 * D_MODEL)

LANES = 128
VMEM_LIMIT = 56 * 1024 * 1024

F32 = jnp.float32
CDT = jnp.bfloat16

_NT = (((1,), (1,)), ((), ()))


def _dot(a, b):
    return jnp.dot(a, b, preferred_element_type=F32)


def _dot_nt(a, b):
    return lax.dot_general(a, b, _NT, preferred_element_type=F32)


def _params(*sem):
    return pltpu.CompilerParams(dimension_semantics=sem, vmem_limit_bytes=VMEM_LIMIT)


def _gelu(x):
    return 0.5 * x * (1.0 + lax.erf(x * (2.0 ** -0.5)))


def _layer_norm(z, g, b):
    mu = jnp.mean(z, axis=-1, keepdims=True)
    d = z - mu
    var = jnp.mean(d * d, axis=-1, keepdims=True)
    return d * lax.rsqrt(var + LN_EPS) * g + b


def _ln_in_kernel(x_ref, g_ref, b_ref, h_ref, hb_ref):
    h = _layer_norm(x_ref[...], g_ref[...], b_ref[...])
    h_ref[...] = h
    hb_ref[...] = h.astype(CDT)


def _ln_in(x, g, b, tm=512):
    T, D = x.shape
    row = pl.BlockSpec((tm, D), lambda i: (i, 0))
    vec = pl.BlockSpec((1, D), lambda i: (0, 0))
    return pl.pallas_call(
        _ln_in_kernel, grid=(T // tm,), in_specs=[row, vec, vec], out_specs=[row, row],
        out_shape=[jax.ShapeDtypeStruct((T, D), F32), jax.ShapeDtypeStruct((T, D), CDT)],
        compiler_params=_params("parallel"), name="ln_in")(x, g.reshape(1, D), b.reshape(1, D))


_PROJ_CHUNK = 512


def _proj_all_kernel(hb_ref, w_ref, cm_ref, sm_ref, cn_ref, sn_ref, hot_ref, one_ref, *out_refs, plan):
    hb = hb_ref[...]
    tables = {"mla": (cm_ref, sm_ref), "nsa": (cn_ref, sn_ref)}
    for o_ref, (off, width, rot_off, table, act, bias) in zip(out_refs, plan):
        for c0 in range(0, width, _PROJ_CHUNK):
            cw = min(_PROJ_CHUNK, width - c0)
            y = _dot(hb, w_ref[:, off + c0:off + c0 + cw])
            if rot_off is not None:
                yr = _dot(hb, w_ref[:, rot_off + c0:rot_off + c0 + cw])
                cos = jnp.concatenate([tables[table][0][...]] * (cw // LANES), axis=1)
                sin = jnp.concatenate([tables[table][1][...]] * (cw // LANES), axis=1)
                y = y * cos + yr * sin
            if act == "sigmoid":
                y = 1.0 / (1.0 + jnp.exp(-y))
            if bias == "block_onehot":
                y = y + hot_ref[:, c0:c0 + cw]
            elif bias == "ones_lane":
                y = y + one_ref[:, c0:c0 + cw]
            o_ref[:, c0:c0 + cw] = y.astype(o_ref.dtype)


def _proj_all(hb, lw, tab_mla, tab_nsa, blk_onehot, ones_lane, tm=512):
    segs = [("cqkv", lw["w_cqkv"], None, None, None, None, CDT),
            ("kr", lw["w_kr"], lw["w_kr_rot"], "mla", None, None, CDT),
            ("nq", lw["w_nq"], lw["w_nq_rot"], "nsa", None, None, CDT),
            ("kcmp", lw["w_kcmp"], lw["w_kcmp_rot"], "nsa", None, None, CDT),
            ("ksel", lw["w_ksel"], lw["w_ksel_rot"], "nsa", None, "block_onehot", CDT),
            ("kwin", lw["w_kwin"], lw["w_kwin_rot"], "nsa", None, None, CDT),
            ("vcmp", lw["w_vcmp"], None, None, None, None, CDT),
            ("vsel", lw["w_vsel"], None, None, None, "ones_lane", CDT),
            ("vwin", lw["w_vwin"], None, None, None, "ones_lane", CDT),
            ("ngate", lw["w_ngate"], None, None, "sigmoid", None, F32),
            ("mgate", lw["w_mgate"], None, None, "sigmoid", None, CDT)]
    T, K = hb.shape
    cols, plan, off = [], [], 0
    for _, w, w_rot, table, act, bias, _ in segs:
        width = w.shape[1]
        cols.append(w)
        rot_off = None
        if w_rot is not None:
            cols.append(w_rot)
            rot_off = off + width
        plan.append((off, width, rot_off, table, act, bias))
        off += width * (2 if w_rot is not None else 1)
    assert T % tm == 0 and blk_onehot.shape[0] % tm == 0
    w_all = jnp.concatenate(cols, axis=1)
    row = lambda n: pl.BlockSpec((tm, n), lambda i: (i, 0))
    full = lambda a: pl.BlockSpec(a.shape, lambda i: (0, 0))
    outs = pl.pallas_call(
        functools.partial(_proj_all_kernel, plan=tuple(plan)), grid=(T // tm,),
        in_specs=[row(K), full(w_all), row(LANES), row(LANES), row(LANES), row(LANES),
                  pl.BlockSpec((tm, blk_onehot.shape[1]), lambda i: (i % (blk_onehot.shape[0] // tm), 0)),
                  full(ones_lane)],
        out_specs=[row(s[1].shape[1]) for s in segs],
        out_shape=[jax.ShapeDtypeStruct((T, s[1].shape[1]), s[6]) for s in segs],
        compiler_params=_params("parallel"), name="proj_all")(
            hb, w_all, tab_mla[0], tab_mla[1], tab_nsa[0], tab_nsa[1], blk_onehot, ones_lane)
    return dict(zip([s[0] for s in segs], outs))


def _rms(x, g):
    return x * lax.rsqrt(jnp.mean(x * x, axis=-1, keepdims=True) + RMS_EPS) * g


def _mla_up_kernel(c_ref, kr_ref, qg_ref, kvg_ref, wqn_ref, wqr_ref, wqrr_ref, pq_ref,
                   wkn_ref, wv_ref, pk_ref, cos_ref, sin_ref, q_ref, k_ref, v_ref):
    c = c_ref[...].astype(F32)
    cqn = _rms(c[:, :MLA_Q_RANK], qg_ref[...]).astype(CDT)
    ckvn = _rms(c[:, MLA_Q_RANK:], kvg_ref[...]).astype(CDT)
    cos = jnp.concatenate([cos_ref[...]] * 2, axis=1)
    sin = jnp.concatenate([sin_ref[...]] * 2, axis=1)
    roped = (_dot(cqn, wqr_ref[...]) * cos + _dot(cqn, wqrr_ref[...]) * sin).astype(CDT)
    q_ref[...] = (_dot(cqn, wqn_ref[...]) + _dot(roped, pq_ref[...])).astype(CDT)
    k_ref[...] = (_dot(ckvn, wkn_ref[...]) + _dot(kr_ref[...], pk_ref[...])).astype(CDT)
    v_ref[...] = _dot(ckvn, wv_ref[...]).astype(CDT)


def _mla_up(cqkv, kr, lw, tables, tm=512):
    T = cqkv.shape[0]
    row = lambda n: pl.BlockSpec((tm, n), lambda i: (i, 0))
    full = lambda a: pl.BlockSpec(a.shape, lambda i: (0, 0))
    ws = (lw["q_gain"], lw["kv_gain"], lw["wqn"], lw["wqr"], lw["wqr_rot"], lw["place_q"],
          lw["wkn"], lw["wv"], lw["place_k"])
    hp = MLA_HEADS * LANES
    return pl.pallas_call(
        _mla_up_kernel, grid=(T // tm,),
        in_specs=[row(cqkv.shape[1]), row(LANES)] + [full(a) for a in ws] + [row(LANES), row(LANES)],
        out_specs=[row(hp), row(hp), row(MLA_HEADS * MLA_V_DIM)],
        out_shape=[jax.ShapeDtypeStruct((T, hp), CDT), jax.ShapeDtypeStruct((T, hp), CDT),
                   jax.ShapeDtypeStruct((T, MLA_HEADS * MLA_V_DIM), CDT)],
        compiler_params=_params("parallel"), name="mla_up")(cqkv, kr, *ws, tables[0], tables[1])


def _mla_attn_kernel(q_ref, k_ref, v_ref, o_ref, *, tq, scale):
    i = pl.program_id(2)
    outs = []
    for hd in range(2):
        q = q_ref[0, :, hd * LANES:(hd + 1) * LANES]

        def step(kt, carry, diagonal, hd=hd, q=q):
            m, l, acc = carry
            k0 = pl.multiple_of(kt * tq, tq)
            k = k_ref[0, pl.ds(k0, tq), hd * LANES:(hd + 1) * LANES]
            v = v_ref[0, pl.ds(k0, tq), :]
            s = _dot_nt(q, k) * scale
            if diagonal:
                row = lax.broadcasted_iota(jnp.int32, (tq, tq), 0)
                col = lax.broadcasted_iota(jnp.int32, (tq, tq), 1)
                s = jnp.where(col <= row, s, NEG_BIG)
            m_new = jnp.maximum(m, jnp.max(s, axis=1, keepdims=True))
            alpha = jnp.exp(m - m_new)
            p = jnp.exp(s - m_new)
            l = l * alpha + jnp.sum(p, axis=1, keepdims=True)
            acc = acc * alpha + _dot(p.astype(CDT), v)
            return m_new, l, acc

        init = (jnp.full((tq, 1), NEG_BIG, F32), jnp.zeros((tq, 1), F32), jnp.zeros((tq, LANES), F32))
        carry = lax.fori_loop(0, i, lambda kt, c: step(kt, c, False), init)
        _, l, acc = step(i, carry, True)
        outs.append(acc / l)
    lane = lax.broadcasted_iota(jnp.int32, (tq, LANES), 1)
    o_ref[0] = jnp.where(lane < MLA_V_DIM, outs[0], outs[1]).astype(o_ref.dtype)


def _mla_attn(q, k, v, tq=512):
    B, S, _ = q.shape
    scale = (MLA_NOPE_DIM + MLA_ROPE_DIM) ** -0.5
    kern = functools.partial(_mla_attn_kernel, tq=tq, scale=scale)
    return pl.pallas_call(
        kern, grid=(B, MLA_HEADS // 2, S // tq),
        in_specs=[pl.BlockSpec((1, tq, 2 * LANES), lambda b, h, i: (b, i, h)),
                  pl.BlockSpec((1, S, 2 * LANES), lambda b, h, i: (b, 0, h)),
                  pl.BlockSpec((1, S, LANES), lambda b, h, i: (b, 0, h))],
        out_specs=pl.BlockSpec((1, tq, LANES), lambda b, h, i: (b, i, h)),
        out_shape=jax.ShapeDtypeStruct((B, S, MLA_HEADS * MLA_V_DIM), CDT),
        compiler_params=_params("parallel", "parallel", "arbitrary"), name="mla_attn")(q, k, v)


def _compress_kernel(c_ref, pe_ref, w1a_ref, w1b_ref, w2_ref, o_ref, *, n_valid):
    c = c_ref[0].astype(F32)
    n = c.shape[0]
    a = _dot((c + pe_ref[0:1, :]).astype(CDT), w1a_ref[...])
    b = _dot((c + pe_ref[1:2, :]).astype(CDT), w1b_ref[...])
    hid = _gelu(a + pltpu.roll(b, n - 1, 0))
    out = _dot(hid.astype(CDT), w2_ref[...])
    row = lax.broadcasted_iota(jnp.int32, out.shape, 0)
    o_ref[0] = jnp.where(row < n_valid, out, 0.0).astype(o_ref.dtype)


def _compress(chunks, pe, w1, w2):
    BG, n, width = chunks.shape
    pe2 = pe.reshape(2, width).astype(F32)
    w1a = w1[:width].astype(CDT)
    w1b = w1[width:].astype(CDT)
    w2p = jnp.pad(w2, ((0, 0), (0, LANES - w2.shape[1]))).astype(CDT)
    full = lambda a: pl.BlockSpec(a.shape, lambda i: (0, 0))
    return pl.pallas_call(
        functools.partial(_compress_kernel, n_valid=n - 1), grid=(BG,),
        in_specs=[pl.BlockSpec((1, n, width), lambda i: (i, 0, 0)), full(pe2), full(w1a), full(w1b), full(w2p)],
        out_specs=pl.BlockSpec((1, n, LANES), lambda i: (i, 0, 0)),
        out_shape=jax.ShapeDtypeStruct((BG, n, LANES), CDT),
        compiler_params=_params("parallel"), name="nsa_compress")(chunks, pe2, w1a, w1b, w2p)


def _nsa_kernel(q_ref, kc_ref, vc_ref, ks_ref, vs_ref, kw_ref, vw_ref, g_ref, ovl_ref, edge_ref, o_ref,
                *, tq, n_sel, n_cmp, scale):
    i = pl.program_id(2)
    q0 = i * tq
    R = NSA_HPG
    ncp = kc_ref.shape[1]
    Q = jnp.concatenate([q_ref[0, :, r * LANES:(r + 1) * LANES] for r in range(R)], axis=0)

    scale2 = scale * LOG2E
    hidden = -MASK_BIAS * scale2
    t_c = q0 + lax.broadcasted_iota(jnp.int32, (tq, ncp), 0)
    n_c = lax.broadcasted_iota(jnp.int32, (tq, ncp), 1)
    bias_c = jnp.where((n_c * CMP_STRIDE + (CMP_BLOCK - 1) <= t_c) & (n_c < n_cmp), 0.0, hidden)
    s = (_dot_nt(Q, kc_ref[0]) * scale2).reshape(R, tq, ncp) + bias_c[None]
    p = jnp.exp2(s - jnp.max(s, axis=2, keepdims=True))
    t_q = q0 + lax.broadcasted_iota(jnp.int32, (tq, 1), 0)
    sees_any = jnp.where(t_q >= CMP_BLOCK - 1, 1.0, 0.0)
    p = p * (sees_any / jnp.sum(p, axis=2, keepdims=True))
    o_c = _dot(p.reshape(R * tq, ncp).astype(CDT), vc_ref[0])

    psum = jnp.sum(p, axis=0)
    p_hi = psum.astype(CDT)
    p_lo = (psum - p_hi.astype(F32)).astype(CDT)
    imp = (_dot_nt(ovl_ref[...], p_hi) + _dot_nt(ovl_ref[...], p_lo))[0:n_sel]
    jj = lax.broadcasted_iota(jnp.int32, (n_sel, tq), 0)
    cur = (q0 + lax.broadcasted_iota(jnp.int32, (n_sel, tq), 1)) // SEL_BLOCK
    forced = (jj == 0) | (jj == cur) | (jj == cur - 1)
    top_n = min(SEL_TOPN, n_sel)
    assert top_n > 3
    free = jnp.where(forced | (jj > cur), -FORCE_SCORE, imp)
    tops, v = [], free
    for _ in range(top_n - 1):
        m = jnp.max(v, axis=0, keepdims=True)
        tops.append(m)
        v = jnp.where(v == m, -jnp.inf, v)
    cur_row = cur[0:1, :]
    tau = jnp.where(cur_row >= 2, tops[top_n - 4], jnp.where(cur_row == 1, tops[top_n - 3], tops[top_n - 2]))
    half = LANES // 2
    parts = [jnp.zeros((half, tq), F32), jnp.where(forced | (free >= tau), 0.0, -MASK_BIAS)]
    if n_sel < half:
        parts.append(jnp.zeros((half - n_sel, tq), F32))
    q_bias = jnp.concatenate(parts, axis=0).T.astype(CDT)
    Qb = Q + jnp.concatenate([q_bias] * R, axis=0)

    causal_bias = edge_ref[0]
    window_bias = edge_ref[1]

    def step(k_ref, v_ref, kt, carry, bias):
        m, acc = carry
        k0 = pl.multiple_of(kt * tq, tq)
        s = (_dot_nt(Qb, k_ref[0, pl.ds(k0, tq), :]) * scale2).reshape(R, tq, tq)
        if bias is not None:
            s = s + bias[None]
        m_new = jnp.maximum(m, jnp.max(s, axis=2, keepdims=True))
        p = jnp.exp2(s - m_new).reshape(R * tq, tq).astype(CDT)
        pv = _dot(p, v_ref[0, pl.ds(k0, tq), :]).reshape(R, tq, LANES)
        return m_new, acc * jnp.exp2(m - m_new) + pv

    def finish(carry):
        acc = carry[1]
        return acc * (1.0 / acc[:, :, half:half + 1])

    def flag(cond):
        return jnp.where(cond, 1.0, 0.0)

    init = (jnp.full((R, tq, 1), NEG_BIG, F32), jnp.zeros((R, tq, LANES), F32))

    carry = lax.fori_loop(0, i, lambda kt, c: step(ks_ref, vs_ref, kt, c, None), init)
    o_s = finish(lax.fori_loop(i, i + 1, lambda kt, c: step(ks_ref, vs_ref, kt, c, causal_bias), carry))

    n_back = WINDOW // tq

    def win_step(it, c):
        kt = i - n_back + it
        bias = window_bias * flag(it == 0) + causal_bias * flag(it == n_back) + hidden * flag(kt < 0)
        return step(kw_ref, vw_ref, jnp.maximum(kt, 0), c, bias)

    o_w = finish(lax.fori_loop(0, n_back + 1, win_step, init))

    g = g_ref[0]
    o_c = o_c.reshape(R, tq, LANES)
    for r in range(R):
        o = (g[:, 3 * r:3 * r + 1] * o_c[r] + g[:, 3 * r + 1:3 * r + 2] * o_s[r]
             + g[:, 3 * r + 2:3 * r + 3] * o_w[r])
        o_ref[0, :, r * LANES:(r + 1) * LANES] = o.astype(o_ref.dtype)


def _nsa(q, kc, vc, ks, vs, kw, vw, gates, tq=512):
    B, S, _ = q.shape
    G = NSA_GROUPS
    n_sel = S // SEL_BLOCK
    n_cmp = (S - CMP_BLOCK) // CMP_STRIDE + 1
    ncp = kc.shape[1]
    assert n_sel <= LANES // 2 and n_sel % 8 == 0 and ncp % LANES == 0 and WINDOW % tq == 0
    jn = np.arange(LANES)[:, None] * SEL_BLOCK
    cn = np.arange(ncp)[None, :] * CMP_STRIDE
    ovl = ((cn <= jn + SEL_BLOCK - 1) & (cn + CMP_BLOCK - 1 >= jn)
           & (np.arange(LANES)[:, None] < n_sel) & (np.arange(ncp)[None, :] < n_cmp))
    ovl = jnp.asarray(ovl, CDT)
    scale = NSA_HEAD_DIM ** -0.5
    hidden = -MASK_BIAS * scale * LOG2E
    row, col = np.arange(tq)[:, None], np.arange(tq)[None, :]
    edge = jnp.asarray(np.stack([np.where(col <= row, 0.0, hidden), np.where(col > row, 0.0, hidden)]), F32)
    kern = functools.partial(_nsa_kernel, tq=tq, n_sel=n_sel, n_cmp=n_cmp, scale=scale)
    cmp_spec = pl.BlockSpec((1, ncp, LANES), lambda b, g, i: (b * G + g, 0, 0))
    kv_spec = pl.BlockSpec((1, S, LANES), lambda b, g, i: (b, 0, g))
    qo_spec = pl.BlockSpec((1, tq, NSA_HPG * LANES), lambda b, g, i: (b, i, g))
    return pl.pallas_call(
        kern, grid=(B, G, S // tq),
        in_specs=[qo_spec, cmp_spec, cmp_spec, kv_spec, kv_spec, kv_spec, kv_spec,
                  pl.BlockSpec((1, tq, LANES), lambda b, g, i: (b, i, g)),
                  pl.BlockSpec(ovl.shape, lambda b, g, i: (0, 0)),
                  pl.BlockSpec(edge.shape, lambda b, g, i: (0, 0, 0))],
        out_specs=qo_spec,
        out_shape=jax.ShapeDtypeStruct((B, S, NSA_HEADS * LANES), CDT),
        compiler_params=_params("parallel", "parallel", "arbitrary"), name="nsa_attn")(
            q, kc, vc, ks, vs, kw, vw, gates, ovl, edge)


def _merge_kernel(om_ref, on_ref, mg_ref, h_ref, wbm_ref, wbn_ref, wo_ref, g_ref, b_ref, x_ref, xb_ref):
    D = D_MODEL
    y_mla = _dot(om_ref[...], wbm_ref[...])
    y_nsa = _dot(on_ref[...], wbn_ref[...])
    mg = mg_ref[...].astype(F32)
    mixed = _dot((mg[:, :D] * y_mla + mg[:, D:] * y_nsa).astype(CDT), wo_ref[...])
    x = _layer_norm(DEEPNORM_ALPHA * h_ref[...] + mixed, g_ref[...], b_ref[...])
    x_ref[...] = x
    xb_ref[...] = x.astype(CDT)


def _merge(o_mla, o_nsa, mg, h, lw, tm=512):
    T, D = h.shape
    row = lambda n: pl.BlockSpec((tm, n), lambda i: (i, 0))
    full = lambda a: pl.BlockSpec(a.shape, lambda i: (0, 0))
    ws = (lw["w_bm"], lw["w_bn"], lw["w_out"], lw["ln1_g"], lw["ln1_b"])
    return pl.pallas_call(
        _merge_kernel, grid=(T // tm,),
        in_specs=[row(o_mla.shape[1]), row(o_nsa.shape[1]), row(2 * D), row(D)] + [full(a) for a in ws],
        out_specs=[row(D), row(D)],
        out_shape=[jax.ShapeDtypeStruct((T, D), F32), jax.ShapeDtypeStruct((T, D), CDT)],
        compiler_params=_params("parallel"), name="merge_ln1")(o_mla, o_nsa, mg, h, *ws)


def _rows_per_vreg():
    return 8 * (4 // jnp.dtype(CDT).itemsize)


def _pack_words(x):
    if jnp.dtype(CDT).itemsize == 4:
        return pltpu.bitcast(x, jnp.uint32)
    bits = pltpu.bitcast(x.astype(CDT).astype(F32), jnp.uint32)
    return (bits >> 16) | (bits & jnp.uint32(0xFFFF0000))


def _unpack_words(row):
    return pltpu.bitcast(jnp.broadcast_to(row, (8, LANES)), CDT)


_PEER_PAIRS = [(a, b) for a in range(PEER_TOPK) for b in range(PEER_TOPK) if (a + 1) * (b + 1) <= PEER_TOPK]
_N_CAND = -(-len(_PEER_PAIRS) // 8) * 8
HEADS_PER_TRIP = 2


def _top_values(v, want_rank):
    tops = []
    rank = jnp.full(v.shape, 127.0, F32) if want_rank else None
    for r in range(PEER_TOPK):
        m = jnp.max(v, axis=0, keepdims=True)
        tops.append(m)
        eq = v == m
        if want_rank:
            rank = jnp.where(eq, float(r), rank)
        v = jnp.where(eq, -jnp.inf, v)
    return tops, rank


def _peer_stats_kernel(xb_ref, wq_ref, keys_ref, c1_ref, e1_ref, r2_ref, e2_ref, s_ref, cand_ref):
    qp = _dot(xb_ref[...], wq_ref[...]).astype(CDT)
    s_ref[...] = _dot_nt(keys_ref[...], qp)
    n = PEER_N_KEYS
    cand_ref[...] = jnp.full(cand_ref.shape, -jnp.inf, F32)

    def one_head(h, cand_ref):
        s1 = s_ref[pl.ds(pl.multiple_of(2 * h * n, n), n), :]
        s2 = s_ref[pl.ds(pl.multiple_of((2 * h + 1) * n, n), n), :]
        t1, _ = _top_values(s1, False)
        t2, rank2 = _top_values(s2, True)
        for c, (a, b) in enumerate(_PEER_PAIRS):
            cand_ref[c:c + 1, :] = t1[a] + t2[b]
        cand = cand_ref[...]
        tc, _ = _top_values(cand, False)
        tau = tc[PEER_TOPK - 1]
        top = t1[0] + t2[0]
        z = jnp.sum(jnp.where(cand >= tau, jnp.exp(cand - top), 0.0), axis=0, keepdims=True)
        cnt = jnp.zeros(s1.shape, F32)
        for b in range(PEER_TOPK):
            cnt = cnt + jnp.where(s1 + t2[b] >= tau, 1.0, 0.0)
        c1_ref[h] = _pack_words(cnt)
        e1_ref[h] = _pack_words(jnp.exp(s1 - t1[0]))
        r2_ref[h] = rank2.astype(r2_ref.dtype)
        e2_ref[h] = (jnp.exp(s2 - t2[0]) / z).astype(e2_ref.dtype)

    def heads(idx, carry):
        for k in range(HEADS_PER_TRIP):
            one_head(idx * HEADS_PER_TRIP + k, cand_ref.at[k])
        return carry

    lax.fori_loop(0, PEER_HEADS // HEADS_PER_TRIP, heads, 0)


def _peer_stats(xb, wq, keys_bd_t, tm=256):
    T, D = xb.shape
    H, n = PEER_HEADS, PEER_N_KEYS
    full = lambda a: pl.BlockSpec(a.shape, lambda i: (0, 0))
    o_spec = pl.BlockSpec((H, n, tm), lambda i: (0, 0, i))
    o_u32 = jax.ShapeDtypeStruct((H, n, T), jnp.uint32)
    o_cdt = jax.ShapeDtypeStruct((H, n, T), CDT)
    return pl.pallas_call(
        _peer_stats_kernel, grid=(T // tm,),
        in_specs=[pl.BlockSpec((tm, D), lambda i: (i, 0)), full(wq), full(keys_bd_t)],
        out_specs=[o_spec] * 4, out_shape=[o_u32, o_u32, o_cdt, o_cdt],
        scratch_shapes=[pltpu.VMEM((2 * H * n, tm), F32), pltpu.VMEM((HEADS_PER_TRIP, _N_CAND, tm), F32)],
        compiler_params=_params("parallel"), name="peer_stats")(xb, wq, keys_bd_t)


def _peer_dense_kernel(xb_ref, x_ref, u_ref, vt_ref, c1_ref, e1_ref, r2_in_ref, e2_in_ref, g_ref, b_ref,
                       o_ref, ob_ref, acc_ref, a_ref, h_ref, r2_ref, e2_ref, *, tm, te):
    j = pl.program_id(1)
    n = PEER_N_KEYS
    rows_per_tile = te // n

    @pl.when(j == 0)
    def _():
        acc_ref[...] = jnp.zeros(acc_ref.shape, F32)
        r2_ref[...] = r2_in_ref[...]
        e2_ref[...] = e2_in_ref[...]

    a_ref[...] = _gelu(_dot_nt(u_ref[...], xb_ref[...])).astype(CDT)
    ic = n // 2
    sub = _rows_per_vreg()
    nv = ic // sub
    zero = jnp.zeros((nv, sub, LANES), CDT)
    n_i2 = n // ic

    def gate_block(idx):
        sl = pl.ds((idx // n_i2) * LANES, LANES)
        i2 = (idx % n_i2) * ic
        ws = [zero] * rows_per_tile
        for h in range(PEER_HEADS):
            r2 = r2_ref[h, pl.ds(i2, ic), sl].reshape(nv, sub, LANES)
            e2 = e2_ref[h, pl.ds(i2, ic), sl].reshape(nv, sub, LANES)
            for r in range(rows_per_tile):
                c1 = _unpack_words(c1_ref[h, r:r + 1, sl])[None]
                e1 = _unpack_words(e1_ref[h, r:r + 1, sl])[None]
                ws[r] = ws[r] + jnp.maximum(jnp.minimum(e1 * e2, c1 - r2), zero)
        for r in range(rows_per_tile):
            rows = pl.ds(r * n + i2, ic)
            h_ref[rows, sl] = ws[r].reshape(ic, LANES) * a_ref[rows, sl]

    for idx in range((tm // LANES) * n_i2):
        gate_block(idx)
    acc_ref[...] += _dot(vt_ref[...], h_ref[...])

    @pl.when(j == pl.num_programs(1) - 1)
    def _():
        z = DEEPNORM_ALPHA * x_ref[...] + acc_ref[...].T
        out = _layer_norm(z, g_ref[...], b_ref[...])
        o_ref[...] = out
        ob_ref[...] = out.astype(CDT)


def _peer_dense(xb, x, u, vt, stats, ln_g, ln_b, tm=512, te=1024):
    T, D = x.shape
    E = u.shape[0]
    H, n = PEER_HEADS, PEER_N_KEYS
    row = pl.BlockSpec((tm, D), lambda i, j: (i, 0))
    st = pl.BlockSpec((H, n, tm), lambda i, j: (0, 0, i))
    st1 = pl.BlockSpec((H, te // n, tm), lambda i, j: (0, j, i))
    vec = pl.BlockSpec((1, D), lambda i, j: (0, 0))
    assert (te // n) % 8 == 0
    kern = functools.partial(_peer_dense_kernel, tm=tm, te=te)
    return pl.pallas_call(
        kern, grid=(T // tm, E // te),
        in_specs=[row, row, pl.BlockSpec((te, D), lambda i, j: (j, 0)), pl.BlockSpec((D, te), lambda i, j: (0, j)),
                  st1, st1, st, st, vec, vec],
        out_specs=[row, row],
        out_shape=[jax.ShapeDtypeStruct((T, D), F32), jax.ShapeDtypeStruct((T, D), CDT)],
        scratch_shapes=[pltpu.VMEM((D, tm), F32), pltpu.VMEM((te, tm), CDT), pltpu.VMEM((te, tm), CDT),
                        pltpu.VMEM((H, n, tm), CDT), pltpu.VMEM((H, n, tm), CDT)],
        compiler_params=_params("parallel", "arbitrary"), name="peer_dense")(
            xb, x, u, vt, *stats, ln_g, ln_b)


def _rot_cols(w, dim):
    k, n = w.shape
    w4 = w.reshape(k, n // dim, 2, dim // 2)
    return jnp.stack([-w4[:, :, 1], w4[:, :, 0]], axis=2).reshape(k, n)


def _pad_heads(w, dim):
    k, n = w.shape
    w3 = w.reshape(k, n // dim, dim)
    return jnp.pad(w3, ((0, 0), (0, 0), (0, LANES - dim))).reshape(k, (n // dim) * LANES)


def _pad_cols(w, n):
    return jnp.pad(w, ((0, 0), (0, n - w.shape[1])))


def _layer_weights(l, w_in, mla_q_norm, mla_w_uq, mla_kv_norm, mla_w_ukv, w_branch_mla, w_branch_nsa, w_out,
                   ln1_g, ln1_b, peer_w_query, peer_sub_keys, peer_u, peer_v, ln2_g, ln2_b):
    off = np.cumsum((0,) + IN_WIDTHS)
    wi = w_in[l]
    seg = lambda k: wi[:, off[k]:off[k + 1]]
    G, dk = NSA_GROUPS, NSA_HEAD_DIM
    kv = seg(4).reshape(D_MODEL, 6, G * dk)
    lw = {}
    lw["w_cqkv"] = jnp.concatenate([seg(0), seg(1)], axis=1).astype(CDT)
    kr = seg(2)
    lw["w_kr"] = _pad_cols(kr, LANES).astype(CDT)
    lw["w_kr_rot"] = _pad_cols(_rot_cols(kr, MLA_ROPE_DIM), LANES).astype(CDT)
    lw["w_nq"] = _pad_heads(seg(3), dk).astype(CDT)
    lw["w_nq_rot"] = _pad_heads(_rot_cols(seg(3), dk), dk).astype(CDT)
    lw["w_kcmp"] = kv[:, 0].astype(CDT)
    lw["w_kcmp_rot"] = _rot_cols(kv[:, 0], dk).astype(CDT)
    for name, idx in (("ksel", 2), ("kwin", 4)):
        lw["w_" + name] = _pad_heads(kv[:, idx], dk).astype(CDT)
        lw["w_" + name + "_rot"] = _pad_heads(_rot_cols(kv[:, idx], dk), dk).astype(CDT)
    lw["w_vcmp"] = kv[:, 1].astype(CDT)
    lw["w_vsel"] = _pad_heads(kv[:, 3], dk).astype(CDT)
    lw["w_vwin"] = _pad_heads(kv[:, 5], dk).astype(CDT)
    ng = seg(5).reshape(D_MODEL, G, NSA_HPG * 3)
    lw["w_ngate"] = jnp.pad(ng, ((0, 0), (0, 0), (0, LANES - NSA_HPG * 3))).reshape(D_MODEL, G * LANES).astype(CDT)
    lw["w_mgate"] = seg(6).astype(CDT)

    H = MLA_HEADS
    dq = MLA_NOPE_DIM + MLA_ROPE_DIM
    uq = mla_w_uq[l].reshape(MLA_Q_RANK, H, dq)
    lw["wqn"] = _pad_heads(uq[:, :, :MLA_NOPE_DIM].reshape(MLA_Q_RANK, -1), MLA_NOPE_DIM).astype(CDT)
    wqr = uq[:, :, MLA_NOPE_DIM:].reshape(MLA_Q_RANK, H * MLA_ROPE_DIM)
    lw["wqr"] = wqr.astype(CDT)
    lw["wqr_rot"] = _rot_cols(wqr, MLA_ROPE_DIM).astype(CDT)
    pq = np.zeros((H * MLA_ROPE_DIM, H * LANES), np.float32)
    pk = np.zeros((LANES, H * LANES), np.float32)
    for h in range(H):
        for j in range(MLA_ROPE_DIM):
            pq[h * MLA_ROPE_DIM + j, h * LANES + MLA_NOPE_DIM + j] = 1.0
            pk[j, h * LANES + MLA_NOPE_DIM + j] = 1.0
    lw["place_q"] = jnp.asarray(pq, CDT)
    lw["place_k"] = jnp.asarray(pk, CDT)
    ukv = mla_w_ukv[l].reshape(MLA_KV_RANK, H, MLA_NOPE_DIM + MLA_V_DIM)
    lw["wkn"] = _pad_heads(ukv[:, :, :MLA_NOPE_DIM].reshape(MLA_KV_RANK, -1), MLA_NOPE_DIM).astype(CDT)
    lw["wv"] = ukv[:, :, MLA_NOPE_DIM:].reshape(MLA_KV_RANK, H * MLA_V_DIM).astype(CDT)
    lw["q_gain"] = mla_q_norm[l].reshape(1, -1)
    lw["kv_gain"] = mla_kv_norm[l].reshape(1, -1)

    lw["w_bm"] = w_branch_mla[l].astype(CDT)
    wbn = w_branch_nsa[l].reshape(NSA_HEADS, dk, D_MODEL)
    lw["w_bn"] = jnp.pad(wbn, ((0, 0), (0, LANES - dk), (0, 0))).reshape(NSA_HEADS * LANES, D_MODEL).astype(CDT)
    lw["w_out"] = w_out[l].astype(CDT)
    lw["ln1_g"] = ln1_g[l].reshape(1, -1)
    lw["ln1_b"] = ln1_b[l].reshape(1, -1)

    lw["w_pq"] = peer_w_query[l].astype(CDT)
    sk = peer_sub_keys[l].reshape(PEER_HEADS * 2, PEER_N_KEYS, PEER_KEY_DIM // 2)
    eye = jnp.eye(PEER_HEADS * 2, dtype=sk.dtype)
    lw["keys_bd_t"] = jnp.einsum("gnd,gf->gnfd", sk, eye).reshape(
        PEER_HEADS * 2 * PEER_N_KEYS, PEER_HEADS * PEER_KEY_DIM).astype(CDT)
    lw["u"] = peer_u[l].astype(CDT)
    lw["vt"] = peer_v[l].T.astype(CDT)
    lw["ln2_g"] = ln2_g[l].reshape(1, -1)
    lw["ln2_b"] = ln2_b[l].reshape(1, -1)
    return lw


def _rope_tables(positions, dim):
    inv_freq = ROPE_THETA ** (-jnp.arange(0, dim, 2, dtype=F32) / dim)
    ang = positions.astype(F32).reshape(-1, 1) * inv_freq
    rep = lambda t: jnp.tile(jnp.concatenate([t, t], axis=-1), (1, LANES // dim))
    return rep(jnp.cos(ang)), rep(jnp.sin(ang))


def _hybrid_layer(h, hb, B, S, lw, cmp_w, tab_mla, tab_nsa):
    T = B * S
    G, dk = NSA_GROUPS, NSA_HEAD_DIM
    lane = jnp.arange(G * LANES) % LANES - LANES // 2
    blk_onehot = (jnp.arange(S)[:, None] // SEL_BLOCK == lane[None, :]).astype(F32)
    ones_lane = np.zeros((1, G * LANES), np.float32)
    ones_lane[0, [g * LANES + dk for g in range(G)]] = 1.0
    p = _proj_all(hb, lw, tab_mla, tab_nsa, blk_onehot, jnp.asarray(ones_lane))
    cqkv, kr, nq, kcmp, ksel, kwin = p["cqkv"], p["kr"], p["nq"], p["kcmp"], p["ksel"], p["kwin"]
    vcmp, vsel, vwin, ngate, mgate = p["vcmp"], p["vsel"], p["vwin"], p["ngate"], p["mgate"]

    q, k, v = _mla_up(cqkv, kr, lw, tab_mla)
    o_mla = _mla_attn(q.reshape(B, S, -1), k.reshape(B, S, -1), v.reshape(B, S, -1))

    def chunks(a):
        return a.reshape(B, S, G, dk).transpose(0, 2, 1, 3).reshape(B * G, S // CMP_STRIDE, CMP_STRIDE * dk)

    kc = _compress(chunks(kcmp), cmp_w["k_pe"], cmp_w["k_w1"], cmp_w["k_w2"])
    vc = _compress(chunks(vcmp), cmp_w["v_pe"], cmp_w["v_w1"], cmp_w["v_w2"])
    r3 = lambda a: a.reshape(B, S, -1)
    o_nsa = _nsa(r3(nq), kc, vc, r3(ksel), r3(vsel), r3(kwin), r3(vwin), r3(ngate))

    x1, x1b = _merge(o_mla.reshape(T, -1), o_nsa.reshape(T, -1), mgate, h, lw)
    stats = _peer_stats(x1b, lw["w_pq"], lw["keys_bd_t"])
    return _peer_dense(x1b, x1, lw["u"], lw["vt"], stats, lw["ln2_g"], lw["ln2_b"])


def kernel(x, positions, ln_in_g, ln_in_b, w_in, mla_q_norm, mla_w_uq, mla_kv_norm, mla_w_ukv, nsa_cmp_k_pe, nsa_cmp_k_w1, nsa_cmp_k_w2, nsa_cmp_v_pe, nsa_cmp_v_w1, nsa_cmp_v_w2, w_branch_mla, w_branch_nsa, w_out, ln1_g, ln1_b, peer_w_query, peer_sub_keys, peer_u, peer_v, ln2_g, ln2_b):
    B, S, D = x.shape
    tab_mla = _rope_tables(positions, MLA_ROPE_DIM)
    tab_nsa = _rope_tables(positions, NSA_HEAD_DIM)
    h, hb = _ln_in(x.reshape(B * S, D), ln_in_g, ln_in_b)
    for l in range(DEPTH):
        lw = _layer_weights(l, w_in, mla_q_norm, mla_w_uq, mla_kv_norm, mla_w_ukv, w_branch_mla, w_branch_nsa,
                            w_out, ln1_g, ln1_b, peer_w_query, peer_sub_keys, peer_u, peer_v, ln2_g, ln2_b)
        cmp_w = {"k_pe": nsa_cmp_k_pe[l], "k_w1": nsa_cmp_k_w1[l], "k_w2": nsa_cmp_k_w2[l],
                 "v_pe": nsa_cmp_v_pe[l], "v_w1": nsa_cmp_v_w1[l], "v_w2": nsa_cmp_v_w2[l]}
        h, hb = _hybrid_layer(h, hb, B, S, lw, cmp_w, tab_mla, tab_nsa)
    return h.reshape(B, S, D)
```

```python
import functools
import math

import numpy as np
import jax
import jax.numpy as jnp
from jax import lax
from jax.experimental import pallas as pl
from jax.experimental.pallas import tpu as pltpu

D_MODEL = 1024
DEPTH = 2
ROPE_THETA = 10000.0

MLA_HEADS = 8
MLA_NOPE_DIM = 64
MLA_ROPE_DIM = 32
MLA_V_DIM = 64
MLA_Q_RANK = 768
MLA_KV_RANK = 256

NSA_HEADS = 8
NSA_GROUPS = 2
NSA_HPG = NSA_HEADS // NSA_GROUPS
NSA_HEAD_DIM = 64
CMP_BLOCK = 32
CMP_STRIDE = 16
CMP_HIDDEN = 2 * NSA_HEAD_DIM
SEL_BLOCK = 64
SEL_TOPN = 16
WINDOW = 512
FORCE_SCORE = 1e9

PEER_HEADS = 8
PEER_KEY_DIM = 128
PEER_N_KEYS = 128
PEER_N_EXPERTS = PEER_N_KEYS * PEER_N_KEYS
PEER_TOPK = 16

DEEPNORM_ALPHA = (2 * DEPTH) ** 0.25
LN_EPS = 1e-5
RMS_EPS = 1e-6
NEG_BIG = -1e30
LOG2E = math.log2(math.e)
MASK_BIAS = 32768.0

IN_WIDTHS = (MLA_Q_RANK, MLA_KV_RANK, MLA_ROPE_DIM, NSA_HEADS * NSA_HEAD_DIM,
             6 * NSA_GROUPS * NSA_HEAD_DIM, 3 * NSA_HEADS, 2 * D_MODEL)

LANES = 128
VMEM_LIMIT = 56 * 1024 * 1024

F32 = jnp.float32
CDT = jnp.bfloat16

_NT = (((1,), (1,)), ((), ()))


def _dot(a, b):
    return jnp.dot(a, b, preferred_element_type=F32)


def _dot_nt(a, b):
    return lax.dot_general(a, b, _NT, preferred_element_type=F32)


def _params(*sem):
    return pltpu.CompilerParams(dimension_semantics=sem, vmem_limit_bytes=VMEM_LIMIT)


def _gelu(x):
    return 0.5 * x * (1.0 + lax.erf(x * (2.0 ** -0.5)))


def _layer_norm(z, g, b):
    mu = jnp.mean(z, axis=-1, keepdims=True)
    d = z - mu
    var = jnp.mean(d * d, axis=-1, keepdims=True)
    return d * lax.rsqrt(var + LN_EPS) * g + b


def _ln_in_kernel(x_ref, g_ref, b_ref, h_ref, hb_ref):
    h = _layer_norm(x_ref[...], g_ref[...], b_ref[...])
    h_ref[...] = h
    hb_ref[...] = h.astype(CDT)


def _ln_in(x, g, b, tm=512):
    T, D = x.shape
    row = pl.BlockSpec((tm, D), lambda i: (i, 0))
    vec = pl.BlockSpec((1, D), lambda i: (0, 0))
    return pl.pallas_call(
        _ln_in_kernel, grid=(T // tm,), in_specs=[row, vec, vec], out_specs=[row, row],
        out_shape=[jax.ShapeDtypeStruct((T, D), F32), jax.ShapeDtypeStruct((T, D), CDT)],
        compiler_params=_params("parallel"), name="ln_in")(x, g.reshape(1, D), b.reshape(1, D))


_PROJ_CHUNK = 512


def _proj_all_kernel(hb_ref, w_ref, cm_ref, sm_ref, cn_ref, sn_ref, hot_ref, one_ref, *out_refs, plan):
    hb = hb_ref[...]
    tables = {"mla": (cm_ref, sm_ref), "nsa": (cn_ref, sn_ref)}
    for o_ref, (off, width, rot_off, table, act, bias) in zip(out_refs, plan):
        for c0 in range(0, width, _PROJ_CHUNK):
            cw = min(_PROJ_CHUNK, width - c0)
            y = _dot(hb, w_ref[:, off + c0:off + c0 + cw])
            if rot_off is not None:
                yr = _dot(hb, w_ref[:, rot_off + c0:rot_off + c0 + cw])
                cos = jnp.concatenate([tables[table][0][...]] * (cw // LANES), axis=1)
                sin = jnp.concatenate([tables[table][1][...]] * (cw // LANES), axis=1)
                y = y * cos + yr * sin
            if act == "sigmoid":
                y = 1.0 / (1.0 + jnp.exp(-y))
            if bias == "block_onehot":
                y = y + hot_ref[:, c0:c0 + cw]
            elif bias == "ones_lane":
                y = y + one_ref[:, c0:c0 + cw]
            o_ref[:, c0:c0 + cw] = y.astype(o_ref.dtype)


def _proj_all(hb, lw, tab_mla, tab_nsa, blk_onehot, ones_lane, tm=512):
    segs = [("cqkv", lw["w_cqkv"], None, None, None, None, CDT),
            ("kr", lw["w_kr"], lw["w_kr_rot"], "mla", None, None, CDT),
            ("nq", lw["w_nq"], lw["w_nq_rot"], "nsa", None, None, CDT),
            ("kcmp", lw["w_kcmp"], lw["w_kcmp_rot"], "nsa", None, None, CDT),
            ("ksel", lw["w_ksel"], lw["w_ksel_rot"], "nsa", None, "block_onehot", CDT),
            ("kwin", lw["w_kwin"], lw["w_kwin_rot"], "nsa", None, None, CDT),
            ("vcmp", lw["w_vcmp"], None, None, None, None, CDT),
            ("vsel", lw["w_vsel"], None, None, None, "ones_lane", CDT),
            ("vwin", lw["w_vwin"], None, None, None, "ones_lane", CDT),
            ("ngate", lw["w_ngate"], None, None, "sigmoid", None, F32),
            ("mgate", lw["w_mgate"], None, None, "sigmoid", None, CDT)]
    T, K = hb.shape
    cols, plan, off = [], [], 0
    for _, w, w_rot, table, act, bias, _ in segs:
        width = w.shape[1]
        cols.append(w)
        rot_off = None
        if w_rot is not None:
            cols.append(w_rot)
            rot_off = off + width
        plan.append((off, width, rot_off, table, act, bias))
        off += width * (2 if w_rot is not None else 1)
    assert T % tm == 0 and blk_onehot.shape[0] % tm == 0
    w_all = jnp.concatenate(cols, axis=1)
    row = lambda n: pl.BlockSpec((tm, n), lambda i: (i, 0))
    full = lambda a: pl.BlockSpec(a.shape, lambda i: (0, 0))
    outs = pl.pallas_call(
        functools.partial(_proj_all_kernel, plan=tuple(plan)), grid=(T // tm,),
        in_specs=[row(K), full(w_all), row(LANES), row(LANES), row(LANES), row(LANES),
                  pl.BlockSpec((tm, blk_onehot.shape[1]), lambda i: (i % (blk_onehot.shape[0] // tm), 0)),
                  full(ones_lane)],
        out_specs=[row(s[1].shape[1]) for s in segs],
        out_shape=[jax.ShapeDtypeStruct((T, s[1].shape[1]), s[6]) for s in segs],
        compiler_params=_params("parallel"), name="proj_all")(
            hb, w_all, tab_mla[0], tab_mla[1], tab_nsa[0], tab_nsa[1], blk_onehot, ones_lane)
    return dict(zip([s[0] for s in segs], outs))


def _rms(x, g):
    return x * lax.rsqrt(jnp.mean(x * x, axis=-1, keepdims=True) + RMS_EPS) * g


def _mla_up_kernel(c_ref, kr_ref, qg_ref, kvg_ref, wqn_ref, wqr_ref, wqrr_ref, pq_ref,
                   wkn_ref, wv_ref, pk_ref, cos_ref, sin_ref, q_ref, k_ref, v_ref):
    c = c_ref[...].astype(F32)
    cqn = _rms(c[:, :MLA_Q_RANK], qg_ref[...]).astype(CDT)
    ckvn = _rms(c[:, MLA_Q_RANK:], kvg_ref[...]).astype(CDT)
    cos = jnp.concatenate([cos_ref[...]] * 2, axis=1)
    sin = jnp.concatenate([sin_ref[...]] * 2, axis=1)
    roped = (_dot(cqn, wqr_ref[...]) * cos + _dot(cqn, wqrr_ref[...]) * sin).astype(CDT)
    q_ref[...] = (_dot(cqn, wqn_ref[...]) + _dot(roped, pq_ref[...])).astype(CDT)
    k_ref[...] = (_dot(ckvn, wkn_ref[...]) + _dot(kr_ref[...], pk_ref[...])).astype(CDT)
    v_ref[...] = _dot(ckvn, wv_ref[...]).astype(CDT)


def _mla_up(cqkv, kr, lw, tables, tm=512):
    T = cqkv.shape[0]
    row = lambda n: pl.BlockSpec((tm, n), lambda i: (i, 0))
    full = lambda a: pl.BlockSpec(a.shape, lambda i: (0, 0))
    ws = (lw["q_gain"], lw["kv_gain"], lw["wqn"], lw["wqr"], lw["wqr_rot"], lw["place_q"],
          lw["wkn"], lw["wv"], lw["place_k"])
    hp = MLA_HEADS * LANES
    return pl.pallas_call(
        _mla_up_kernel, grid=(T // tm,),
        in_specs=[row(cqkv.shape[1]), row(LANES)] + [full(a) for a in ws] + [row(LANES), row(LANES)],
        out_specs=[row(hp), row(hp), row(MLA_HEADS * MLA_V_DIM)],
        out_shape=[jax.ShapeDtypeStruct((T, hp), CDT), jax.ShapeDtypeStruct((T, hp), CDT),
                   jax.ShapeDtypeStruct((T, MLA_HEADS * MLA_V_DIM), CDT)],
        compiler_params=_params("parallel"), name="mla_up")(cqkv, kr, *ws, tables[0], tables[1])


def _mla_attn_kernel(q_ref, k_ref, v_ref, o_ref, *, tq, scale):
    i = pl.program_id(2)
    qs = [q_ref[0, :, hd * LANES:(hd + 1) * LANES] for hd in range(2)]

    def step(kt, carries, diagonal):
        k0 = pl.multiple_of(kt * tq, tq)
        v = v_ref[0, pl.ds(k0, tq), :]
        new = []
        for hd, (m, l, acc) in enumerate(carries):
            k = k_ref[0, pl.ds(k0, tq), hd * LANES:(hd + 1) * LANES]
            s = _dot_nt(qs[hd], k) * scale
            if diagonal:
                row = lax.broadcasted_iota(jnp.int32, (tq, tq), 0)
                col = lax.broadcasted_iota(jnp.int32, (tq, tq), 1)
                s = jnp.where(col <= row, s, NEG_BIG)
            m_new = jnp.maximum(m, jnp.max(s, axis=1, keepdims=True))
            alpha = jnp.exp(m - m_new)
            p = jnp.exp(s - m_new)
            l = l * alpha + jnp.sum(p, axis=1, keepdims=True)
            acc = acc * alpha + _dot(p.astype(CDT), v)
            new.append((m_new, l, acc))
        return tuple(new)

    one = (jnp.full((tq, 1), NEG_BIG, F32), jnp.zeros((tq, 1), F32), jnp.zeros((tq, LANES), F32))
    carries = lax.fori_loop(0, i, lambda kt, c: step(kt, c, False), (one, one))
    outs = [acc / l for _, l, acc in step(i, carries, True)]
    lane = lax.broadcasted_iota(jnp.int32, (tq, LANES), 1)
    o_ref[0] = jnp.where(lane < MLA_V_DIM, outs[0], outs[1]).astype(o_ref.dtype)


def _mla_attn(q, k, v, tq=512):
    B, S, _ = q.shape
    scale = (MLA_NOPE_DIM + MLA_ROPE_DIM) ** -0.5
    kern = functools.partial(_mla_attn_kernel, tq=tq, scale=scale)
    return pl.pallas_call(
        kern, grid=(B, MLA_HEADS // 2, S // tq),
        in_specs=[pl.BlockSpec((1, tq, 2 * LANES), lambda b, h, i: (b, i, h)),
                  pl.BlockSpec((1, S, 2 * LANES), lambda b, h, i: (b, 0, h)),
                  pl.BlockSpec((1, S, LANES), lambda b, h, i: (b, 0, h))],
        out_specs=pl.BlockSpec((1, tq, LANES), lambda b, h, i: (b, i, h)),
        out_shape=jax.ShapeDtypeStruct((B, S, MLA_HEADS * MLA_V_DIM), CDT),
        compiler_params=_params("parallel", "parallel", "arbitrary"), name="mla_attn")(q, k, v)


def _compress_kernel(c_ref, pe_ref, w1a_ref, w1b_ref, w2_ref, o_ref, *, n_valid):
    c = c_ref[0].astype(F32)
    n = c.shape[0]
    a = _dot((c + pe_ref[0:1, :]).astype(CDT), w1a_ref[...])
    b = _dot((c + pe_ref[1:2, :]).astype(CDT), w1b_ref[...])
    hid = _gelu(a + pltpu.roll(b, n - 1, 0))
    out = _dot(hid.astype(CDT), w2_ref[...])
    row = lax.broadcasted_iota(jnp.int32, out.shape, 0)
    o_ref[0] = jnp.where(row < n_valid, out, 0.0).astype(o_ref.dtype)


def _compress(chunks, pe, w1, w2):
    BG, n, width = chunks.shape
    pe2 = pe.reshape(2, width).astype(F32)
    w1a = w1[:width].astype(CDT)
    w1b = w1[width:].astype(CDT)
    w2p = jnp.pad(w2, ((0, 0), (0, LANES - w2.shape[1]))).astype(CDT)
    full = lambda a: pl.BlockSpec(a.shape, lambda i: (0, 0))
    return pl.pallas_call(
        functools.partial(_compress_kernel, n_valid=n - 1), grid=(BG,),
        in_specs=[pl.BlockSpec((1, n, width), lambda i: (i, 0, 0)), full(pe2), full(w1a), full(w1b), full(w2p)],
        out_specs=pl.BlockSpec((1, n, LANES), lambda i: (i, 0, 0)),
        out_shape=jax.ShapeDtypeStruct((BG, n, LANES), CDT),
        compiler_params=_params("parallel"), name="nsa_compress")(chunks, pe2, w1a, w1b, w2p)


def _nsa_kernel(q_ref, kc_ref, vc_ref, ks_ref, vs_ref, kw_ref, vw_ref, g_ref, ovl_ref, edge_ref, o_ref,
                *, tq, n_sel, n_cmp, scale):
    i = pl.program_id(2)
    q0 = i * tq
    R = NSA_HPG
    ncp = kc_ref.shape[1]
    Q = jnp.concatenate([q_ref[0, :, r * LANES:(r + 1) * LANES] for r in range(R)], axis=0)

    scale2 = scale * LOG2E
    hidden = -MASK_BIAS * scale2
    t_c = q0 + lax.broadcasted_iota(jnp.int32, (tq, ncp), 0)
    n_c = lax.broadcasted_iota(jnp.int32, (tq, ncp), 1)
    bias_c = jnp.where((n_c * CMP_STRIDE + (CMP_BLOCK - 1) <= t_c) & (n_c < n_cmp), 0.0, hidden)
    s = (_dot_nt(Q, kc_ref[0]) * scale2).reshape(R, tq, ncp) + bias_c[None]
    p = jnp.exp2(s - jnp.max(s, axis=2, keepdims=True))
    t_q = q0 + lax.broadcasted_iota(jnp.int32, (tq, 1), 0)
    sees_any = jnp.where(t_q >= CMP_BLOCK - 1, 1.0, 0.0)
    p = p * (sees_any / jnp.sum(p, axis=2, keepdims=True))
    o_c = _dot(p.reshape(R * tq, ncp).astype(CDT), vc_ref[0])

    psum = jnp.sum(p, axis=0)
    p_hi = psum.astype(CDT)
    p_lo = (psum - p_hi.astype(F32)).astype(CDT)
    imp = (_dot_nt(ovl_ref[...], p_hi) + _dot_nt(ovl_ref[...], p_lo))[0:n_sel]
    jj = lax.broadcasted_iota(jnp.int32, (n_sel, tq), 0)
    cur = (q0 + lax.broadcasted_iota(jnp.int32, (n_sel, tq), 1)) // SEL_BLOCK
    forced = (jj == 0) | (jj == cur) | (jj == cur - 1)
    top_n = min(SEL_TOPN, n_sel)
    assert top_n > 3
    free = jnp.where(forced | (jj > cur), -FORCE_SCORE, imp)
    tops, v = [], free
    for _ in range(top_n - 1):
        m = jnp.max(v, axis=0, keepdims=True)
        tops.append(m)
        v = jnp.where(v == m, -jnp.inf, v)
    cur_row = cur[0:1, :]
    tau = jnp.where(cur_row >= 2, tops[top_n - 4], jnp.where(cur_row == 1, tops[top_n - 3], tops[top_n - 2]))
    half = LANES // 2
    parts = [jnp.zeros((half, tq), F32), jnp.where(forced | (free >= tau), 0.0, -MASK_BIAS)]
    if n_sel < half:
        parts.append(jnp.zeros((half - n_sel, tq), F32))
    q_bias = jnp.concatenate(parts, axis=0).T.astype(CDT)
    Qb = Q + jnp.concatenate([q_bias] * R, axis=0)

    causal_bias = edge_ref[0]
    window_bias = edge_ref[1]

    def step(k_ref, v_ref, kt, carry, bias):
        m, acc = carry
        k0 = pl.multiple_of(kt * tq, tq)
        s = (_dot_nt(Qb, k_ref[0, pl.ds(k0, tq), :]) * scale2).reshape(R, tq, tq)
        if bias is not None:
            s = s + bias[None]
        m_new = jnp.maximum(m, jnp.max(s, axis=2, keepdims=True))
        p = jnp.exp2(s - m_new).reshape(R * tq, tq).astype(CDT)
        pv = _dot(p, v_ref[0, pl.ds(k0, tq), :]).reshape(R, tq, LANES)
        return m_new, acc * jnp.exp2(m - m_new) + pv

    def finish(carry):
        acc = carry[1]
        return acc * (1.0 / acc[:, :, half:half + 1])

    def flag(cond):
        return jnp.where(cond, 1.0, 0.0)

    init = (jnp.full((R, tq, 1), NEG_BIG, F32), jnp.zeros((R, tq, LANES), F32))

    carry = lax.fori_loop(0, i, lambda kt, c: step(ks_ref, vs_ref, kt, c, None), init)
    o_s = finish(lax.fori_loop(i, i + 1, lambda kt, c: step(ks_ref, vs_ref, kt, c, causal_bias), carry))

    n_back = WINDOW // tq

    def win_step(it, c):
        kt = i - n_back + it
        bias = window_bias * flag(it == 0) + causal_bias * flag(it == n_back) + hidden * flag(kt < 0)
        return step(kw_ref, vw_ref, jnp.maximum(kt, 0), c, bias)

    o_w = finish(lax.fori_loop(0, n_back + 1, win_step, init))

    g = g_ref[0]
    o_c = o_c.reshape(R, tq, LANES)
    for r in range(R):
        o = (g[:, 3 * r:3 * r + 1] * o_c[r] + g[:, 3 * r + 1:3 * r + 2] * o_s[r]
             + g[:, 3 * r + 2:3 * r + 3] * o_w[r])
        o_ref[0, :, r * LANES:(r + 1) * LANES] = o.astype(o_ref.dtype)


def _nsa(q, kc, vc, ks, vs, kw, vw, gates, tq=512):
    B, S, _ = q.shape
    G = NSA_GROUPS
    n_sel = S // SEL_BLOCK
    n_cmp = (S - CMP_BLOCK) // CMP_STRIDE + 1
    ncp = kc.shape[1]
    assert n_sel <= LANES // 2 and n_sel % 8 == 0 and ncp % LANES == 0 and WINDOW % tq == 0
    jn = np.arange(LANES)[:, None] * SEL_BLOCK
    cn = np.arange(ncp)[None, :] * CMP_STRIDE
    ovl = ((cn <= jn + SEL_BLOCK - 1) & (cn + CMP_BLOCK - 1 >= jn)
           & (np.arange(LANES)[:, None] < n_sel) & (np.arange(ncp)[None, :] < n_cmp))
    ovl = jnp.asarray(ovl, CDT)
    scale = NSA_HEAD_DIM ** -0.5
    hidden = -MASK_BIAS * scale * LOG2E
    row, col = np.arange(tq)[:, None], np.arange(tq)[None, :]
    edge = jnp.asarray(np.stack([np.where(col <= row, 0.0, hidden), np.where(col > row, 0.0, hidden)]), F32)
    kern = functools.partial(_nsa_kernel, tq=tq, n_sel=n_sel, n_cmp=n_cmp, scale=scale)
    cmp_spec = pl.BlockSpec((1, ncp, LANES), lambda b, g, i: (b * G + g, 0, 0))
    kv_spec = pl.BlockSpec((1, S, LANES), lambda b, g, i: (b, 0, g))
    qo_spec = pl.BlockSpec((1, tq, NSA_HPG * LANES), lambda b, g, i: (b, i, g))
    return pl.pallas_call(
        kern, grid=(B, G, S // tq),
        in_specs=[qo_spec, cmp_spec, cmp_spec, kv_spec, kv_spec, kv_spec, kv_spec,
                  pl.BlockSpec((1, tq, LANES), lambda b, g, i: (b, i, g)),
                  pl.BlockSpec(ovl.shape, lambda b, g, i: (0, 0)),
                  pl.BlockSpec(edge.shape, lambda b, g, i: (0, 0, 0))],
        out_specs=qo_spec,
        out_shape=jax.ShapeDtypeStruct((B, S, NSA_HEADS * LANES), CDT),
        compiler_params=_params("parallel", "parallel", "arbitrary"), name="nsa_attn")(
            q, kc, vc, ks, vs, kw, vw, gates, ovl, edge)


def _merge_kernel(om_ref, on_ref, mg_ref, h_ref, wbm_ref, wbn_ref, wo_ref, g_ref, b_ref, x_ref, xb_ref):
    D = D_MODEL
    y_mla = _dot(om_ref[...], wbm_ref[...])
    y_nsa = _dot(on_ref[...], wbn_ref[...])
    mg = mg_ref[...].astype(F32)
    mixed = _dot((mg[:, :D] * y_mla + mg[:, D:] * y_nsa).astype(CDT), wo_ref[...])
    x = _layer_norm(DEEPNORM_ALPHA * h_ref[...] + mixed, g_ref[...], b_ref[...])
    x_ref[...] = x
    xb_ref[...] = x.astype(CDT)


def _merge(o_mla, o_nsa, mg, h, lw, tm=512):
    T, D = h.shape
    row = lambda n: pl.BlockSpec((tm, n), lambda i: (i, 0))
    full = lambda a: pl.BlockSpec(a.shape, lambda i: (0, 0))
    ws = (lw["w_bm"], lw["w_bn"], lw["w_out"], lw["ln1_g"], lw["ln1_b"])
    return pl.pallas_call(
        _merge_kernel, grid=(T // tm,),
        in_specs=[row(o_mla.shape[1]), row(o_nsa.shape[1]), row(2 * D), row(D)] + [full(a) for a in ws],
        out_specs=[row(D), row(D)],
        out_shape=[jax.ShapeDtypeStruct((T, D), F32), jax.ShapeDtypeStruct((T, D), CDT)],
        compiler_params=_params("parallel"), name="merge_ln1")(o_mla, o_nsa, mg, h, *ws)


def _rows_per_vreg():
    return 8 * (4 // jnp.dtype(CDT).itemsize)


def _pack_words(x):
    if jnp.dtype(CDT).itemsize == 4:
        return pltpu.bitcast(x, jnp.uint32)
    bits = pltpu.bitcast(x.astype(CDT).astype(F32), jnp.uint32)
    return (bits >> 16) | (bits & jnp.uint32(0xFFFF0000))


def _unpack_words(row):
    return pltpu.bitcast(jnp.broadcast_to(row, (8, LANES)), CDT)


_PEER_PAIRS = [(a, b) for a in range(PEER_TOPK) for b in range(PEER_TOPK) if (a + 1) * (b + 1) <= PEER_TOPK]
_N_CAND = -(-len(_PEER_PAIRS) // 8) * 8
HEADS_PER_TRIP = 2


def _top_values(v, want_rank):
    tops = []
    rank = jnp.full(v.shape, 127.0, F32) if want_rank else None
    for r in range(PEER_TOPK):
        m = jnp.max(v, axis=0, keepdims=True)
        tops.append(m)
        eq = v == m
        if want_rank:
            rank = jnp.where(eq, float(r), rank)
        v = jnp.where(eq, -jnp.inf, v)
    return tops, rank


def _peer_stats_kernel(xb_ref, wq_ref, keys_ref, c1_ref, e1_ref, r2_ref, e2_ref, s_ref, cand_ref):
    qp = _dot(xb_ref[...], wq_ref[...]).astype(CDT)
    s_ref[...] = _dot_nt(keys_ref[...], qp)
    n = PEER_N_KEYS
    cand_ref[...] = jnp.full(cand_ref.shape, -jnp.inf, F32)

    def one_head(h, cand_ref):
        s1 = s_ref[pl.ds(pl.multiple_of(2 * h * n, n), n), :]
        s2 = s_ref[pl.ds(pl.multiple_of((2 * h + 1) * n, n), n), :]
        t1, _ = _top_values(s1, False)
        t2, rank2 = _top_values(s2, True)
        for c, (a, b) in enumerate(_PEER_PAIRS):
            cand_ref[c:c + 1, :] = t1[a] + t2[b]
        cand = cand_ref[...]
        tc, _ = _top_values(cand, False)
        tau = tc[PEER_TOPK - 1]
        top = t1[0] + t2[0]
        z = jnp.sum(jnp.where(cand >= tau, jnp.exp(cand - top), 0.0), axis=0, keepdims=True)
        cnt = jnp.zeros(s1.shape, F32)
        for b in range(PEER_TOPK):
            cnt = cnt + jnp.where(s1 + t2[b] >= tau, 1.0, 0.0)
        c1_ref[h] = _pack_words(cnt)
        e1_ref[h] = _pack_words(jnp.exp(s1 - t1[0]))
        r2_ref[h] = rank2.astype(r2_ref.dtype)
        e2_ref[h] = (jnp.exp(s2 - t2[0]) / z).astype(e2_ref.dtype)

    def heads(idx, carry):
        for k in range(HEADS_PER_TRIP):
            one_head(idx * HEADS_PER_TRIP + k, cand_ref.at[k])
        return carry

    lax.fori_loop(0, PEER_HEADS // HEADS_PER_TRIP, heads, 0)


def _peer_stats(xb, wq, keys_bd_t, tm=256):
    T, D = xb.shape
    H, n = PEER_HEADS, PEER_N_KEYS
    full = lambda a: pl.BlockSpec(a.shape, lambda i: (0, 0))
    o_spec = pl.BlockSpec((H, n, tm), lambda i: (0, 0, i))
    o_u32 = jax.ShapeDtypeStruct((H, n, T), jnp.uint32)
    o_cdt = jax.ShapeDtypeStruct((H, n, T), CDT)
    return pl.pallas_call(
        _peer_stats_kernel, grid=(T // tm,),
        in_specs=[pl.BlockSpec((tm, D), lambda i: (i, 0)), full(wq), full(keys_bd_t)],
        out_specs=[o_spec] * 4, out_shape=[o_u32, o_u32, o_cdt, o_cdt],
        scratch_shapes=[pltpu.VMEM((2 * H * n, tm), F32), pltpu.VMEM((HEADS_PER_TRIP, _N_CAND, tm), F32)],
        compiler_params=_params("parallel"), name="peer_stats")(xb, wq, keys_bd_t)


def _peer_dense_kernel(xb_ref, x_ref, u_ref, vt_ref, c1_ref, e1_ref, r2_in_ref, e2_in_ref, g_ref, b_ref,
                       o_ref, ob_ref, acc_ref, a_ref, h_ref, r2_ref, e2_ref, *, tm, te):
    j = pl.program_id(1)
    n = PEER_N_KEYS
    rows_per_tile = te // n

    @pl.when(j == 0)
    def _():
        acc_ref[...] = jnp.zeros(acc_ref.shape, F32)
        r2_ref[...] = r2_in_ref[...]
        e2_ref[...] = e2_in_ref[...]

    a_ref[...] = _gelu(_dot_nt(u_ref[...], xb_ref[...])).astype(CDT)
    ic = n // 2
    sub = _rows_per_vreg()
    nv = ic // sub
    zero = jnp.zeros((nv, sub, LANES), CDT)
    n_i2 = n // ic

    def gate_block(idx):
        sl = pl.ds((idx // n_i2) * LANES, LANES)
        i2 = (idx % n_i2) * ic
        ws = [zero] * rows_per_tile
        for h in range(PEER_HEADS):
            r2 = r2_ref[h, pl.ds(i2, ic), sl].reshape(nv, sub, LANES)
            e2 = e2_ref[h, pl.ds(i2, ic), sl].reshape(nv, sub, LANES)
            for r in range(rows_per_tile):
                c1 = _unpack_words(c1_ref[h, r:r + 1, sl])[None]
                e1 = _unpack_words(e1_ref[h, r:r + 1, sl])[None]
                ws[r] = ws[r] + jnp.maximum(jnp.minimum(e1 * e2, c1 - r2), zero)
        for r in range(rows_per_tile):
            rows = pl.ds(r * n + i2, ic)
            h_ref[rows, sl] = ws[r].reshape(ic, LANES) * a_ref[rows, sl]

    for idx in range((tm // LANES) * n_i2):
        gate_block(idx)
    acc_ref[...] += _dot(vt_ref[...], h_ref[...])

    @pl.when(j == pl.num_programs(1) - 1)
    def _():
        z = DEEPNORM_ALPHA * x_ref[...] + acc_ref[...].T
        out = _layer_norm(z, g_ref[...], b_ref[...])
        o_ref[...] = out
        ob_ref[...] = out.astype(CDT)


def _peer_dense(xb, x, u, vt, stats, ln_g, ln_b, tm=512, te=1024):
    T, D = x.shape
    E = u.shape[0]
    H, n = PEER_HEADS, PEER_N_KEYS
    row = pl.BlockSpec((tm, D), lambda i, j: (i, 0))
    st = pl.BlockSpec((H, n, tm), lambda i, j: (0, 0, i))
    st1 = pl.BlockSpec((H, te // n, tm), lambda i, j: (0, j, i))
    vec = pl.BlockSpec((1, D), lambda i, j: (0, 0))
    assert (te // n) % 8 == 0
    kern = functools.partial(_peer_dense_kernel, tm=tm, te=te)
    return pl.pallas_call(
        kern, grid=(T // tm, E // te),
        in_specs=[row, row, pl.BlockSpec((te, D), lambda i, j: (j, 0)), pl.BlockSpec((D, te), lambda i, j: (0, j)),
                  st1, st1, st, st, vec, vec],
        out_specs=[row, row],
        out_shape=[jax.ShapeDtypeStruct((T, D), F32), jax.ShapeDtypeStruct((T, D), CDT)],
        scratch_shapes=[pltpu.VMEM((D, tm), F32), pltpu.VMEM((te, tm), CDT), pltpu.VMEM((te, tm), CDT),
                        pltpu.VMEM((H, n, tm), CDT), pltpu.VMEM((H, n, tm), CDT)],
        compiler_params=_params("parallel", "arbitrary"), name="peer_dense")(
            xb, x, u, vt, *stats, ln_g, ln_b)


def _rot_cols(w, dim):
    k, n = w.shape
    w4 = w.reshape(k, n // dim, 2, dim // 2)
    return jnp.stack([-w4[:, :, 1], w4[:, :, 0]], axis=2).reshape(k, n)


def _pad_heads(w, dim):
    k, n = w.shape
    w3 = w.reshape(k, n // dim, dim)
    return jnp.pad(w3, ((0, 0), (0, 0), (0, LANES - dim))).reshape(k, (n // dim) * LANES)


def _pad_cols(w, n):
    return jnp.pad(w, ((0, 0), (0, n - w.shape[1])))


def _layer_weights(l, w_in, mla_q_norm, mla_w_uq, mla_kv_norm, mla_w_ukv, w_branch_mla, w_branch_nsa, w_out,
                   ln1_g, ln1_b, peer_w_query, peer_sub_keys, peer_u, peer_v, ln2_g, ln2_b):
    off = np.cumsum((0,) + IN_WIDTHS)
    wi = w_in[l]
    seg = lambda k: wi[:, off[k]:off[k + 1]]
    G, dk = NSA_GROUPS, NSA_HEAD_DIM
    kv = seg(4).reshape(D_MODEL, 6, G * dk)
    lw = {}
    lw["w_cqkv"] = jnp.concatenate([seg(0), seg(1)], axis=1).astype(CDT)
    kr = seg(2)
    lw["w_kr"] = _pad_cols(kr, LANES).astype(CDT)
    lw["w_kr_rot"] = _pad_cols(_rot_cols(kr, MLA_ROPE_DIM), LANES).astype(CDT)
    lw["w_nq"] = _pad_heads(seg(3), dk).astype(CDT)
    lw["w_nq_rot"] = _pad_heads(_rot_cols(seg(3), dk), dk).astype(CDT)
    lw["w_kcmp"] = kv[:, 0].astype(CDT)
    lw["w_kcmp_rot"] = _rot_cols(kv[:, 0], dk).astype(CDT)
    for name, idx in (("ksel", 2), ("kwin", 4)):
        lw["w_" + name] = _pad_heads(kv[:, idx], dk).astype(CDT)
        lw["w_" + name + "_rot"] = _pad_heads(_rot_cols(kv[:, idx], dk), dk).astype(CDT)
    lw["w_vcmp"] = kv[:, 1].astype(CDT)
    lw["w_vsel"] = _pad_heads(kv[:, 3], dk).astype(CDT)
    lw["w_vwin"] = _pad_heads(kv[:, 5], dk).astype(CDT)
    ng = seg(5).reshape(D_MODEL, G, NSA_HPG * 3)
    lw["w_ngate"] = jnp.pad(ng, ((0, 0), (0, 0), (0, LANES - NSA_HPG * 3))).reshape(D_MODEL, G * LANES).astype(CDT)
    lw["w_mgate"] = seg(6).astype(CDT)

    H = MLA_HEADS
    dq = MLA_NOPE_DIM + MLA_ROPE_DIM
    uq = mla_w_uq[l].reshape(MLA_Q_RANK, H, dq)
    lw["wqn"] = _pad_heads(uq[:, :, :MLA_NOPE_DIM].reshape(MLA_Q_RANK, -1), MLA_NOPE_DIM).astype(CDT)
    wqr = uq[:, :, MLA_NOPE_DIM:].reshape(MLA_Q_RANK, H * MLA_ROPE_DIM)
    lw["wqr"] = wqr.astype(CDT)
    lw["wqr_rot"] = _rot_cols(wqr, MLA_ROPE_DIM).astype(CDT)
    pq = np.zeros((H * MLA_ROPE_DIM, H * LANES), np.float32)
    pk = np.zeros((LANES, H * LANES), np.float32)
    for h in range(H):
        for j in range(MLA_ROPE_DIM):
            pq[h * MLA_ROPE_DIM + j, h * LANES + MLA_NOPE_DIM + j] = 1.0
            pk[j, h * LANES + MLA_NOPE_DIM + j] = 1.0
    lw["place_q"] = jnp.asarray(pq, CDT)
    lw["place_k"] = jnp.asarray(pk, CDT)
    ukv = mla_w_ukv[l].reshape(MLA_KV_RANK, H, MLA_NOPE_DIM + MLA_V_DIM)
    lw["wkn"] = _pad_heads(ukv[:, :, :MLA_NOPE_DIM].reshape(MLA_KV_RANK, -1), MLA_NOPE_DIM).astype(CDT)
    lw["wv"] = ukv[:, :, MLA_NOPE_DIM:].reshape(MLA_KV_RANK, H * MLA_V_DIM).astype(CDT)
    lw["q_gain"] = mla_q_norm[l].reshape(1, -1)
    lw["kv_gain"] = mla_kv_norm[l].reshape(1, -1)

    lw["w_bm"] = w_branch_mla[l].astype(CDT)
    wbn = w_branch_nsa[l].reshape(NSA_HEADS, dk, D_MODEL)
    lw["w_bn"] = jnp.pad(wbn, ((0, 0), (0, LANES - dk), (0, 0))).reshape(NSA_HEADS * LANES, D_MODEL).astype(CDT)
    lw["w_out"] = w_out[l].astype(CDT)
    lw["ln1_g"] = ln1_g[l].reshape(1, -1)
    lw["ln1_b"] = ln1_b[l].reshape(1, -1)

    lw["w_pq"] = peer_w_query[l].astype(CDT)
    sk = peer_sub_keys[l].reshape(PEER_HEADS * 2, PEER_N_KEYS, PEER_KEY_DIM // 2)
    eye = jnp.eye(PEER_HEADS * 2, dtype=sk.dtype)
    lw["keys_bd_t"] = jnp.einsum("gnd,gf->gnfd", sk, eye).reshape(
        PEER_HEADS * 2 * PEER_N_KEYS, PEER_HEADS * PEER_KEY_DIM).astype(CDT)
    lw["u"] = peer_u[l].astype(CDT)
    lw["vt"] = peer_v[l].T.astype(CDT)
    lw["ln2_g"] = ln2_g[l].reshape(1, -1)
    lw["ln2_b"] = ln2_b[l].reshape(1, -1)
    return lw


def _rope_tables(positions, dim):
    inv_freq = ROPE_THETA ** (-jnp.arange(0, dim, 2, dtype=F32) / dim)
    ang = positions.astype(F32).reshape(-1, 1) * inv_freq
    rep = lambda t: jnp.tile(jnp.concatenate([t, t], axis=-1), (1, LANES // dim))
    return rep(jnp.cos(ang)), rep(jnp.sin(ang))


def _hybrid_layer(h, hb, B, S, lw, cmp_w, tab_mla, tab_nsa):
    T = B * S
    G, dk = NSA_GROUPS, NSA_HEAD_DIM
    lane = jnp.arange(G * LANES) % LANES - LANES // 2
    blk_onehot = (jnp.arange(S)[:, None] // SEL_BLOCK == lane[None, :]).astype(F32)
    ones_lane = np.zeros((1, G * LANES), np.float32)
    ones_lane[0, [g * LANES + dk for g in range(G)]] = 1.0
    p = _proj_all(hb, lw, tab_mla, tab_nsa, blk_onehot, jnp.asarray(ones_lane))
    cqkv, kr, nq, kcmp, ksel, kwin = p["cqkv"], p["kr"], p["nq"], p["kcmp"], p["ksel"], p["kwin"]
    vcmp, vsel, vwin, ngate, mgate = p["vcmp"], p["vsel"], p["vwin"], p["ngate"], p["mgate"]

    q, k, v = _mla_up(cqkv, kr, lw, tab_mla)
    o_mla = _mla_attn(q.reshape(B, S, -1), k.reshape(B, S, -1), v.reshape(B, S, -1))

    def chunks(a):
        return a.reshape(B, S, G, dk).transpose(0, 2, 1, 3).reshape(B * G, S // CMP_STRIDE, CMP_STRIDE * dk)

    kc = _compress(chunks(kcmp), cmp_w["k_pe"], cmp_w["k_w1"], cmp_w["k_w2"])
    vc = _compress(chunks(vcmp), cmp_w["v_pe"], cmp_w["v_w1"], cmp_w["v_w2"])
    r3 = lambda a: a.reshape(B, S, -1)
    o_nsa = _nsa(r3(nq), kc, vc, r3(ksel), r3(vsel), r3(kwin), r3(vwin), r3(ngate))

    x1, x1b = _merge(o_mla.reshape(T, -1), o_nsa.reshape(T, -1), mgate, h, lw)
    stats = _peer_stats(x1b, lw["w_pq"], lw["keys_bd_t"])
    return _peer_dense(x1b, x1, lw["u"], lw["vt"], stats, lw["ln2_g"], lw["ln2_b"])


def kernel(x, positions, ln_in_g, ln_in_b, w_in, mla_q_norm, mla_w_uq, mla_kv_norm, mla_w_ukv, nsa_cmp_k_pe, nsa_cmp_k_w1, nsa_cmp_k_w2, nsa_cmp_v_pe, nsa_cmp_v_w1, nsa_cmp_v_w2, w_branch_mla, w_branch_nsa, w_out, ln1_g, ln1_b, peer_w_query, peer_sub_keys, peer_u, peer_v, ln2_g, ln2_b):
    B, S, D = x.shape
    tab_mla = _rope_tables(positions, MLA_ROPE_DIM)
    tab_nsa = _rope_tables(positions, NSA_HEAD_DIM)
    h, hb = _ln_in(x.reshape(B * S, D), ln_in_g, ln_in_b)
    for l in range(DEPTH):
        lw = _layer_weights(l, w_in, mla_q_norm, mla_w_uq, mla_kv_norm, mla_w_ukv, w_branch_mla, w_branch_nsa,
                            w_out, ln1_g, ln1_b, peer_w_query, peer_sub_keys, peer_u, peer_v, ln2_g, ln2_b)
        cmp_w = {"k_pe": nsa_cmp_k_pe[l], "k_w1": nsa_cmp_k_w1[l], "k_w2": nsa_cmp_k_w2[l],
                 "v_pe": nsa_cmp_v_pe[l], "v_w1": nsa_cmp_v_w1[l], "v_w2": nsa_cmp_v_w2[l]}
        h, hb = _hybrid_layer(h, hb, B, S, lw, cmp_w, tab_mla, tab_nsa)
    return h.reshape(B, S, D)
```

```python
import functools
import math

import numpy as np
import jax
import jax.numpy as jnp
from jax import lax
from jax.experimental import pallas as pl
from jax.experimental.pallas import tpu as pltpu

D_MODEL = 1024
DEPTH = 2
ROPE_THETA = 10000.0

MLA_HEADS = 8
MLA_NOPE_DIM = 64
MLA_ROPE_DIM = 32
MLA_V_DIM = 64
MLA_Q_RANK = 768
MLA_KV_RANK = 256

NSA_HEADS = 8
NSA_GROUPS = 2
NSA_HPG = NSA_HEADS // NSA_GROUPS
NSA_HEAD_DIM = 64
CMP_BLOCK = 32
CMP_STRIDE = 16
CMP_HIDDEN = 2 * NSA_HEAD_DIM
SEL_BLOCK = 64
SEL_TOPN = 16
WINDOW = 512
FORCE_SCORE = 1e9

PEER_HEADS = 8
PEER_KEY_DIM = 128
PEER_N_KEYS = 128
PEER_N_EXPERTS = PEER_N_KEYS * PEER_N_KEYS
PEER_TOPK = 16

DEEPNORM_ALPHA = (2 * DEPTH) ** 0.25
LN_EPS = 1e-5
RMS_EPS = 1e-6
NEG_BIG = -1e30
LOG2E = math.log2(math.e)
MASK_BIAS = 32768.0

IN_WIDTHS = (MLA_Q_RANK, MLA_KV_RANK, MLA_ROPE_DIM, NSA_HEADS * NSA_HEAD_DIM,
             6 * NSA_GROUPS * NSA_HEAD_DIM, 3 * NSA_HEADS, 2 * D_MODEL)

LANES = 128
VMEM_LIMIT = 56 * 1024 * 1024

F32 = jnp.float32
CDT = jnp.bfloat16

_NT = (((1,), (1,)), ((), ()))


def _dot(a, b):
    return jnp.dot(a, b, preferred_element_type=F32)


def _dot_nt(a, b):
    return lax.dot_general(a, b, _NT, preferred_element_type=F32)


def _params(*sem):
    return pltpu.CompilerParams(dimension_semantics=sem, vmem_limit_bytes=VMEM_LIMIT)


def _gelu(x):
    return 0.5 * x * (1.0 + lax.erf(x * (2.0 ** -0.5)))


def _layer_norm(z, g, b):
    mu = jnp.mean(z, axis=-1, keepdims=True)
    d = z - mu
    var = jnp.mean(d * d, axis=-1, keepdims=True)
    return d * lax.rsqrt(var + LN_EPS) * g + b


def _ln_in_kernel(x_ref, g_ref, b_ref, h_ref, hb_ref):
    h = _layer_norm(x_ref[...], g_ref[...], b_ref[...])
    h_ref[...] = h
    hb_ref[...] = h.astype(CDT)


def _ln_in(x, g, b, tm=512):
    T, D = x.shape
    row = pl.BlockSpec((tm, D), lambda i: (i, 0))
    vec = pl.BlockSpec((1, D), lambda i: (0, 0))
    return pl.pallas_call(
        _ln_in_kernel, grid=(T // tm,), in_specs=[row, vec, vec], out_specs=[row, row],
        out_shape=[jax.ShapeDtypeStruct((T, D), F32), jax.ShapeDtypeStruct((T, D), CDT)],
        compiler_params=_params("parallel"), name="ln_in")(x, g.reshape(1, D), b.reshape(1, D))


_PROJ_CHUNK = 512


def _proj_all_kernel(hb_ref, w_ref, cm_ref, sm_ref, cn_ref, sn_ref, hot_ref, one_ref, *out_refs, plan):
    hb = hb_ref[...]
    tables = {"mla": (cm_ref, sm_ref), "nsa": (cn_ref, sn_ref)}
    for o_ref, (off, width, rot_off, table, act, bias) in zip(out_refs, plan):
        for c0 in range(0, width, _PROJ_CHUNK):
            cw = min(_PROJ_CHUNK, width - c0)
            y = _dot(hb, w_ref[:, off + c0:off + c0 + cw])
            if rot_off is not None:
                yr = _dot(hb, w_ref[:, rot_off + c0:rot_off + c0 + cw])
                cos = jnp.concatenate([tables[table][0][...]] * (cw // LANES), axis=1)
                sin = jnp.concatenate([tables[table][1][...]] * (cw // LANES), axis=1)
                y = y * cos + yr * sin
            if act == "sigmoid":
                y = 1.0 / (1.0 + jnp.exp(-y))
            if bias == "block_onehot":
                y = y + hot_ref[:, c0:c0 + cw]
            elif bias == "ones_lane":
                y = y + one_ref[:, c0:c0 + cw]
            o_ref[:, c0:c0 + cw] = y.astype(o_ref.dtype)


def _proj_all(hb, lw, tab_mla, tab_nsa, blk_onehot, ones_lane, tm=512):
    segs = [("cqkv", lw["w_cqkv"], None, None, None, None, CDT),
            ("kr", lw["w_kr"], lw["w_kr_rot"], "mla", None, None, CDT),
            ("nq", lw["w_nq"], lw["w_nq_rot"], "nsa", None, None, CDT),
            ("kcmp", lw["w_kcmp"], lw["w_kcmp_rot"], "nsa", None, None, CDT),
            ("ksel", lw["w_ksel"], lw["w_ksel_rot"], "nsa", None, "block_onehot", CDT),
            ("kwin", lw["w_kwin"], lw["w_kwin_rot"], "nsa", None, None, CDT),
            ("vcmp", lw["w_vcmp"], None, None, None, None, CDT),
            ("vsel", lw["w_vsel"], None, None, None, "ones_lane", CDT),
            ("vwin", lw["w_vwin"], None, None, None, "ones_lane", CDT),
            ("ngate", lw["w_ngate"], None, None, "sigmoid", None, F32),
            ("mgate", lw["w_mgate"], None, None, "sigmoid", None, CDT)]
    T, K = hb.shape
    cols, plan, off = [], [], 0
    for _, w, w_rot, table, act, bias, _ in segs:
        width = w.shape[1]
        cols.append(w)
        rot_off = None
        if w_rot is not None:
            cols.append(w_rot)
            rot_off = off + width
        plan.append((off, width, rot_off, table, act, bias))
        off += width * (2 if w_rot is not None else 1)
    assert T % tm == 0 and blk_onehot.shape[0] % tm == 0
    w_all = jnp.concatenate(cols, axis=1)
    row = lambda n: pl.BlockSpec((tm, n), lambda i: (i, 0))
    full = lambda a: pl.BlockSpec(a.shape, lambda i: (0, 0))
    outs = pl.pallas_call(
        functools.partial(_proj_all_kernel, plan=tuple(plan)), grid=(T // tm,),
        in_specs=[row(K), full(w_all), row(LANES), row(LANES), row(LANES), row(LANES),
                  pl.BlockSpec((tm, blk_onehot.shape[1]), lambda i: (i % (blk_onehot.shape[0] // tm), 0)),
                  full(ones_lane)],
        out_specs=[row(s[1].shape[1]) for s in segs],
        out_shape=[jax.ShapeDtypeStruct((T, s[1].shape[1]), s[6]) for s in segs],
        compiler_params=_params("parallel"), name="proj_all")(
            hb, w_all, tab_mla[0], tab_mla[1], tab_nsa[0], tab_nsa[1], blk_onehot, ones_lane)
    return dict(zip([s[0] for s in segs], outs))


def _rms(x, g):
    return x * lax.rsqrt(jnp.mean(x * x, axis=-1, keepdims=True) + RMS_EPS) * g


def _mla_up_kernel(c_ref, kr_ref, qg_ref, kvg_ref, wqn_ref, wqr_ref, wqrr_ref, pq_ref,
                   wkn_ref, wv_ref, pk_ref, cos_ref, sin_ref, q_ref, k_ref, v_ref):
    c = c_ref[...].astype(F32)
    cqn = _rms(c[:, :MLA_Q_RANK], qg_ref[...]).astype(CDT)
    ckvn = _rms(c[:, MLA_Q_RANK:], kvg_ref[...]).astype(CDT)
    cos = jnp.concatenate([cos_ref[...]] * 2, axis=1)
    sin = jnp.concatenate([sin_ref[...]] * 2, axis=1)
    roped = (_dot(cqn, wqr_ref[...]) * cos + _dot(cqn, wqrr_ref[...]) * sin).astype(CDT)
    q_ref[...] = (_dot(cqn, wqn_ref[...]) + _dot(roped, pq_ref[...])).astype(CDT)
    k_ref[...] = (_dot(ckvn, wkn_ref[...]) + _dot(kr_ref[...], pk_ref[...])).astype(CDT)
    v_ref[...] = _dot(ckvn, wv_ref[...]).astype(CDT)


def _mla_up(cqkv, kr, lw, tables, tm=512):
    T = cqkv.shape[0]
    row = lambda n: pl.BlockSpec((tm, n), lambda i: (i, 0))
    full = lambda a: pl.BlockSpec(a.shape, lambda i: (0, 0))
    ws = (lw["q_gain"], lw["kv_gain"], lw["wqn"], lw["wqr"], lw["wqr_rot"], lw["place_q"],
          lw["wkn"], lw["wv"], lw["place_k"])
    hp = MLA_HEADS * LANES
    return pl.pallas_call(
        _mla_up_kernel, grid=(T // tm,),
        in_specs=[row(cqkv.shape[1]), row(LANES)] + [full(a) for a in ws] + [row(LANES), row(LANES)],
        out_specs=[row(hp), row(hp), row(MLA_HEADS * MLA_V_DIM)],
        out_shape=[jax.ShapeDtypeStruct((T, hp), CDT), jax.ShapeDtypeStruct((T, hp), CDT),
                   jax.ShapeDtypeStruct((T, MLA_HEADS * MLA_V_DIM), CDT)],
        compiler_params=_params("parallel"), name="mla_up")(cqkv, kr, *ws, tables[0], tables[1])


def _mla_attn_kernel(q_ref, k_ref, v_ref, o_ref, *, tq, scale, nh):
    i = pl.program_id(2)
    qs = [q_ref[0, :, hd * LANES:(hd + 1) * LANES] for hd in range(nh)]

    def step(kt, carries, diagonal):
        k0 = pl.multiple_of(kt * tq, tq)
        new = []
        for hd, (m, l, acc) in enumerate(carries):
            k = k_ref[0, pl.ds(k0, tq), hd * LANES:(hd + 1) * LANES]
            v = v_ref[0, pl.ds(k0, tq), (hd // 2) * LANES:(hd // 2 + 1) * LANES]
            s = _dot_nt(qs[hd], k) * scale
            if diagonal:
                row = lax.broadcasted_iota(jnp.int32, (tq, tq), 0)
                col = lax.broadcasted_iota(jnp.int32, (tq, tq), 1)
                s = jnp.where(col <= row, s, NEG_BIG)
            m_new = jnp.maximum(m, jnp.max(s, axis=1, keepdims=True))
            alpha = jnp.exp(m - m_new)
            p = jnp.exp(s - m_new)
            l = l * alpha + jnp.sum(p, axis=1, keepdims=True)
            acc = acc * alpha + _dot(p.astype(CDT), v)
            new.append((m_new, l, acc))
        return tuple(new)

    one = (jnp.full((tq, 1), NEG_BIG, F32), jnp.zeros((tq, 1), F32), jnp.zeros((tq, LANES), F32))
    carries = lax.fori_loop(0, i, lambda kt, c: step(kt, c, False), (one,) * nh)
    outs = [acc / l for _, l, acc in step(i, carries, True)]
    lane = lax.broadcasted_iota(jnp.int32, (tq, LANES), 1)
    for pair in range(nh // 2):
        o = jnp.where(lane < MLA_V_DIM, outs[2 * pair], outs[2 * pair + 1])
        o_ref[0, :, pair * LANES:(pair + 1) * LANES] = o.astype(o_ref.dtype)


def _mla_attn(q, k, v, tq=512, nh=4):
    B, S, _ = q.shape
    scale = (MLA_NOPE_DIM + MLA_ROPE_DIM) ** -0.5
    kern = functools.partial(_mla_attn_kernel, tq=tq, scale=scale, nh=nh)
    vw = nh * MLA_V_DIM
    return pl.pallas_call(
        kern, grid=(B, MLA_HEADS // nh, S // tq),
        in_specs=[pl.BlockSpec((1, tq, nh * LANES), lambda b, h, i: (b, i, h)),
                  pl.BlockSpec((1, S, nh * LANES), lambda b, h, i: (b, 0, h)),
                  pl.BlockSpec((1, S, vw), lambda b, h, i: (b, 0, h))],
        out_specs=pl.BlockSpec((1, tq, vw), lambda b, h, i: (b, i, h)),
        out_shape=jax.ShapeDtypeStruct((B, S, MLA_HEADS * MLA_V_DIM), CDT),
        compiler_params=_params("parallel", "parallel", "arbitrary"), name="mla_attn")(q, k, v)


def _compress_kernel(c_ref, pe_ref, w1a_ref, w1b_ref, w2_ref, o_ref, *, n_valid):
    c = c_ref[0].astype(F32)
    n = c.shape[0]
    a = _dot((c + pe_ref[0:1, :]).astype(CDT), w1a_ref[...])
    b = _dot((c + pe_ref[1:2, :]).astype(CDT), w1b_ref[...])
    hid = _gelu(a + pltpu.roll(b, n - 1, 0))
    out = _dot(hid.astype(CDT), w2_ref[...])
    row = lax.broadcasted_iota(jnp.int32, out.shape, 0)
    o_ref[0] = jnp.where(row < n_valid, out, 0.0).astype(o_ref.dtype)


def _compress(chunks, pe, w1, w2):
    BG, n, width = chunks.shape
    pe2 = pe.reshape(2, width).astype(F32)
    w1a = w1[:width].astype(CDT)
    w1b = w1[width:].astype(CDT)
    w2p = jnp.pad(w2, ((0, 0), (0, LANES - w2.shape[1]))).astype(CDT)
    full = lambda a: pl.BlockSpec(a.shape, lambda i: (0, 0))
    return pl.pallas_call(
        functools.partial(_compress_kernel, n_valid=n - 1), grid=(BG,),
        in_specs=[pl.BlockSpec((1, n, width), lambda i: (i, 0, 0)), full(pe2), full(w1a), full(w1b), full(w2p)],
        out_specs=pl.BlockSpec((1, n, LANES), lambda i: (i, 0, 0)),
        out_shape=jax.ShapeDtypeStruct((BG, n, LANES), CDT),
        compiler_params=_params("parallel"), name="nsa_compress")(chunks, pe2, w1a, w1b, w2p)


def _nsa_kernel(q_ref, kc_ref, vc_ref, ks_ref, vs_ref, kw_ref, vw_ref, g_ref, ovl_ref, edge_ref, o_ref,
                *, tq, n_sel, n_cmp, scale):
    i = pl.program_id(2)
    q0 = i * tq
    R = NSA_HPG
    ncp = kc_ref.shape[1]
    Q = jnp.concatenate([q_ref[0, :, r * LANES:(r + 1) * LANES] for r in range(R)], axis=0)

    scale2 = scale * LOG2E
    hidden = -MASK_BIAS * scale2
    t_c = q0 + lax.broadcasted_iota(jnp.int32, (tq, ncp), 0)
    n_c = lax.broadcasted_iota(jnp.int32, (tq, ncp), 1)
    bias_c = jnp.where((n_c * CMP_STRIDE + (CMP_BLOCK - 1) <= t_c) & (n_c < n_cmp), 0.0, hidden)
    s = (_dot_nt(Q, kc_ref[0]) * scale2).reshape(R, tq, ncp) + bias_c[None]
    p = jnp.exp2(s - jnp.max(s, axis=2, keepdims=True))
    t_q = q0 + lax.broadcasted_iota(jnp.int32, (tq, 1), 0)
    sees_any = jnp.where(t_q >= CMP_BLOCK - 1, 1.0, 0.0)
    p = p * (sees_any / jnp.sum(p, axis=2, keepdims=True))
    o_c = _dot(p.reshape(R * tq, ncp).astype(CDT), vc_ref[0])

    psum = jnp.sum(p, axis=0)
    p_hi = psum.astype(CDT)
    p_lo = (psum - p_hi.astype(F32)).astype(CDT)
    imp = (_dot_nt(ovl_ref[...], p_hi) + _dot_nt(ovl_ref[...], p_lo))[0:n_sel]
    jj = lax.broadcasted_iota(jnp.int32, (n_sel, tq), 0)
    cur = (q0 + lax.broadcasted_iota(jnp.int32, (n_sel, tq), 1)) // SEL_BLOCK
    forced = (jj == 0) | (jj == cur) | (jj == cur - 1)
    top_n = min(SEL_TOPN, n_sel)
    assert top_n > 3
    free = jnp.where(forced | (jj > cur), -FORCE_SCORE, imp)
    tops, v = [], free
    for _ in range(top_n - 1):
        m = jnp.max(v, axis=0, keepdims=True)
        tops.append(m)
        v = jnp.where(v == m, -jnp.inf, v)
    cur_row = cur[0:1, :]
    tau = jnp.where(cur_row >= 2, tops[top_n - 4], jnp.where(cur_row == 1, tops[top_n - 3], tops[top_n - 2]))
    half = LANES // 2
    parts = [jnp.zeros((half, tq), F32), jnp.where(forced | (free >= tau), 0.0, -MASK_BIAS)]
    if n_sel < half:
        parts.append(jnp.zeros((half - n_sel, tq), F32))
    q_bias = jnp.concatenate(parts, axis=0).T.astype(CDT)
    Qb = Q + jnp.concatenate([q_bias] * R, axis=0)

    causal_bias = edge_ref[0]
    window_bias = edge_ref[1]

    def step(k_ref, v_ref, kt, carry, bias):
        m, acc = carry
        k0 = pl.multiple_of(kt * tq, tq)
        s = (_dot_nt(Qb, k_ref[0, pl.ds(k0, tq), :]) * scale2).reshape(R, tq, tq)
        if bias is not None:
            s = s + bias[None]
        m_new = jnp.maximum(m, jnp.max(s, axis=2, keepdims=True))
        p = jnp.exp2(s - m_new).reshape(R * tq, tq).astype(CDT)
        pv = _dot(p, v_ref[0, pl.ds(k0, tq), :]).reshape(R, tq, LANES)
        return m_new, acc * jnp.exp2(m - m_new) + pv

    def finish(carry):
        acc = carry[1]
        return acc * (1.0 / acc[:, :, half:half + 1])

    def flag(cond):
        return jnp.where(cond, 1.0, 0.0)

    init = (jnp.full((R, tq, 1), NEG_BIG, F32), jnp.zeros((R, tq, LANES), F32))

    carry = lax.fori_loop(0, i, lambda kt, c: step(ks_ref, vs_ref, kt, c, None), init)
    o_s = finish(lax.fori_loop(i, i + 1, lambda kt, c: step(ks_ref, vs_ref, kt, c, causal_bias), carry))

    n_back = WINDOW // tq

    def win_step(it, c):
        kt = i - n_back + it
        bias = window_bias * flag(it == 0) + causal_bias * flag(it == n_back) + hidden * flag(kt < 0)
        return step(kw_ref, vw_ref, jnp.maximum(kt, 0), c, bias)

    o_w = finish(lax.fori_loop(0, n_back + 1, win_step, init))

    g = g_ref[0]
    o_c = o_c.reshape(R, tq, LANES)
    for r in range(R):
        o = (g[:, 3 * r:3 * r + 1] * o_c[r] + g[:, 3 * r + 1:3 * r + 2] * o_s[r]
             + g[:, 3 * r + 2:3 * r + 3] * o_w[r])
        o_ref[0, :, r * LANES:(r + 1) * LANES] = o.astype(o_ref.dtype)


def _nsa(q, kc, vc, ks, vs, kw, vw, gates, tq=512):
    B, S, _ = q.shape
    G = NSA_GROUPS
    n_sel = S // SEL_BLOCK
    n_cmp = (S - CMP_BLOCK) // CMP_STRIDE + 1
    ncp = kc.shape[1]
    assert n_sel <= LANES // 2 and n_sel % 8 == 0 and ncp % LANES == 0 and WINDOW % tq == 0
    jn = np.arange(LANES)[:, None] * SEL_BLOCK
    cn = np.arange(ncp)[None, :] * CMP_STRIDE
    ovl = ((cn <= jn + SEL_BLOCK - 1) & (cn + CMP_BLOCK - 1 >= jn)
           & (np.arange(LANES)[:, None] < n_sel) & (np.arange(ncp)[None, :] < n_cmp))
    ovl = jnp.asarray(ovl, CDT)
    scale = NSA_HEAD_DIM ** -0.5
    hidden = -MASK_BIAS * scale * LOG2E
    row, col = np.arange(tq)[:, None], np.arange(tq)[None, :]
    edge = jnp.asarray(np.stack([np.where(col <= row, 0.0, hidden), np.where(col > row, 0.0, hidden)]), F32)
    kern = functools.partial(_nsa_kernel, tq=tq, n_sel=n_sel, n_cmp=n_cmp, scale=scale)
    cmp_spec = pl.BlockSpec((1, ncp, LANES), lambda b, g, i: (b * G + g, 0, 0))
    kv_spec = pl.BlockSpec((1, S, LANES), lambda b, g, i: (b, 0, g))
    qo_spec = pl.BlockSpec((1, tq, NSA_HPG * LANES), lambda b, g, i: (b, i, g))
    return pl.pallas_call(
        kern, grid=(B, G, S // tq),
        in_specs=[qo_spec, cmp_spec, cmp_spec, kv_spec, kv_spec, kv_spec, kv_spec,
                  pl.BlockSpec((1, tq, LANES), lambda b, g, i: (b, i, g)),
                  pl.BlockSpec(ovl.shape, lambda b, g, i: (0, 0)),
                  pl.BlockSpec(edge.shape, lambda b, g, i: (0, 0, 0))],
        out_specs=qo_spec,
        out_shape=jax.ShapeDtypeStruct((B, S, NSA_HEADS * LANES), CDT),
        compiler_params=_params("parallel", "parallel", "arbitrary"), name="nsa_attn")(
            q, kc, vc, ks, vs, kw, vw, gates, ovl, edge)


def _merge_kernel(om_ref, on_ref, mg_ref, h_ref, wbm_ref, wbn_ref, wo_ref, g_ref, b_ref, x_ref, xb_ref):
    D = D_MODEL
    y_mla = _dot(om_ref[...], wbm_ref[...])
    y_nsa = _dot(on_ref[...], wbn_ref[...])
    mg = mg_ref[...].astype(F32)
    mixed = _dot((mg[:, :D] * y_mla + mg[:, D:] * y_nsa).astype(CDT), wo_ref[...])
    x = _layer_norm(DEEPNORM_ALPHA * h_ref[...] + mixed, g_ref[...], b_ref[...])
    x_ref[...] = x
    xb_ref[...] = x.astype(CDT)


def _merge(o_mla, o_nsa, mg, h, lw, tm=512):
    T, D = h.shape
    row = lambda n: pl.BlockSpec((tm, n), lambda i: (i, 0))
    full = lambda a: pl.BlockSpec(a.shape, lambda i: (0, 0))
    ws = (lw["w_bm"], lw["w_bn"], lw["w_out"], lw["ln1_g"], lw["ln1_b"])
    return pl.pallas_call(
        _merge_kernel, grid=(T // tm,),
        in_specs=[row(o_mla.shape[1]), row(o_nsa.shape[1]), row(2 * D), row(D)] + [full(a) for a in ws],
        out_specs=[row(D), row(D)],
        out_shape=[jax.ShapeDtypeStruct((T, D), F32), jax.ShapeDtypeStruct((T, D), CDT)],
        compiler_params=_params("parallel"), name="merge_ln1")(o_mla, o_nsa, mg, h, *ws)


def _rows_per_vreg():
    return 8 * (4 // jnp.dtype(CDT).itemsize)


def _pack_words(x):
    if jnp.dtype(CDT).itemsize == 4:
        return pltpu.bitcast(x, jnp.uint32)
    bits = pltpu.bitcast(x.astype(CDT).astype(F32), jnp.uint32)
    return (bits >> 16) | (bits & jnp.uint32(0xFFFF0000))


def _unpack_words(row):
    return pltpu.bitcast(jnp.broadcast_to(row, (8, LANES)), CDT)


_PEER_PAIRS = [(a, b) for a in range(PEER_TOPK) for b in range(PEER_TOPK) if (a + 1) * (b + 1) <= PEER_TOPK]
_N_CAND = -(-len(_PEER_PAIRS) // 8) * 8
HEADS_PER_TRIP = 2


def _top_values(v, want_rank):
    tops = []
    rank = jnp.full(v.shape, 127.0, F32) if want_rank else None
    for r in range(PEER_TOPK):
        m = jnp.max(v, axis=0, keepdims=True)
        tops.append(m)
        eq = v == m
        if want_rank:
            rank = jnp.where(eq, float(r), rank)
        v = jnp.where(eq, -jnp.inf, v)
    return tops, rank


def _peer_stats_kernel(xb_ref, wq_ref, keys_ref, c1_ref, e1_ref, r2_ref, e2_ref, s_ref, cand_ref):
    qp = _dot(xb_ref[...], wq_ref[...]).astype(CDT)
    s_ref[...] = _dot_nt(keys_ref[...], qp)
    n = PEER_N_KEYS
    cand_ref[...] = jnp.full(cand_ref.shape, -jnp.inf, F32)

    def one_head(h, cand_ref):
        s1 = s_ref[pl.ds(pl.multiple_of(2 * h * n, n), n), :]
        s2 = s_ref[pl.ds(pl.multiple_of((2 * h + 1) * n, n), n), :]
        t1, _ = _top_values(s1, False)
        t2, rank2 = _top_values(s2, True)
        for c, (a, b) in enumerate(_PEER_PAIRS):
            cand_ref[c:c + 1, :] = t1[a] + t2[b]
        cand = cand_ref[...]
        tc, _ = _top_values(cand, False)
        tau = tc[PEER_TOPK - 1]
        top = t1[0] + t2[0]
        z = jnp.sum(jnp.where(cand >= tau, jnp.exp(cand - top), 0.0), axis=0, keepdims=True)
        cnt = jnp.zeros(s1.shape, F32)
        for b in range(PEER_TOPK // 2):
            cnt = cnt + jnp.where(s1 + t2[b] >= tau, 1.0, 0.0)
        best_extra = jnp.zeros(tau.shape, F32)
        for b in range(PEER_TOPK // 2, PEER_TOPK):
            best_extra = best_extra + jnp.where(t1[0] + t2[b] >= tau, 1.0, 0.0)
        cnt = cnt + jnp.where(s1 == t1[0], best_extra, 0.0)
        c1_ref[h] = _pack_words(cnt)
        e1_ref[h] = _pack_words(jnp.exp(s1 - t1[0]))
        r2_ref[h] = rank2.astype(r2_ref.dtype)
        e2_ref[h] = (jnp.exp(s2 - t2[0]) / z).astype(e2_ref.dtype)

    def heads(idx, carry):
        for k in range(HEADS_PER_TRIP):
            one_head(idx * HEADS_PER_TRIP + k, cand_ref.at[k])
        return carry

    lax.fori_loop(0, PEER_HEADS // HEADS_PER_TRIP, heads, 0)


def _peer_stats(xb, wq, keys_bd_t, tm=256):
    T, D = xb.shape
    H, n = PEER_HEADS, PEER_N_KEYS
    full = lambda a: pl.BlockSpec(a.shape, lambda i: (0, 0))
    o_spec = pl.BlockSpec((H, n, tm), lambda i: (0, 0, i))
    o_u32 = jax.ShapeDtypeStruct((H, n, T), jnp.uint32)
    o_cdt = jax.ShapeDtypeStruct((H, n, T), CDT)
    return pl.pallas_call(
        _peer_stats_kernel, grid=(T // tm,),
        in_specs=[pl.BlockSpec((tm, D), lambda i: (i, 0)), full(wq), full(keys_bd_t)],
        out_specs=[o_spec] * 4, out_shape=[o_u32, o_u32, o_cdt, o_cdt],
        scratch_shapes=[pltpu.VMEM((2 * H * n, tm), F32), pltpu.VMEM((HEADS_PER_TRIP, _N_CAND, tm), F32)],
        compiler_params=_params("parallel"), name="peer_stats")(xb, wq, keys_bd_t)


def _peer_dense_kernel(xb_ref, x_ref, u_ref, vt_ref, c1_ref, e1_ref, r2_in_ref, e2_in_ref, g_ref, b_ref,
                       o_ref, ob_ref, acc_ref, a_ref, h_ref, r2_ref, e2_ref, *, tm, te):
    j = pl.program_id(1)
    n = PEER_N_KEYS
    rows_per_tile = te // n

    @pl.when(j == 0)
    def _():
        acc_ref[...] = jnp.zeros(acc_ref.shape, F32)
        r2_ref[...] = r2_in_ref[...]
        e2_ref[...] = e2_in_ref[...]

    a_ref[...] = _gelu(_dot_nt(u_ref[...], xb_ref[...])).astype(CDT)
    ic = n // 2
    sub = _rows_per_vreg()
    nv = ic // sub
    zero = jnp.zeros((nv, sub, LANES), CDT)
    n_i2 = n // ic

    def gate_block(idx):
        sl = pl.ds((idx // n_i2) * LANES, LANES)
        i2 = (idx % n_i2) * ic
        ws = [zero] * rows_per_tile
        for h in range(PEER_HEADS):
            r2 = r2_ref[h, pl.ds(i2, ic), sl].reshape(nv, sub, LANES)
            e2 = e2_ref[h, pl.ds(i2, ic), sl].reshape(nv, sub, LANES)
            for r in range(rows_per_tile):
                c1 = _unpack_words(c1_ref[h, r:r + 1, sl])[None]
                e1 = _unpack_words(e1_ref[h, r:r + 1, sl])[None]
                ws[r] = ws[r] + jnp.maximum(jnp.minimum(e1 * e2, c1 - r2), zero)
        for r in range(rows_per_tile):
            rows = pl.ds(r * n + i2, ic)
            h_ref[rows, sl] = ws[r].reshape(ic, LANES) * a_ref[rows, sl]

    for idx in range((tm // LANES) * n_i2):
        gate_block(idx)
    acc_ref[...] += _dot(vt_ref[...], h_ref[...])

    @pl.when(j == pl.num_programs(1) - 1)
    def _():
        z = DEEPNORM_ALPHA * x_ref[...] + acc_ref[...].T
        out = _layer_norm(z, g_ref[...], b_ref[...])
        o_ref[...] = out
        ob_ref[...] = out.astype(CDT)


def _peer_dense(xb, x, u, vt, stats, ln_g, ln_b, tm=512, te=1024):
    T, D = x.shape
    E = u.shape[0]
    H, n = PEER_HEADS, PEER_N_KEYS
    row = pl.BlockSpec((tm, D), lambda i, j: (i, 0))
    st = pl.BlockSpec((H, n, tm), lambda i, j: (0, 0, i))
    st1 = pl.BlockSpec((H, te // n, tm), lambda i, j: (0, j, i))
    vec = pl.BlockSpec((1, D), lambda i, j: (0, 0))
    assert (te // n) % 8 == 0
    kern = functools.partial(_peer_dense_kernel, tm=tm, te=te)
    return pl.pallas_call(
        kern, grid=(T // tm, E // te),
        in_specs=[row, row, pl.BlockSpec((te, D), lambda i, j: (j, 0)), pl.BlockSpec((D, te), lambda i, j: (0, j)),
                  st1, st1, st, st, vec, vec],
        out_specs=[row, row],
        out_shape=[jax.ShapeDtypeStruct((T, D), F32), jax.ShapeDtypeStruct((T, D), CDT)],
        scratch_shapes=[pltpu.VMEM((D, tm), F32), pltpu.VMEM((te, tm), CDT), pltpu.VMEM((te, tm), CDT),
                        pltpu.VMEM((H, n, tm), CDT), pltpu.VMEM((H, n, tm), CDT)],
        compiler_params=_params("parallel", "arbitrary"), name="peer_dense")(
            xb, x, u, vt, *stats, ln_g, ln_b)


def _rot_cols(w, dim):
    k, n = w.shape
    w4 = w.reshape(k, n // dim, 2, dim // 2)
    return jnp.stack([-w4[:, :, 1], w4[:, :, 0]], axis=2).reshape(k, n)


def _pad_heads(w, dim):
    k, n = w.shape
    w3 = w.reshape(k, n // dim, dim)
    return jnp.pad(w3, ((0, 0), (0, 0), (0, LANES - dim))).reshape(k, (n // dim) * LANES)


def _pad_cols(w, n):
    return jnp.pad(w, ((0, 0), (0, n - w.shape[1])))


def _layer_weights(l, w_in, mla_q_norm, mla_w_uq, mla_kv_norm, mla_w_ukv, w_branch_mla, w_branch_nsa, w_out,
                   ln1_g, ln1_b, peer_w_query, peer_sub_keys, peer_u, peer_v, ln2_g, ln2_b):
    off = np.cumsum((0,) + IN_WIDTHS)
    wi = w_in[l]
    seg = lambda k: wi[:, off[k]:off[k + 1]]
    G, dk = NSA_GROUPS, NSA_HEAD_DIM
    kv = seg(4).reshape(D_MODEL, 6, G * dk)
    lw = {}
    lw["w_cqkv"] = jnp.concatenate([seg(0), seg(1)], axis=1).astype(CDT)
    kr = seg(2)
    lw["w_kr"] = _pad_cols(kr, LANES).astype(CDT)
    lw["w_kr_rot"] = _pad_cols(_rot_cols(kr, MLA_ROPE_DIM), LANES).astype(CDT)
    lw["w_nq"] = _pad_heads(seg(3), dk).astype(CDT)
    lw["w_nq_rot"] = _pad_heads(_rot_cols(seg(3), dk), dk).astype(CDT)
    lw["w_kcmp"] = kv[:, 0].astype(CDT)
    lw["w_kcmp_rot"] = _rot_cols(kv[:, 0], dk).astype(CDT)
    for name, idx in (("ksel", 2), ("kwin", 4)):
        lw["w_" + name] = _pad_heads(kv[:, idx], dk).astype(CDT)
        lw["w_" + name + "_rot"] = _pad_heads(_rot_cols(kv[:, idx], dk), dk).astype(CDT)
    lw["w_vcmp"] = kv[:, 1].astype(CDT)
    lw["w_vsel"] = _pad_heads(kv[:, 3], dk).astype(CDT)
    lw["w_vwin"] = _pad_heads(kv[:, 5], dk).astype(CDT)
    ng = seg(5).reshape(D_MODEL, G, NSA_HPG * 3)
    lw["w_ngate"] = jnp.pad(ng, ((0, 0), (0, 0), (0, LANES - NSA_HPG * 3))).reshape(D_MODEL, G * LANES).astype(CDT)
    lw["w_mgate"] = seg(6).astype(CDT)

    H = MLA_HEADS
    dq = MLA_NOPE_DIM + MLA_ROPE_DIM
    uq = mla_w_uq[l].reshape(MLA_Q_RANK, H, dq)
    lw["wqn"] = _pad_heads(uq[:, :, :MLA_NOPE_DIM].reshape(MLA_Q_RANK, -1), MLA_NOPE_DIM).astype(CDT)
    wqr = uq[:, :, MLA_NOPE_DIM:].reshape(MLA_Q_RANK, H * MLA_ROPE_DIM)
    lw["wqr"] = wqr.astype(CDT)
    lw["wqr_rot"] = _rot_cols(wqr, MLA_ROPE_DIM).astype(CDT)
    pq = np.zeros((H * MLA_ROPE_DIM, H * LANES), np.float32)
    pk = np.zeros((LANES, H * LANES), np.float32)
    for h in range(H):
        for j in range(MLA_ROPE_DIM):
            pq[h * MLA_ROPE_DIM + j, h * LANES + MLA_NOPE_DIM + j] = 1.0
            pk[j, h * LANES + MLA_NOPE_DIM + j] = 1.0
    lw["place_q"] = jnp.asarray(pq, CDT)
    lw["place_k"] = jnp.asarray(pk, CDT)
    ukv = mla_w_ukv[l].reshape(MLA_KV_RANK, H, MLA_NOPE_DIM + MLA_V_DIM)
    lw["wkn"] = _pad_heads(ukv[:, :, :MLA_NOPE_DIM].reshape(MLA_KV_RANK, -1), MLA_NOPE_DIM).astype(CDT)
    lw["wv"] = ukv[:, :, MLA_NOPE_DIM:].reshape(MLA_KV_RANK, H * MLA_V_DIM).astype(CDT)
    lw["q_gain"] = mla_q_norm[l].reshape(1, -1)
    lw["kv_gain"] = mla_kv_norm[l].reshape(1, -1)

    lw["w_bm"] = w_branch_mla[l].astype(CDT)
    wbn = w_branch_nsa[l].reshape(NSA_HEADS, dk, D_MODEL)
    lw["w_bn"] = jnp.pad(wbn, ((0, 0), (0, LANES - dk), (0, 0))).reshape(NSA_HEADS * LANES, D_MODEL).astype(CDT)
    lw["w_out"] = w_out[l].astype(CDT)
    lw["ln1_g"] = ln1_g[l].reshape(1, -1)
    lw["ln1_b"] = ln1_b[l].reshape(1, -1)

    lw["w_pq"] = peer_w_query[l].astype(CDT)
    sk = peer_sub_keys[l].reshape(PEER_HEADS * 2, PEER_N_KEYS, PEER_KEY_DIM // 2)
    eye = jnp.eye(PEER_HEADS * 2, dtype=sk.dtype)
    lw["keys_bd_t"] = jnp.einsum("gnd,gf->gnfd", sk, eye).reshape(
        PEER_HEADS * 2 * PEER_N_KEYS, PEER_HEADS * PEER_KEY_DIM).astype(CDT)
    lw["u"] = peer_u[l].astype(CDT)
    lw["vt"] = peer_v[l].T.astype(CDT)
    lw["ln2_g"] = ln2_g[l].reshape(1, -1)
    lw["ln2_b"] = ln2_b[l].reshape(1, -1)
    return lw


def _rope_tables(positions, dim):
    inv_freq = ROPE_THETA ** (-jnp.arange(0, dim, 2, dtype=F32) / dim)
    ang = positions.astype(F32).reshape(-1, 1) * inv_freq
    rep = lambda t: jnp.tile(jnp.concatenate([t, t], axis=-1), (1, LANES // dim))
    return rep(jnp.cos(ang)), rep(jnp.sin(ang))


def _hybrid_layer(h, hb, B, S, lw, cmp_w, tab_mla, tab_nsa):
    T = B * S
    G, dk = NSA_GROUPS, NSA_HEAD_DIM
    lane = jnp.arange(G * LANES) % LANES - LANES // 2
    blk_onehot = (jnp.arange(S)[:, None] // SEL_BLOCK == lane[None, :]).astype(F32)
    ones_lane = np.zeros((1, G * LANES), np.float32)
    ones_lane[0, [g * LANES + dk for g in range(G)]] = 1.0
    p = _proj_all(hb, lw, tab_mla, tab_nsa, blk_onehot, jnp.asarray(ones_lane))
    cqkv, kr, nq, kcmp, ksel, kwin = p["cqkv"], p["kr"], p["nq"], p["kcmp"], p["ksel"], p["kwin"]
    vcmp, vsel, vwin, ngate, mgate = p["vcmp"], p["vsel"], p["vwin"], p["ngate"], p["mgate"]

    q, k, v = _mla_up(cqkv, kr, lw, tab_mla)
    o_mla = _mla_attn(q.reshape(B, S, -1), k.reshape(B, S, -1), v.reshape(B, S, -1))

    def chunks(a):
        return a.reshape(B, S, G, dk).transpose(0, 2, 1, 3).reshape(B * G, S // CMP_STRIDE, CMP_STRIDE * dk)

    kc = _compress(chunks(kcmp), cmp_w["k_pe"], cmp_w["k_w1"], cmp_w["k_w2"])
    vc = _compress(chunks(vcmp), cmp_w["v_pe"], cmp_w["v_w1"], cmp_w["v_w2"])
    r3 = lambda a: a.reshape(B, S, -1)
    o_nsa = _nsa(r3(nq), kc, vc, r3(ksel), r3(vsel), r3(kwin), r3(vwin), r3(ngate))

    x1, x1b = _merge(o_mla.reshape(T, -1), o_nsa.reshape(T, -1), mgate, h, lw)
    stats = _peer_stats(x1b, lw["w_pq"], lw["keys_bd_t"])
    return _peer_dense(x1b, x1, lw["u"], lw["vt"], stats, lw["ln2_g"], lw["ln2_b"])


def kernel(x, positions, ln_in_g, ln_in_b, w_in, mla_q_norm, mla_w_uq, mla_kv_norm, mla_w_ukv, nsa_cmp_k_pe, nsa_cmp_k_w1, nsa_cmp_k_w2, nsa_cmp_v_pe, nsa_cmp_v_w1, nsa_cmp_v_w2, w_branch_mla, w_branch_nsa, w_out, ln1_g, ln1_b, peer_w_query, peer_sub_keys, peer_u, peer_v, ln2_g, ln2_b):
    B, S, D = x.shape
    tab_mla = _rope_tables(positions, MLA_ROPE_DIM)
    tab_nsa = _rope_tables(positions, NSA_HEAD_DIM)
    h, hb = _ln_in(x.reshape(B * S, D), ln_in_g, ln_in_b)
    for l in range(DEPTH):
        lw = _layer_weights(l, w_in, mla_q_norm, mla_w_uq, mla_kv_norm, mla_w_ukv, w_branch_mla, w_branch_nsa,
                            w_out, ln1_g, ln1_b, peer_w_query, peer_sub_keys, peer_u, peer_v, ln2_g, ln2_b)
        cmp_w = {"k_pe": nsa_cmp_k_pe[l], "k_w1": nsa_cmp_k_w1[l], "k_w2": nsa_cmp_k_w2[l],
                 "v_pe": nsa_cmp_v_pe[l], "v_w1": nsa_cmp_v_w1[l], "v_w2": nsa_cmp_v_w2[l]}
        h, hb = _hybrid_layer(h, hb, B, S, lw, cmp_w, tab_mla, tab_nsa)
    return h.reshape(B, S, D)
```

```python
import functools
import math

import numpy as np
import jax
import jax.numpy as jnp
from jax import lax
from jax.experimental import pallas as pl
from jax.experimental.pallas import tpu as pltpu

D_MODEL = 1024
DEPTH = 2
ROPE_THETA = 10000.0

MLA_HEADS = 8
MLA_NOPE_DIM = 64
MLA_ROPE_DIM = 32
MLA_V_DIM = 64
MLA_Q_RANK = 768
MLA_KV_RANK = 256

NSA_HEADS = 8
NSA_GROUPS = 2
NSA_HPG = NSA_HEADS // NSA_GROUPS
NSA_HEAD_DIM = 64
CMP_BLOCK = 32
CMP_STRIDE = 16
CMP_HIDDEN = 2 * NSA_HEAD_DIM
SEL_BLOCK = 64
SEL_TOPN = 16
WINDOW = 512
FORCE_SCORE = 1e9

PEER_HEADS = 8
PEER_KEY_DIM = 128
PEER_N_KEYS = 128
PEER_N_EXPERTS = PEER_N_KEYS * PEER_N_KEYS
PEER_TOPK = 16

DEEPNORM_ALPHA = (2 * DEPTH) ** 0.25
LN_EPS = 1e-5
RMS_EPS = 1e-6
NEG_BIG = -1e30
LOG2E = math.log2(math.e)
MASK_BIAS = 32768.0

IN_WIDTHS = (MLA_Q_RANK, MLA_KV_RANK, MLA_ROPE_DIM, NSA_HEADS * NSA_HEAD_DIM,
             6 * NSA_GROUPS * NSA_HEAD_DIM, 3 * NSA_HEADS, 2 * D_MODEL)

LANES = 128
VMEM_LIMIT = 56 * 1024 * 1024

F32 = jnp.float32
CDT = jnp.bfloat16

_NT = (((1,), (1,)), ((), ()))


def _dot(a, b):
    return jnp.dot(a, b, preferred_element_type=F32)


def _dot_nt(a, b):
    return lax.dot_general(a, b, _NT, preferred_element_type=F32)


def _params(*sem):
    return pltpu.CompilerParams(dimension_semantics=sem, vmem_limit_bytes=VMEM_LIMIT)


def _gelu(x):
    return 0.5 * x * (1.0 + lax.erf(x * (2.0 ** -0.5)))


def _layer_norm(z, g, b):
    mu = jnp.mean(z, axis=-1, keepdims=True)
    d = z - mu
    var = jnp.mean(d * d, axis=-1, keepdims=True)
    return d * lax.rsqrt(var + LN_EPS) * g + b


def _ln_in_kernel(x_ref, g_ref, b_ref, h_ref, hb_ref):
    h = _layer_norm(x_ref[...], g_ref[...], b_ref[...])
    h_ref[...] = h
    hb_ref[...] = h.astype(CDT)


def _ln_in(x, g, b, tm=512):
    T, D = x.shape
    row = pl.BlockSpec((tm, D), lambda i: (i, 0))
    vec = pl.BlockSpec((1, D), lambda i: (0, 0))
    return pl.pallas_call(
        _ln_in_kernel, grid=(T // tm,), in_specs=[row, vec, vec], out_specs=[row, row],
        out_shape=[jax.ShapeDtypeStruct((T, D), F32), jax.ShapeDtypeStruct((T, D), CDT)],
        compiler_params=_params("parallel"), name="ln_in")(x, g.reshape(1, D), b.reshape(1, D))


_PROJ_CHUNK = 512


def _proj_all_kernel(hb_ref, w_ref, cm_ref, sm_ref, cn_ref, sn_ref, hot_ref, one_ref, *out_refs, plan):
    hb = hb_ref[...]
    tables = {"mla": (cm_ref, sm_ref), "nsa": (cn_ref, sn_ref)}
    for o_ref, (off, width, rot_off, table, act, bias) in zip(out_refs, plan):
        for c0 in range(0, width, _PROJ_CHUNK):
            cw = min(_PROJ_CHUNK, width - c0)
            y = _dot(hb, w_ref[:, off + c0:off + c0 + cw])
            if rot_off is not None:
                yr = _dot(hb, w_ref[:, rot_off + c0:rot_off + c0 + cw])
                cos = jnp.concatenate([tables[table][0][...]] * (cw // LANES), axis=1)
                sin = jnp.concatenate([tables[table][1][...]] * (cw // LANES), axis=1)
                y = y * cos + yr * sin
            if act == "sigmoid":
                y = 1.0 / (1.0 + jnp.exp(-y))
            if bias == "block_onehot":
                y = y + hot_ref[:, c0:c0 + cw]
            elif bias == "ones_lane":
                y = y + one_ref[:, c0:c0 + cw]
            o_ref[:, c0:c0 + cw] = y.astype(o_ref.dtype)


def _proj_all(hb, lw, tab_mla, tab_nsa, blk_onehot, ones_lane, tm=512):
    segs = [("cqkv", lw["w_cqkv"], None, None, None, None, CDT),
            ("kr", lw["w_kr"], lw["w_kr_rot"], "mla", None, None, CDT),
            ("nq", lw["w_nq"], lw["w_nq_rot"], "nsa", None, None, CDT),
            ("kcmp", lw["w_kcmp"], lw["w_kcmp_rot"], "nsa", None, None, CDT),
            ("ksel", lw["w_ksel"], lw["w_ksel_rot"], "nsa", None, "block_onehot", CDT),
            ("kwin", lw["w_kwin"], lw["w_kwin_rot"], "nsa", None, None, CDT),
            ("vcmp", lw["w_vcmp"], None, None, None, None, CDT),
            ("vsel", lw["w_vsel"], None, None, None, "ones_lane", CDT),
            ("vwin", lw["w_vwin"], None, None, None, "ones_lane", CDT),
            ("ngate", lw["w_ngate"], None, None, "sigmoid", None, F32),
            ("mgate", lw["w_mgate"], None, None, "sigmoid", None, CDT)]
    T, K = hb.shape
    cols, plan, off = [], [], 0
    for _, w, w_rot, table, act, bias, _ in segs:
        width = w.shape[1]
        cols.append(w)
        rot_off = None
        if w_rot is not None:
            cols.append(w_rot)
            rot_off = off + width
        plan.append((off, width, rot_off, table, act, bias))
        off += width * (2 if w_rot is not None else 1)
    assert T % tm == 0 and blk_onehot.shape[0] % tm == 0
    w_all = jnp.concatenate(cols, axis=1)
    row = lambda n: pl.BlockSpec((tm, n), lambda i: (i, 0))
    full = lambda a: pl.BlockSpec(a.shape, lambda i: (0, 0))
    outs = pl.pallas_call(
        functools.partial(_proj_all_kernel, plan=tuple(plan)), grid=(T // tm,),
        in_specs=[row(K), full(w_all), row(LANES), row(LANES), row(LANES), row(LANES),
                  pl.BlockSpec((tm, blk_onehot.shape[1]), lambda i: (i % (blk_onehot.shape[0] // tm), 0)),
                  full(ones_lane)],
        out_specs=[row(s[1].shape[1]) for s in segs],
        out_shape=[jax.ShapeDtypeStruct((T, s[1].shape[1]), s[6]) for s in segs],
        compiler_params=_params("parallel"), name="proj_all")(
            hb, w_all, tab_mla[0], tab_mla[1], tab_nsa[0], tab_nsa[1], blk_onehot, ones_lane)
    return dict(zip([s[0] for s in segs], outs))


def _rms(x, g):
    return x * lax.rsqrt(jnp.mean(x * x, axis=-1, keepdims=True) + RMS_EPS) * g


def _mla_up_kernel(c_ref, kr_ref, qg_ref, kvg_ref, wqn_ref, wqr_ref, wqrr_ref, pq_ref,
                   wkn_ref, wv_ref, pk_ref, cos_ref, sin_ref, q_ref, k_ref, v_ref):
    c = c_ref[...].astype(F32)
    cqn = _rms(c[:, :MLA_Q_RANK], qg_ref[...]).astype(CDT)
    ckvn = _rms(c[:, MLA_Q_RANK:], kvg_ref[...]).astype(CDT)
    cos = jnp.concatenate([cos_ref[...]] * 2, axis=1)
    sin = jnp.concatenate([sin_ref[...]] * 2, axis=1)
    roped = (_dot(cqn, wqr_ref[...]) * cos + _dot(cqn, wqrr_ref[...]) * sin).astype(CDT)
    q_ref[...] = (_dot(cqn, wqn_ref[...]) + _dot(roped, pq_ref[...])).astype(CDT)
    k_ref[...] = (_dot(ckvn, wkn_ref[...]) + _dot(kr_ref[...], pk_ref[...])).astype(CDT)
    v_ref[...] = _dot(ckvn, wv_ref[...]).astype(CDT)


def _mla_up(cqkv, kr, lw, tables, tm=512):
    T = cqkv.shape[0]
    row = lambda n: pl.BlockSpec((tm, n), lambda i: (i, 0))
    full = lambda a: pl.BlockSpec(a.shape, lambda i: (0, 0))
    ws = (lw["q_gain"], lw["kv_gain"], lw["wqn"], lw["wqr"], lw["wqr_rot"], lw["place_q"],
          lw["wkn"], lw["wv"], lw["place_k"])
    hp = MLA_HEADS * LANES
    return pl.pallas_call(
        _mla_up_kernel, grid=(T // tm,),
        in_specs=[row(cqkv.shape[1]), row(LANES)] + [full(a) for a in ws] + [row(LANES), row(LANES)],
        out_specs=[row(hp), row(hp), row(MLA_HEADS * MLA_V_DIM)],
        out_shape=[jax.ShapeDtypeStruct((T, hp), CDT), jax.ShapeDtypeStruct((T, hp), CDT),
                   jax.ShapeDtypeStruct((T, MLA_HEADS * MLA_V_DIM), CDT)],
        compiler_params=_params("parallel"), name="mla_up")(cqkv, kr, *ws, tables[0], tables[1])


def _mla_attn_kernel(q_ref, k_ref, v_ref, o_ref, *, tq, scale, nh):
    i = pl.program_id(2)
    qs = [q_ref[0, :, hd * LANES:(hd + 1) * LANES] for hd in range(nh)]

    def step(kt, carries, diagonal):
        k0 = pl.multiple_of(kt * tq, tq)
        new = []
        for hd, (m, l, acc) in enumerate(carries):
            k = k_ref[0, pl.ds(k0, tq), hd * LANES:(hd + 1) * LANES]
            v = v_ref[0, pl.ds(k0, tq), (hd // 2) * LANES:(hd // 2 + 1) * LANES]
            s = _dot_nt(qs[hd], k) * scale
            if diagonal:
                row = lax.broadcasted_iota(jnp.int32, (tq, tq), 0)
                col = lax.broadcasted_iota(jnp.int32, (tq, tq), 1)
                s = jnp.where(col <= row, s, NEG_BIG)
            m_new = jnp.maximum(m, jnp.max(s, axis=1, keepdims=True))
            alpha = jnp.exp(m - m_new)
            p = jnp.exp(s - m_new)
            l = l * alpha + jnp.sum(p, axis=1, keepdims=True)
            acc = acc * alpha + _dot(p.astype(CDT), v)
            new.append((m_new, l, acc))
        return tuple(new)

    one = (jnp.full((tq, 1), NEG_BIG, F32), jnp.zeros((tq, 1), F32), jnp.zeros((tq, LANES), F32))
    carries = lax.fori_loop(0, i, lambda kt, c: step(kt, c, False), (one,) * nh)
    outs = [acc / l for _, l, acc in step(i, carries, True)]
    lane = lax.broadcasted_iota(jnp.int32, (tq, LANES), 1)
    for pair in range(nh // 2):
        o = jnp.where(lane < MLA_V_DIM, outs[2 * pair], outs[2 * pair + 1])
        o_ref[0, :, pair * LANES:(pair + 1) * LANES] = o.astype(o_ref.dtype)


def _mla_attn(q, k, v, tq=512, nh=4):
    B, S, _ = q.shape
    scale = (MLA_NOPE_DIM + MLA_ROPE_DIM) ** -0.5
    kern = functools.partial(_mla_attn_kernel, tq=tq, scale=scale, nh=nh)
    vw = nh * MLA_V_DIM
    return pl.pallas_call(
        kern, grid=(B, MLA_HEADS // nh, S // tq),
        in_specs=[pl.BlockSpec((1, tq, nh * LANES), lambda b, h, i: (b, i, h)),
                  pl.BlockSpec((1, S, nh * LANES), lambda b, h, i: (b, 0, h)),
                  pl.BlockSpec((1, S, vw), lambda b, h, i: (b, 0, h))],
        out_specs=pl.BlockSpec((1, tq, vw), lambda b, h, i: (b, i, h)),
        out_shape=jax.ShapeDtypeStruct((B, S, MLA_HEADS * MLA_V_DIM), CDT),
        compiler_params=_params("parallel", "parallel", "arbitrary"), name="mla_attn")(q, k, v)


def _compress_kernel(c_ref, pe_ref, w1a_ref, w1b_ref, w2_ref, o_ref, *, n_valid):
    c = c_ref[0].astype(F32)
    n = c.shape[0]
    a = _dot((c + pe_ref[0:1, :]).astype(CDT), w1a_ref[...])
    b = _dot((c + pe_ref[1:2, :]).astype(CDT), w1b_ref[...])
    hid = _gelu(a + pltpu.roll(b, n - 1, 0))
    out = _dot(hid.astype(CDT), w2_ref[...])
    row = lax.broadcasted_iota(jnp.int32, out.shape, 0)
    o_ref[0] = jnp.where(row < n_valid, out, 0.0).astype(o_ref.dtype)


def _compress(chunks, pe, w1, w2):
    BG, n, width = chunks.shape
    pe2 = pe.reshape(2, width).astype(F32)
    w1a = w1[:width].astype(CDT)
    w1b = w1[width:].astype(CDT)
    w2p = jnp.pad(w2, ((0, 0), (0, LANES - w2.shape[1]))).astype(CDT)
    full = lambda a: pl.BlockSpec(a.shape, lambda i: (0, 0))
    return pl.pallas_call(
        functools.partial(_compress_kernel, n_valid=n - 1), grid=(BG,),
        in_specs=[pl.BlockSpec((1, n, width), lambda i: (i, 0, 0)), full(pe2), full(w1a), full(w1b), full(w2p)],
        out_specs=pl.BlockSpec((1, n, LANES), lambda i: (i, 0, 0)),
        out_shape=jax.ShapeDtypeStruct((BG, n, LANES), CDT),
        compiler_params=_params("parallel"), name="nsa_compress")(chunks, pe2, w1a, w1b, w2p)


def _nsa_kernel(q_ref, kc_ref, vc_ref, ks_ref, vs_ref, kw_ref, vw_ref, g_ref, ovl_ref, edge_ref, o_ref,
                *, tq, n_sel, n_cmp, scale):
    i = pl.program_id(2)
    q0 = i * tq
    R = NSA_HPG
    ncp = kc_ref.shape[1]
    Q = jnp.concatenate([q_ref[0, :, r * LANES:(r + 1) * LANES] for r in range(R)], axis=0)

    scale2 = scale * LOG2E
    hidden = -MASK_BIAS * scale2
    t_c = q0 + lax.broadcasted_iota(jnp.int32, (tq, ncp), 0)
    n_c = lax.broadcasted_iota(jnp.int32, (tq, ncp), 1)
    bias_c = jnp.where((n_c * CMP_STRIDE + (CMP_BLOCK - 1) <= t_c) & (n_c < n_cmp), 0.0, hidden)
    s = (_dot_nt(Q, kc_ref[0]) * scale2).reshape(R, tq, ncp) + bias_c[None]
    p = jnp.exp2(s - jnp.max(s, axis=2, keepdims=True))
    t_q = q0 + lax.broadcasted_iota(jnp.int32, (tq, 1), 0)
    sees_any = jnp.where(t_q >= CMP_BLOCK - 1, 1.0, 0.0)
    p = p * (sees_any / jnp.sum(p, axis=2, keepdims=True))
    o_c = _dot(p.reshape(R * tq, ncp).astype(CDT), vc_ref[0])

    psum = jnp.sum(p, axis=0)
    p_hi = psum.astype(CDT)
    p_lo = (psum - p_hi.astype(F32)).astype(CDT)
    imp = (_dot_nt(ovl_ref[...], p_hi) + _dot_nt(ovl_ref[...], p_lo))[0:n_sel]
    jj = lax.broadcasted_iota(jnp.int32, (n_sel, tq), 0)
    cur = (q0 + lax.broadcasted_iota(jnp.int32, (n_sel, tq), 1)) // SEL_BLOCK
    forced = (jj == 0) | (jj == cur) | (jj == cur - 1)
    top_n = min(SEL_TOPN, n_sel)
    assert top_n > 3
    free = jnp.where(forced | (jj > cur), -FORCE_SCORE, imp)
    tops, v = [], free
    for _ in range(top_n - 1):
        m = jnp.max(v, axis=0, keepdims=True)
        tops.append(m)
        v = jnp.where(v == m, -jnp.inf, v)
    cur_row = cur[0:1, :]
    tau = jnp.where(cur_row >= 2, tops[top_n - 4], jnp.where(cur_row == 1, tops[top_n - 3], tops[top_n - 2]))
    half = LANES // 2
    parts = [jnp.zeros((half, tq), F32), jnp.where(forced | (free >= tau), 0.0, -MASK_BIAS)]
    if n_sel < half:
        parts.append(jnp.zeros((half - n_sel, tq), F32))
    q_bias = jnp.concatenate(parts, axis=0).T.astype(CDT)
    Qb = Q + jnp.concatenate([q_bias] * R, axis=0)

    causal_bias = edge_ref[0]
    window_bias = edge_ref[1]

    def step(k_ref, v_ref, kt, carry, bias):
        m, acc = carry
        k0 = pl.multiple_of(kt * tq, tq)
        s = (_dot_nt(Qb, k_ref[0, pl.ds(k0, tq), :]) * scale2).reshape(R, tq, tq)
        if bias is not None:
            s = s + bias[None]
        m_new = jnp.maximum(m, jnp.max(s, axis=2, keepdims=True))
        p = jnp.exp2(s - m_new).reshape(R * tq, tq).astype(CDT)
        pv = _dot(p, v_ref[0, pl.ds(k0, tq), :]).reshape(R, tq, LANES)
        return m_new, acc * jnp.exp2(m - m_new) + pv

    def finish(carry):
        acc = carry[1]
        return acc * (1.0 / acc[:, :, half:half + 1])

    def flag(cond):
        return jnp.where(cond, 1.0, 0.0)

    init = (jnp.full((R, tq, 1), NEG_BIG, F32), jnp.zeros((R, tq, LANES), F32))

    carry = lax.fori_loop(0, i, lambda kt, c: step(ks_ref, vs_ref, kt, c, None), init)
    o_s = finish(lax.fori_loop(i, i + 1, lambda kt, c: step(ks_ref, vs_ref, kt, c, causal_bias), carry))

    n_back = WINDOW // tq

    def win_step(it, c):
        kt = i - n_back + it
        bias = window_bias * flag(it == 0) + causal_bias * flag(it == n_back) + hidden * flag(kt < 0)
        return step(kw_ref, vw_ref, jnp.maximum(kt, 0), c, bias)

    o_w = finish(lax.fori_loop(0, n_back + 1, win_step, init))

    g = g_ref[0]
    o_c = o_c.reshape(R, tq, LANES)
    for r in range(R):
        o = (g[:, 3 * r:3 * r + 1] * o_c[r] + g[:, 3 * r + 1:3 * r + 2] * o_s[r]
             + g[:, 3 * r + 2:3 * r + 3] * o_w[r])
        o_ref[0, :, r * LANES:(r + 1) * LANES] = o.astype(o_ref.dtype)


def _nsa(q, kc, vc, ks, vs, kw, vw, gates, tq=512):
    B, S, _ = q.shape
    G = NSA_GROUPS
    n_sel = S // SEL_BLOCK
    n_cmp = (S - CMP_BLOCK) // CMP_STRIDE + 1
    ncp = kc.shape[1]
    assert n_sel <= LANES // 2 and n_sel % 8 == 0 and ncp % LANES == 0 and WINDOW % tq == 0
    jn = np.arange(LANES)[:, None] * SEL_BLOCK
    cn = np.arange(ncp)[None, :] * CMP_STRIDE
    ovl = ((cn <= jn + SEL_BLOCK - 1) & (cn + CMP_BLOCK - 1 >= jn)
           & (np.arange(LANES)[:, None] < n_sel) & (np.arange(ncp)[None, :] < n_cmp))
    ovl = jnp.asarray(ovl, CDT)
    scale = NSA_HEAD_DIM ** -0.5
    hidden = -MASK_BIAS * scale * LOG2E
    row, col = np.arange(tq)[:, None], np.arange(tq)[None, :]
    edge = jnp.asarray(np.stack([np.where(col <= row, 0.0, hidden), np.where(col > row, 0.0, hidden)]), F32)
    kern = functools.partial(_nsa_kernel, tq=tq, n_sel=n_sel, n_cmp=n_cmp, scale=scale)
    cmp_spec = pl.BlockSpec((1, ncp, LANES), lambda b, g, i: (b * G + g, 0, 0))
    kv_spec = pl.BlockSpec((1, S, LANES), lambda b, g, i: (b, 0, g))
    qo_spec = pl.BlockSpec((1, tq, NSA_HPG * LANES), lambda b, g, i: (b, i, g))
    return pl.pallas_call(
        kern, grid=(B, G, S // tq),
        in_specs=[qo_spec, cmp_spec, cmp_spec, kv_spec, kv_spec, kv_spec, kv_spec,
                  pl.BlockSpec((1, tq, LANES), lambda b, g, i: (b, i, g)),
                  pl.BlockSpec(ovl.shape, lambda b, g, i: (0, 0)),
                  pl.BlockSpec(edge.shape, lambda b, g, i: (0, 0, 0))],
        out_specs=qo_spec,
        out_shape=jax.ShapeDtypeStruct((B, S, NSA_HEADS * LANES), CDT),
        compiler_params=_params("parallel", "parallel", "arbitrary"), name="nsa_attn")(
            q, kc, vc, ks, vs, kw, vw, gates, ovl, edge)


def _merge_kernel(om_ref, on_ref, mg_ref, h_ref, wbm_ref, wbn_ref, wo_ref, g_ref, b_ref, x_ref, xb_ref):
    D = D_MODEL
    y_mla = _dot(om_ref[...], wbm_ref[...])
    y_nsa = _dot(on_ref[...], wbn_ref[...])
    mg = mg_ref[...].astype(F32)
    mixed = _dot((mg[:, :D] * y_mla + mg[:, D:] * y_nsa).astype(CDT), wo_ref[...])
    x = _layer_norm(DEEPNORM_ALPHA * h_ref[...] + mixed, g_ref[...], b_ref[...])
    x_ref[...] = x
    xb_ref[...] = x.astype(CDT)


def _merge(o_mla, o_nsa, mg, h, lw, tm=512):
    T, D = h.shape
    row = lambda n: pl.BlockSpec((tm, n), lambda i: (i, 0))
    full = lambda a: pl.BlockSpec(a.shape, lambda i: (0, 0))
    ws = (lw["w_bm"], lw["w_bn"], lw["w_out"], lw["ln1_g"], lw["ln1_b"])
    return pl.pallas_call(
        _merge_kernel, grid=(T // tm,),
        in_specs=[row(o_mla.shape[1]), row(o_nsa.shape[1]), row(2 * D), row(D)] + [full(a) for a in ws],
        out_specs=[row(D), row(D)],
        out_shape=[jax.ShapeDtypeStruct((T, D), F32), jax.ShapeDtypeStruct((T, D), CDT)],
        compiler_params=_params("parallel"), name="merge_ln1")(o_mla, o_nsa, mg, h, *ws)


def _rows_per_vreg():
    return 8 * (4 // jnp.dtype(CDT).itemsize)


def _pack_words(x):
    if jnp.dtype(CDT).itemsize == 4:
        return pltpu.bitcast(x, jnp.uint32)
    bits = pltpu.bitcast(x.astype(CDT).astype(F32), jnp.uint32)
    return (bits >> 16) | (bits & jnp.uint32(0xFFFF0000))


def _unpack_words(row):
    return pltpu.bitcast(jnp.broadcast_to(row, (8, LANES)), CDT)


_PEER_PAIRS = [(a, b) for a in range(PEER_TOPK) for b in range(PEER_TOPK) if (a + 1) * (b + 1) <= PEER_TOPK]
_N_CAND = -(-len(_PEER_PAIRS) // 8) * 8
HEADS_PER_TRIP = 2


def _top_values(v, want_rank):
    tops = []
    rank = jnp.full(v.shape, 127.0, F32) if want_rank else None
    for r in range(PEER_TOPK):
        m = jnp.max(v, axis=0, keepdims=True)
        tops.append(m)
        eq = v == m
        if want_rank:
            rank = jnp.where(eq, float(r), rank)
        v = jnp.where(eq, -jnp.inf, v)
    return tops, rank


def _peer_stats_kernel(xb_ref, wq_ref, keys_ref, c1_ref, e1_ref, r2_ref, e2_ref, s_ref, cand_ref):
    qp = _dot(xb_ref[...], wq_ref[...]).astype(CDT)
    n_k = PEER_N_KEYS
    for g in range(2 * PEER_HEADS):
        s_ref[g * n_k:(g + 1) * n_k, :] = _dot_nt(keys_ref[g], qp[:, g * LANES:(g + 1) * LANES])
    n = PEER_N_KEYS
    cand_ref[...] = jnp.full(cand_ref.shape, -jnp.inf, F32)

    def one_head(h, cand_ref):
        s1 = s_ref[pl.ds(pl.multiple_of(2 * h * n, n), n), :]
        s2 = s_ref[pl.ds(pl.multiple_of((2 * h + 1) * n, n), n), :]
        t1, _ = _top_values(s1, False)
        t2, rank2 = _top_values(s2, True)
        for c, (a, b) in enumerate(_PEER_PAIRS):
            cand_ref[c:c + 1, :] = t1[a] + t2[b]
        cand = cand_ref[...]
        tc, _ = _top_values(cand, False)
        tau = tc[PEER_TOPK - 1]
        top = t1[0] + t2[0]
        z = jnp.sum(jnp.where(cand >= tau, jnp.exp(cand - top), 0.0), axis=0, keepdims=True)
        cnt = jnp.zeros(s1.shape, F32)
        for b in range(PEER_TOPK // 2):
            cnt = cnt + jnp.where(s1 + t2[b] >= tau, 1.0, 0.0)
        best_extra = jnp.zeros(tau.shape, F32)
        for b in range(PEER_TOPK // 2, PEER_TOPK):
            best_extra = best_extra + jnp.where(t1[0] + t2[b] >= tau, 1.0, 0.0)
        cnt = cnt + jnp.where(s1 == t1[0], best_extra, 0.0)
        c1_ref[h] = _pack_words(cnt)
        e1_ref[h] = _pack_words(jnp.exp(s1 - t1[0]))
        r2_ref[h] = rank2.astype(r2_ref.dtype)
        e2_ref[h] = (jnp.exp(s2 - t2[0]) / z).astype(e2_ref.dtype)

    def heads(idx, carry):
        for k in range(HEADS_PER_TRIP):
            one_head(idx * HEADS_PER_TRIP + k, cand_ref.at[k])
        return carry

    lax.fori_loop(0, PEER_HEADS // HEADS_PER_TRIP, heads, 0)


def _peer_stats(xb, wq, keys_bd_t, tm=256):
    T, D = xb.shape
    H, n = PEER_HEADS, PEER_N_KEYS
    full = lambda a: pl.BlockSpec(a.shape, lambda i: (0,) * a.ndim)
    o_spec = pl.BlockSpec((H, n, tm), lambda i: (0, 0, i))
    o_u32 = jax.ShapeDtypeStruct((H, n, T), jnp.uint32)
    o_cdt = jax.ShapeDtypeStruct((H, n, T), CDT)
    return pl.pallas_call(
        _peer_stats_kernel, grid=(T // tm,),
        in_specs=[pl.BlockSpec((tm, D), lambda i: (i, 0)), full(wq), full(keys_bd_t)],
        out_specs=[o_spec] * 4, out_shape=[o_u32, o_u32, o_cdt, o_cdt],
        scratch_shapes=[pltpu.VMEM((2 * H * n, tm), F32), pltpu.VMEM((HEADS_PER_TRIP, _N_CAND, tm), F32)],
        compiler_params=_params("parallel"), name="peer_stats")(xb, wq, keys_bd_t)


def _peer_dense_kernel(xb_ref, x_ref, u_ref, vt_ref, c1_ref, e1_ref, r2_in_ref, e2_in_ref, g_ref, b_ref,
                       o_ref, ob_ref, acc_ref, a_ref, h_ref, r2_ref, e2_ref, *, tm, te):
    j = pl.program_id(1)
    n = PEER_N_KEYS
    rows_per_tile = te // n

    @pl.when(j == 0)
    def _():
        acc_ref[...] = jnp.zeros(acc_ref.shape, F32)
        r2_ref[...] = r2_in_ref[...]
        e2_ref[...] = e2_in_ref[...]

    a_ref[...] = _gelu(_dot_nt(u_ref[...], xb_ref[...])).astype(CDT)
    ic = n // 2
    sub = _rows_per_vreg()
    nv = ic // sub
    zero = jnp.zeros((nv, sub, LANES), CDT)
    n_i2 = n // ic

    def gate_block(idx):
        sl = pl.ds((idx // n_i2) * LANES, LANES)
        i2 = (idx % n_i2) * ic
        ws = [zero] * rows_per_tile
        for h in range(PEER_HEADS):
            r2 = r2_ref[h, pl.ds(i2, ic), sl].reshape(nv, sub, LANES)
            e2 = e2_ref[h, pl.ds(i2, ic), sl].reshape(nv, sub, LANES)
            for r in range(rows_per_tile):
                c1 = _unpack_words(c1_ref[h, r:r + 1, sl])[None]
                e1 = _unpack_words(e1_ref[h, r:r + 1, sl])[None]
                ws[r] = ws[r] + jnp.maximum(jnp.minimum(e1 * e2, c1 - r2), zero)
        for r in range(rows_per_tile):
            rows = pl.ds(r * n + i2, ic)
            h_ref[rows, sl] = ws[r].reshape(ic, LANES) * a_ref[rows, sl]

    for idx in range((tm // LANES) * n_i2):
        gate_block(idx)
    acc_ref[...] += _dot(vt_ref[...], h_ref[...])

    @pl.when(j == pl.num_programs(1) - 1)
    def _():
        z = DEEPNORM_ALPHA * x_ref[...] + acc_ref[...].T
        out = _layer_norm(z, g_ref[...], b_ref[...])
        o_ref[...] = out
        ob_ref[...] = out.astype(CDT)


def _peer_dense(xb, x, u, vt, stats, ln_g, ln_b, tm=512, te=1024):
    T, D = x.shape
    E = u.shape[0]
    H, n = PEER_HEADS, PEER_N_KEYS
    row = pl.BlockSpec((tm, D), lambda i, j: (i, 0))
    st = pl.BlockSpec((H, n, tm), lambda i, j: (0, 0, i))
    st1 = pl.BlockSpec((H, te // n, tm), lambda i, j: (0, j, i))
    vec = pl.BlockSpec((1, D), lambda i, j: (0, 0))
    assert (te // n) % 8 == 0
    kern = functools.partial(_peer_dense_kernel, tm=tm, te=te)
    return pl.pallas_call(
        kern, grid=(T // tm, E // te),
        in_specs=[row, row, pl.BlockSpec((te, D), lambda i, j: (j, 0)), pl.BlockSpec((D, te), lambda i, j: (0, j)),
                  st1, st1, st, st, vec, vec],
        out_specs=[row, row],
        out_shape=[jax.ShapeDtypeStruct((T, D), F32), jax.ShapeDtypeStruct((T, D), CDT)],
        scratch_shapes=[pltpu.VMEM((D, tm), F32), pltpu.VMEM((te, tm), CDT), pltpu.VMEM((te, tm), CDT),
                        pltpu.VMEM((H, n, tm), CDT), pltpu.VMEM((H, n, tm), CDT)],
        compiler_params=_params("parallel", "arbitrary"), name="peer_dense")(
            xb, x, u, vt, *stats, ln_g, ln_b)


def _rot_cols(w, dim):
    k, n = w.shape
    w4 = w.reshape(k, n // dim, 2, dim // 2)
    return jnp.stack([-w4[:, :, 1], w4[:, :, 0]], axis=2).reshape(k, n)


def _pad_heads(w, dim):
    k, n = w.shape
    w3 = w.reshape(k, n // dim, dim)
    return jnp.pad(w3, ((0, 0), (0, 0), (0, LANES - dim))).reshape(k, (n // dim) * LANES)


def _pad_cols(w, n):
    return jnp.pad(w, ((0, 0), (0, n - w.shape[1])))


def _layer_weights(l, w_in, mla_q_norm, mla_w_uq, mla_kv_norm, mla_w_ukv, w_branch_mla, w_branch_nsa, w_out,
                   ln1_g, ln1_b, peer_w_query, peer_sub_keys, peer_u, peer_v, ln2_g, ln2_b):
    off = np.cumsum((0,) + IN_WIDTHS)
    wi = w_in[l]
    seg = lambda k: wi[:, off[k]:off[k + 1]]
    G, dk = NSA_GROUPS, NSA_HEAD_DIM
    kv = seg(4).reshape(D_MODEL, 6, G * dk)
    lw = {}
    lw["w_cqkv"] = jnp.concatenate([seg(0), seg(1)], axis=1).astype(CDT)
    kr = seg(2)
    lw["w_kr"] = _pad_cols(kr, LANES).astype(CDT)
    lw["w_kr_rot"] = _pad_cols(_rot_cols(kr, MLA_ROPE_DIM), LANES).astype(CDT)
    lw["w_nq"] = _pad_heads(seg(3), dk).astype(CDT)
    lw["w_nq_rot"] = _pad_heads(_rot_cols(seg(3), dk), dk).astype(CDT)
    lw["w_kcmp"] = kv[:, 0].astype(CDT)
    lw["w_kcmp_rot"] = _rot_cols(kv[:, 0], dk).astype(CDT)
    for name, idx in (("ksel", 2), ("kwin", 4)):
        lw["w_" + name] = _pad_heads(kv[:, idx], dk).astype(CDT)
        lw["w_" + name + "_rot"] = _pad_heads(_rot_cols(kv[:, idx], dk), dk).astype(CDT)
    lw["w_vcmp"] = kv[:, 1].astype(CDT)
    lw["w_vsel"] = _pad_heads(kv[:, 3], dk).astype(CDT)
    lw["w_vwin"] = _pad_heads(kv[:, 5], dk).astype(CDT)
    ng = seg(5).reshape(D_MODEL, G, NSA_HPG * 3)
    lw["w_ngate"] = jnp.pad(ng, ((0, 0), (0, 0), (0, LANES - NSA_HPG * 3))).reshape(D_MODEL, G * LANES).astype(CDT)
    lw["w_mgate"] = seg(6).astype(CDT)

    H = MLA_HEADS
    dq = MLA_NOPE_DIM + MLA_ROPE_DIM
    uq = mla_w_uq[l].reshape(MLA_Q_RANK, H, dq)
    lw["wqn"] = _pad_heads(uq[:, :, :MLA_NOPE_DIM].reshape(MLA_Q_RANK, -1), MLA_NOPE_DIM).astype(CDT)
    wqr = uq[:, :, MLA_NOPE_DIM:].reshape(MLA_Q_RANK, H * MLA_ROPE_DIM)
    lw["wqr"] = wqr.astype(CDT)
    lw["wqr_rot"] = _rot_cols(wqr, MLA_ROPE_DIM).astype(CDT)
    pq = np.zeros((H * MLA_ROPE_DIM, H * LANES), np.float32)
    pk = np.zeros((LANES, H * LANES), np.float32)
    for h in range(H):
        for j in range(MLA_ROPE_DIM):
            pq[h * MLA_ROPE_DIM + j, h * LANES + MLA_NOPE_DIM + j] = 1.0
            pk[j, h * LANES + MLA_NOPE_DIM + j] = 1.0
    lw["place_q"] = jnp.asarray(pq, CDT)
    lw["place_k"] = jnp.asarray(pk, CDT)
    ukv = mla_w_ukv[l].reshape(MLA_KV_RANK, H, MLA_NOPE_DIM + MLA_V_DIM)
    lw["wkn"] = _pad_heads(ukv[:, :, :MLA_NOPE_DIM].reshape(MLA_KV_RANK, -1), MLA_NOPE_DIM).astype(CDT)
    lw["wv"] = ukv[:, :, MLA_NOPE_DIM:].reshape(MLA_KV_RANK, H * MLA_V_DIM).astype(CDT)
    lw["q_gain"] = mla_q_norm[l].reshape(1, -1)
    lw["kv_gain"] = mla_kv_norm[l].reshape(1, -1)

    lw["w_bm"] = w_branch_mla[l].astype(CDT)
    wbn = w_branch_nsa[l].reshape(NSA_HEADS, dk, D_MODEL)
    lw["w_bn"] = jnp.pad(wbn, ((0, 0), (0, LANES - dk), (0, 0))).reshape(NSA_HEADS * LANES, D_MODEL).astype(CDT)
    lw["w_out"] = w_out[l].astype(CDT)
    lw["ln1_g"] = ln1_g[l].reshape(1, -1)
    lw["ln1_b"] = ln1_b[l].reshape(1, -1)

    lw["w_pq"] = _pad_heads(peer_w_query[l], PEER_KEY_DIM // 2).astype(CDT)
    sk = peer_sub_keys[l].reshape(PEER_HEADS * 2, PEER_N_KEYS, PEER_KEY_DIM // 2)
    lw["keys_bd_t"] = jnp.pad(sk, ((0, 0), (0, 0), (0, LANES - PEER_KEY_DIM // 2))).astype(CDT)
    lw["u"] = peer_u[l].astype(CDT)
    lw["vt"] = peer_v[l].T.astype(CDT)
    lw["ln2_g"] = ln2_g[l].reshape(1, -1)
    lw["ln2_b"] = ln2_b[l].reshape(1, -1)
    return lw


def _rope_tables(positions, dim):
    inv_freq = ROPE_THETA ** (-jnp.arange(0, dim, 2, dtype=F32) / dim)
    ang = positions.astype(F32).reshape(-1, 1) * inv_freq
    rep = lambda t: jnp.tile(jnp.concatenate([t, t], axis=-1), (1, LANES // dim))
    return rep(jnp.cos(ang)), rep(jnp.sin(ang))


def _hybrid_layer(h, hb, B, S, lw, cmp_w, tab_mla, tab_nsa):
    T = B * S
    G, dk = NSA_GROUPS, NSA_HEAD_DIM
    lane = jnp.arange(G * LANES) % LANES - LANES // 2
    blk_onehot = (jnp.arange(S)[:, None] // SEL_BLOCK == lane[None, :]).astype(F32)
    ones_lane = np.zeros((1, G * LANES), np.float32)
    ones_lane[0, [g * LANES + dk for g in range(G)]] = 1.0
    p = _proj_all(hb, lw, tab_mla, tab_nsa, blk_onehot, jnp.asarray(ones_lane))
    cqkv, kr, nq, kcmp, ksel, kwin = p["cqkv"], p["kr"], p["nq"], p["kcmp"], p["ksel"], p["kwin"]
    vcmp, vsel, vwin, ngate, mgate = p["vcmp"], p["vsel"], p["vwin"], p["ngate"], p["mgate"]

    q, k, v = _mla_up(cqkv, kr, lw, tab_mla)
    o_mla = _mla_attn(q.reshape(B, S, -1), k.reshape(B, S, -1), v.reshape(B, S, -1))

    def chunks(a):
        return a.reshape(B, S, G, dk).transpose(0, 2, 1, 3).reshape(B * G, S // CMP_STRIDE, CMP_STRIDE * dk)

    kc = _compress(chunks(kcmp), cmp_w["k_pe"], cmp_w["k_w1"], cmp_w["k_w2"])
    vc = _compress(chunks(vcmp), cmp_w["v_pe"], cmp_w["v_w1"], cmp_w["v_w2"])
    r3 = lambda a: a.reshape(B, S, -1)
    o_nsa = _nsa(r3(nq), kc, vc, r3(ksel), r3(vsel), r3(kwin), r3(vwin), r3(ngate))

    x1, x1b = _merge(o_mla.reshape(T, -1), o_nsa.reshape(T, -1), mgate, h, lw)
    stats = _peer_stats(x1b, lw["w_pq"], lw["keys_bd_t"])
    return _peer_dense(x1b, x1, lw["u"], lw["vt"], stats, lw["ln2_g"], lw["ln2_b"])


def kernel(x, positions, ln_in_g, ln_in_b, w_in, mla_q_norm, mla_w_uq, mla_kv_norm, mla_w_ukv, nsa_cmp_k_pe, nsa_cmp_k_w1, nsa_cmp_k_w2, nsa_cmp_v_pe, nsa_cmp_v_w1, nsa_cmp_v_w2, w_branch_mla, w_branch_nsa, w_out, ln1_g, ln1_b, peer_w_query, peer_sub_keys, peer_u, peer_v, ln2_g, ln2_b):
    B, S, D = x.shape
    tab_mla = _rope_tables(positions, MLA_ROPE_DIM)
    tab_nsa = _rope_tables(positions, NSA_HEAD_DIM)
    h, hb = _ln_in(x.reshape(B * S, D), ln_in_g, ln_in_b)
    for l in range(DEPTH):
        lw = _layer_weights(l, w_in, mla_q_norm, mla_w_uq, mla_kv_norm, mla_w_ukv, w_branch_mla, w_branch_nsa,
                            w_out, ln1_g, ln1_b, peer_w_query, peer_sub_keys, peer_u, peer_v, ln2_g, ln2_b)
        cmp_w = {"k_pe": nsa_cmp_k_pe[l], "k_w1": nsa_cmp_k_w1[l], "k_w2": nsa_cmp_k_w2[l],
                 "v_pe": nsa_cmp_v_pe[l], "v_w1": nsa_cmp_v_w1[l], "v_w2": nsa_cmp_v_w2[l]}
        h, hb = _hybrid_layer(h, hb, B, S, lw, cmp_w, tab_mla, tab_nsa)
    return h.reshape(B, S, D)
```
